```python
import jax, jax.numpy as jnp
from jax import lax
import numpy as np

D_MODEL = 1024
BATCH = 8
SEQ = 2048
DEPTH = 2

MLSTM_HEADS = 4
MLSTM_HEAD_DIM = 256
MLSTM_WIDTH = MLSTM_HEADS * MLSTM_HEAD_DIM
MLSTM_CHUNK = 128
CONV_WIDTH = 4
POOL_WINDOWS = (2, 4, 8, 16)
POOL_GROUPS = 4
POOL_GROUP_DIM = 128
POOL_WIDTH = POOL_GROUPS * POOL_GROUP_DIM
N_BRANCH = 2
SPLIT_POINTS = (
    MLSTM_WIDTH,
    2 * MLSTM_WIDTH,
    3 * MLSTM_WIDTH,
    4 * MLSTM_WIDTH,
    4 * MLSTM_WIDTH + MLSTM_HEADS,
    4 * MLSTM_WIDTH + 2 * MLSTM_HEADS,
    4 * MLSTM_WIDTH + 2 * MLSTM_HEADS + POOL_WIDTH,
)
N_IN = 4 * MLSTM_WIDTH + 2 * MLSTM_HEADS + POOL_WIDTH + N_BRANCH * D_MODEL
D_FF_DENSE = 2816
N_EXPERTS = 8
TOP_K = 2
D_FF_EXPERT = 3584
N_DENSE = (DEPTH + 1) // 2
N_MOE = DEPTH // 2
EPS = 1e-6

kernel_name = "hybrid_mlstm_pool_moe_adaln"


def rmsnorm(x, g):
    xf = x.astype(jnp.float32)
    y = xf * lax.rsqrt(jnp.mean(xf * xf, axis=-1, keepdims=True) + EPS)
    return (y * g.astype(jnp.float32)).astype(x.dtype)


def modulate(h, shift, scale):
    return h * (1 + scale[:, None, :]) + shift[:, None, :]


def causal_depthwise_conv(x, w):
    return lax.conv_general_dilated(
        x, w[:, None, :].astype(x.dtype), window_strides=(1,),
        padding=((CONV_WIDTH - 1, 0),), dimension_numbers=('NWC', 'WIO', 'NWC'),
        feature_group_count=x.shape[-1])


def mlstm_chunkwise(q, k, v, i_pre, f_pre):
    bsz, nh, t_len, dh = q.shape
    L = MLSTM_CHUNK
    nc = t_len // L
    f32 = jnp.float32
    q = q.astype(f32).reshape(bsz, nh, nc, L, dh) * (dh ** -0.5)
    k = k.astype(f32).reshape(bsz, nh, nc, L, dh)
    v = v.astype(f32).reshape(bsz, nh, nc, L, dh)
    log_f = jax.nn.log_sigmoid(f_pre.astype(f32)).reshape(bsz, nh, nc, L)
    log_i = i_pre.astype(f32).reshape(bsz, nh, nc, L)
    b = jnp.cumsum(log_f, axis=-1)
    b_tot = b[..., -1]
    a = b_tot[..., None] - b + log_i
    m_loc = jnp.max(a, axis=-1)
    w = jnp.exp(a - m_loc[..., None])
    c_loc = jnp.einsum('bhclv,bhclk->bhcvk', w[..., None] * v, k)
    n_loc = jnp.einsum('bhcl,bhclk->bhck', w, k)

    def step(carry, xs):
        c_st, n_st, m_st = carry
        bt, ml, cl, nl = xs
        m_new = jnp.maximum(bt + m_st, ml)
        s_old = jnp.exp(bt + m_st - m_new)
        s_loc = jnp.exp(ml - m_new)
        c_new = s_old[..., None, None] * c_st + s_loc[..., None, None] * cl
        n_new = s_old[..., None] * n_st + s_loc[..., None] * nl
        return (c_new, n_new, m_new), (c_st, n_st, m_st)

    init = (jnp.zeros((bsz, nh, dh, dh), f32), jnp.zeros((bsz, nh, dh), f32), jnp.zeros((bsz, nh), f32))
    xs = (jnp.moveaxis(b_tot, 2, 0), jnp.moveaxis(m_loc, 2, 0),
          jnp.moveaxis(c_loc, 2, 0), jnp.moveaxis(n_loc, 2, 0))
    _, (c_prev, n_prev, m_prev) = lax.scan(step, init, xs)
    c_prev = jnp.moveaxis(c_prev, 0, 2)
    n_prev = jnp.moveaxis(n_prev, 0, 2)
    m_prev = jnp.moveaxis(m_prev, 0, 2)

    causal = jnp.tril(jnp.ones((L, L), dtype=bool))
    d = jnp.where(causal, b[..., :, None] - b[..., None, :] + log_i[..., None, :], -jnp.inf)
    inter_log = b + m_prev[..., None]
    m_comb = jnp.maximum(inter_log, jnp.max(d, axis=-1))
    s = jnp.einsum('bhctd,bhcsd->bhcts', q, k) * jnp.exp(d - m_comb[..., None])
    w_inter = jnp.exp(inter_log - m_comb)
    num = (jnp.einsum('bhcts,bhcsd->bhctd', s, v)
           + w_inter[..., None] * jnp.einsum('bhcvk,bhctk->bhctv', c_prev, q))
    den = jnp.sum(s, axis=-1) + w_inter * jnp.einsum('bhck,bhctk->bhct', n_prev, q)
    den = jnp.maximum(jnp.abs(den), jnp.exp(-m_comb))
    h = num / den[..., None]
    return h.reshape(bsz, nh, t_len, dh)


def multiscale_pool(u):
    bsz, t_len, _ = u.shape
    uf = u.astype(jnp.float32).reshape(bsz, t_len, POOL_GROUPS, POOL_GROUP_DIM)
    cs = jnp.pad(jnp.cumsum(uf, axis=1), ((0, 0), (1, 0), (0, 0), (0, 0)))
    t = jnp.arange(t_len)[:, None]
    win = jnp.asarray(np.array(POOL_WINDOWS, dtype=np.int32))[None, :]
    lo = jnp.maximum(t + 1 - win, 0)
    g_idx = jnp.arange(POOL_GROUPS)[None, :]
    window_sum = cs[:, 1:] - cs[:, lo, g_idx]
    count = (t + 1 - lo).astype(jnp.float32)
    return window_sum / count[None, :, :, None] - uf


def token_mixer(h, w_in, conv_w, i_bias, f_bias, head_gain, pool_w, pool_scale, proj_a, proj_b, w_out):
    bsz, t_len, _ = h.shape
    z = h @ w_in.astype(h.dtype)
    q, k, v, o, ig, fg, u, gates = jnp.split(z, SPLIT_POINTS, axis=-1)
    qk = jax.nn.silu(causal_depthwise_conv(jnp.concatenate([q, k], axis=-1), conv_w))
    q, k = jnp.split(qk, 2, axis=-1)
    to_heads = lambda a: a.reshape(bsz, t_len, MLSTM_HEADS, MLSTM_HEAD_DIM).transpose(0, 2, 1, 3)
    h_a = mlstm_chunkwise(to_heads(q), to_heads(k), to_heads(v),
                          (ig + i_bias).transpose(0, 2, 1), (fg + f_bias).transpose(0, 2, 1))
    h_a = h_a * lax.rsqrt(jnp.mean(h_a * h_a, axis=-1, keepdims=True) + EPS)
    h_a = h_a.transpose(0, 2, 1, 3).reshape(bsz, t_len, MLSTM_WIDTH).astype(h.dtype)
    h_a = h_a * head_gain * jax.nn.sigmoid(o)
    pooled = multiscale_pool(u).astype(h.dtype)
    h_b = jnp.einsum('btgc,gcd->btgd', pooled, pool_w).reshape(bsz, t_len, POOL_WIDTH) * pool_scale
    gate_a, gate_b = jnp.split(jax.nn.sigmoid(gates), N_BRANCH, axis=-1)
    merged = gate_a * (h_a @ proj_a) + gate_b * (h_b @ proj_b)
    return merged @ w_out


def swiglu(h, w_gate, w_up, w_down):
    return (jax.nn.silu(h @ w_gate) * (h @ w_up)) @ w_down


def moe_swiglu(h, router_w, router_b, w_gate, w_up, w_down):
    logits = (h @ router_w).astype(jnp.float32) + router_b.astype(jnp.float32)
    top_val, top_idx = lax.top_k(logits, TOP_K)
    top_w = jax.nn.softmax(top_val, axis=-1)
    combine = jnp.sum(jax.nn.one_hot(top_idx, N_EXPERTS, dtype=jnp.float32) * top_w[..., None], axis=-2)
    combine = combine.astype(h.dtype)
    out = jnp.zeros_like(h)
    for e in range(N_EXPERTS):
        out = out + combine[..., e:e + 1] * swiglu(h, w_gate[e], w_up[e], w_down[e])
    return out


def setup_inputs(seed: int = 0) -> dict:
    key = jax.random.key(seed)
    ks = jax.random.split(key, 32)
    f32 = jnp.float32
    D, W, H, P = D_MODEL, MLSTM_WIDTH, MLSTM_HEADS, POOL_WIDTH
    nrm = lambda kk, shape, fan_in: jax.random.normal(kk, shape, f32) * (fan_in ** -0.5)
    noise = lambda kk, shape, s: s * jax.random.normal(kk, shape, f32)
    return {
        "x": jax.random.normal(ks[0], (BATCH, SEQ, D), f32),
        "c": jax.random.normal(ks[1], (BATCH, D), f32),
        "norm_mix": 1.0 + noise(ks[2], (DEPTH, D), 0.05),
        "norm_ffn": 1.0 + noise(ks[3], (DEPTH, D), 0.05),
        "w_ada": 0.5 * nrm(ks[4], (DEPTH, D, 6 * D), D),
        "b_ada": noise(ks[5], (DEPTH, 6 * D), 0.02),
        "w_in": nrm(ks[6], (DEPTH, D, N_IN), D),
        "conv_w": nrm(ks[7], (DEPTH, CONV_WIDTH, 2 * W), CONV_WIDTH),
        "i_bias": noise(ks[8], (DEPTH, H), 0.1),
        "f_bias": jnp.linspace(3.0, 6.0, H, dtype=f32)[None, :] + noise(ks[9], (DEPTH, H), 0.1),
        "head_gain": 1.0 + noise(ks[10], (DEPTH, W), 0.05),
        "pool_w": nrm(ks[11], (DEPTH, POOL_GROUPS, POOL_GROUP_DIM, POOL_GROUP_DIM), POOL_GROUP_DIM),
        "pool_scale": 1.0 + noise(ks[12], (DEPTH, P), 0.05),
        "proj_a": nrm(ks[13], (DEPTH, W, D), W),
        "proj_b": nrm(ks[14], (DEPTH, P, D), P),
        "w_out": nrm(ks[15], (DEPTH, D, D), D),
        "ffn_w_gate": nrm(ks[16], (N_DENSE, D, D_FF_DENSE), D),
        "ffn_w_up": nrm(ks[17], (N_DENSE, D, D_FF_DENSE), D),
        "ffn_w_down": nrm(ks[18], (N_DENSE, D_FF_DENSE, D), D_FF_DENSE),
        "router_w": nrm(ks[19], (N_MOE, D, N_EXPERTS), D),
        "router_b": noise(ks[20], (N_MOE, N_EXPERTS), 0.01),
        "moe_w_gate": nrm(ks[21], (N_MOE, N_EXPERTS, D, D_FF_EXPERT), D),
        "moe_w_up": nrm(ks[22], (N_MOE, N_EXPERTS, D, D_FF_EXPERT), D),
        "moe_w_down": nrm(ks[23], (N_MOE, N_EXPERTS, D_FF_EXPERT, D), D_FF_EXPERT),
        "final_norm": 1.0 + noise(ks[24], (D,), 0.05),
    }


def reference(x, c, norm_mix, norm_ffn, w_ada, b_ada, w_in, conv_w, i_bias, f_bias, head_gain,
              pool_w, pool_scale, proj_a, proj_b, w_out, ffn_w_gate, ffn_w_up, ffn_w_down,
              router_w, router_b, moe_w_gate, moe_w_up, moe_w_down, final_norm):
    c_act = jax.nn.silu(c)
    for l in range(DEPTH):
        mod = c_act @ w_ada[l] + b_ada[l]
        sh1, sc1, g1, sh2, sc2, g2 = jnp.split(mod, 6, axis=-1)
        h = modulate(rmsnorm(x, norm_mix[l]), sh1, sc1)
        x = x + g1[:, None, :] * token_mixer(h, w_in[l], conv_w[l], i_bias[l], f_bias[l], head_gain[l],
                                             pool_w[l], pool_scale[l], proj_a[l], proj_b[l], w_out[l])
        h = modulate(rmsnorm(x, norm_ffn[l]), sh2, sc2)
        j = l // 2
        if l % 2 == 0:
            f = swiglu(h, ffn_w_gate[j], ffn_w_up[j], ffn_w_down[j])
        else:
            f = moe_swiglu(h, router_w[j], router_b[j], moe_w_gate[j], moe_w_up[j], moe_w_down[j])
        x = x + g2[:, None, :] * f
    return rmsnorm(x, final_norm)
```

```python
import functools

import jax
import jax.numpy as jnp
from jax import lax
from jax.experimental import pallas as pl
from jax.experimental.pallas import tpu as pltpu

F32 = jnp.float32
BF16 = jnp.bfloat16

EPS = 1e-6
MLSTM_CHUNK = 128
POOL_WINDOWS = (2, 4, 8, 16)
TOP_K = 2
LANES = 128
SUBLANES = 8
VMEM_LIMIT = 56 * 1024 * 1024

_NT = (((1,), (1,)), ((), ()))
_TN = (((0,), (0,)), ((), ()))


def _params(sem):
    return pltpu.CompilerParams(dimension_semantics=sem, vmem_limit_bytes=VMEM_LIMIT)


def _sigmoid(x):
    return 1.0 / (1.0 + jnp.exp(-x))


def _log_sigmoid(x):
    return jnp.minimum(x, 0.0) - jnp.log(1.0 + jnp.exp(-jnp.abs(x)))


def _rms(x, g):
    return x * lax.rsqrt(jnp.mean(x * x, axis=-1, keepdims=True) + EPS) * g


def _norm_mod(x, g, shift, scale):
    return _rms(x, g) * (1.0 + scale) + shift


def _dot(a, b):
    return jnp.dot(a, b, preferred_element_type=F32)


def _dot_f32(a, b, dims=None):
    dims = dims or (((1,), (0,)), ((), ()))
    return lax.dot_general(a, b, dims, precision=lax.Precision.HIGHEST, preferred_element_type=F32)


def _adaln_kernel(c_ref, w_ref, b_ref, o_ref):
    c = c_ref[...]
    o_ref[...] = _dot_f32(c * _sigmoid(c), w_ref[...]) + b_ref[...]


def _adaln(c, w_ada, b_ada, tn=1536):
    depth, d, n = w_ada.shape
    b = c.shape[0]
    assert n % tn == 0
    return pl.pallas_call(
        _adaln_kernel,
        grid=(depth, n // tn),
        in_specs=[
            pl.BlockSpec((b, d), lambda l, j: (0, 0)),
            pl.BlockSpec((None, d, tn), lambda l, j: (l, 0, j)),
            pl.BlockSpec((None, 1, tn), lambda l, j: (l, 0, j)),
        ],
        out_specs=pl.BlockSpec((None, b, tn), lambda l, j: (l, 0, j)),
        out_shape=jax.ShapeDtypeStruct((depth, b, n), F32),
        compiler_params=_params(("arbitrary", "arbitrary")),
        name="adaln",
    )(c, w_ada, b_ada.reshape(depth, 1, n))


def _inproj_kernel(x_ref, g_ref, mod_ref, w_ref, wif_ref, wift_ref, z_ref, gcol_ref, grow_ref, h_ref):
    @pl.when(pl.program_id(2) == 0)
    def _():
        h = _norm_mod(x_ref[...], g_ref[...], mod_ref[0], mod_ref[1]).astype(BF16)
        h_ref[...] = h
        gcol_ref[...] = _dot(h, wif_ref[...])
        grow_ref[...] = lax.dot_general(wift_ref[...], h, _NT, preferred_element_type=F32)

    z_ref[...] = _dot(h_ref[...], w_ref[...]).astype(z_ref.dtype)


def _inproj(x, g, mod, w_main, w_if, w_ift, tm=1024, tn=1664):
    b, t, d = x.shape
    n = w_main.shape[1]
    ng = w_ift.shape[0]
    tm = min(tm, t)
    assert t % tm == 0 and n % tn == 0
    return pl.pallas_call(
        _inproj_kernel,
        grid=(b, t // tm, n // tn),
        in_specs=[
            pl.BlockSpec((None, tm, d), lambda bi, i, j: (bi, i, 0)),
            pl.BlockSpec((1, d), lambda bi, i, j: (0, 0)),
            pl.BlockSpec((None, 6, 1, d), lambda bi, i, j: (bi, 0, 0, 0)),
            pl.BlockSpec((d, tn), lambda bi, i, j: (0, j)),
            pl.BlockSpec((d, LANES), lambda bi, i, j: (0, 0)),
            pl.BlockSpec((ng, d), lambda bi, i, j: (0, 0)),
        ],
        out_specs=[
            pl.BlockSpec((None, tm, tn), lambda bi, i, j: (bi, i, j)),
            pl.BlockSpec((None, tm, LANES), lambda bi, i, j: (bi, i, 0)),
            pl.BlockSpec((None, ng, tm), lambda bi, i, j: (bi, 0, i)),
        ],
        out_shape=[
            jax.ShapeDtypeStruct((b, t, n), BF16),
            jax.ShapeDtypeStruct((b, t, LANES), F32),
            jax.ShapeDtypeStruct((b, ng, t), F32),
        ],
        scratch_shapes=[pltpu.VMEM((tm, d), BF16)],
        compiler_params=_params(("arbitrary", "arbitrary", "arbitrary")),
        name="inproj",
    )(x, g, mod, w_main, w_if, w_ift)


def _mlstm_kernel(q_ref, k_ref, v_ref, o_ref, gcol_ref, grow_ref, convw_ref, bcol_ref, brow_ref, gain_ref,
                  out_ref, cbuf, c_st, n_st, m_st, *, heads):
    L, W = q_ref.shape
    dh = W // heads
    taps = convw_ref.shape[0]
    halo = SUBLANES

    @pl.when(pl.program_id(1) == 0)
    def _():
        cbuf[0:halo, :] = jnp.zeros((halo, 2 * W), F32)
        c_st[...] = jnp.zeros_like(c_st)
        n_st[...] = jnp.zeros_like(n_st)
        m_st[...] = jnp.zeros_like(m_st)

    cbuf[halo:halo + L, 0:W] = q_ref[...].astype(F32)
    cbuf[halo:halo + L, W:2 * W] = k_ref[...].astype(F32)
    acc = convw_ref[taps - 1:taps, :] * cbuf[halo:halo + L, :]
    for j in range(taps - 1):
        off = halo - (taps - 1) + j
        acc = acc + convw_ref[j:j + 1, :] * cbuf[off:off + L, :]
    cbuf[0:halo, :] = cbuf[L:L + halo, :]
    qk = acc * _sigmoid(acc)
    q_all = qk[:, 0:W] * (dh ** -0.5)
    k_all = qk[:, W:2 * W]

    gc = gcol_ref[...] + bcol_ref[...]
    gr = grow_ref[...] + brow_ref[...]
    row = lax.broadcasted_iota(jnp.int32, (L, L), 0)
    col = lax.broadcasted_iota(jnp.int32, (L, L), 1)
    causal = row >= col
    tri_low = causal.astype(F32)
    tri_up = (row <= col).astype(F32)
    b_cols = _dot_f32(tri_low, _log_sigmoid(gc))
    b_rows = _dot_f32(_log_sigmoid(gr), tri_up)

    for h in range(heads):
        hs = slice(h * dh, (h + 1) * dh)
        q = q_all[:, hs]
        k = k_all[:, hs]
        v = v_ref[:, hs].astype(F32)
        qb = q.astype(BF16)
        kb = k.astype(BF16)
        li_c = gc[:, h:h + 1]
        b_c = b_cols[:, heads + h:heads + h + 1]
        li_r = gr[h:h + 1, :]
        b_r = b_rows[heads + h:heads + h + 1, :]
        b_tot = b_r[:, L - 1:L]
        c_prev = c_st[h]
        n_prev = n_st[h]
        m_prev = m_st[h][:, 0:1]

        d = jnp.where(causal, b_c - b_r + li_r, -jnp.inf)
        inter_log = b_c + m_prev
        m_comb = jnp.maximum(inter_log, jnp.max(d, axis=-1, keepdims=True))
        s = lax.dot_general(qb, kb, _NT, preferred_element_type=F32) * jnp.exp(d - m_comb)
        w_inter = jnp.exp(inter_log - m_comb)
        num = _dot(s.astype(BF16), v.astype(BF16)) + w_inter * lax.dot_general(
            qb, c_prev.astype(BF16), _NT, preferred_element_type=F32)
        den = jnp.sum(s, axis=-1, keepdims=True) + w_inter * jnp.sum(q * n_prev, axis=-1, keepdims=True)
        den = jnp.maximum(jnp.abs(den), jnp.exp(-m_comb))
        hh = num / den
        hh = hh * lax.rsqrt(jnp.mean(hh * hh, axis=-1, keepdims=True) + EPS)
        gate = _sigmoid(o_ref[:, hs].astype(F32))
        out_ref[:, hs] = (hh * gain_ref[:, hs] * gate).astype(out_ref.dtype)

        a = b_tot - b_c + li_c
        m_loc = jnp.max(a, axis=0, keepdims=True)
        w = jnp.exp(a - m_loc)
        c_loc = lax.dot_general((w * v).astype(BF16), kb, _TN, preferred_element_type=F32)
        n_loc = jnp.sum(w * k, axis=0, keepdims=True)
        m_new = jnp.maximum(b_tot + m_prev, m_loc)
        s_old = jnp.exp(b_tot + m_prev - m_new)
        s_loc = jnp.exp(m_loc - m_new)
        c_st[h] = s_old * c_prev + s_loc * c_loc
        n_st[h] = s_old * n_prev + s_loc * n_loc
        m_st[h] = jnp.broadcast_to(m_new, (1, LANES))


def _mlstm(z, gcol, grow, conv_w, bias_col, bias_row, head_gain, width, heads):
    b, t, _ = z.shape
    L = MLSTM_CHUNK
    dh = width // heads
    ng = grow.shape[1]
    assert t % L == 0 and conv_w.shape[0] - 1 <= SUBLANES
    zspec = lambda blk: pl.BlockSpec((None, L, width), lambda bi, c, blk=blk: (bi, c, blk))
    return pl.pallas_call(
        functools.partial(_mlstm_kernel, heads=heads),
        grid=(b, t // L),
        in_specs=[
            zspec(0), zspec(1), zspec(2), zspec(3),
            pl.BlockSpec((None, L, LANES), lambda bi, c: (bi, c, 0)),
            pl.BlockSpec((None, ng, L), lambda bi, c: (bi, 0, c)),
            pl.BlockSpec(conv_w.shape, lambda bi, c: (0, 0)),
            pl.BlockSpec((1, LANES), lambda bi, c: (0, 0)),
            pl.BlockSpec((ng, 1), lambda bi, c: (0, 0)),
            pl.BlockSpec((1, width), lambda bi, c: (0, 0)),
        ],
        out_specs=pl.BlockSpec((None, L, width), lambda bi, c: (bi, c, 0)),
        out_shape=jax.ShapeDtypeStruct((b, t, width), BF16),
        scratch_shapes=[
            pltpu.VMEM((SUBLANES + L, 2 * width), F32),
            pltpu.VMEM((heads, dh, dh), F32),
            pltpu.VMEM((heads, 1, dh), F32),
            pltpu.VMEM((heads, 1, LANES), F32),
        ],
        compiler_params=_params(("arbitrary", "arbitrary")),
        name="mlstm",
    )(z, z, z, z, gcol, grow, conv_w, bias_col, bias_row, head_gain)


def _mixout_kernel(ha_ref, u_ref, ga_ref, gb_ref, x_ref, mod_ref, poolw_ref, pscale_ref, pa_ref, pb_ref, wo_ref,
                   out_ref, ubuf, *, windows):
    tm = x_ref.shape[0]
    gd = poolw_ref.shape[1]
    halo = max(windows)
    i = pl.program_id(1)

    @pl.when(i == 0)
    def _():
        ubuf[0:halo, :] = jnp.zeros((halo, ubuf.shape[1]), F32)

    ubuf[halo:halo + tm, :] = u_ref[...].astype(F32)
    tpos = i * tm + lax.broadcasted_iota(jnp.int32, (tm, 1), 0)
    parts = []
    for g, win in enumerate(windows):
        cs = slice(g * gd, (g + 1) * gd)
        cur = ubuf[halo:halo + tm, cs]
        wsum = cur
        for j in range(1, win):
            wsum = wsum + ubuf[halo - j:halo - j + tm, cs]
        count = jnp.minimum(tpos + 1, win).astype(F32)
        pooled = wsum / count - cur
        parts.append(_dot(pooled.astype(BF16), poolw_ref[g]))
    ubuf[0:halo, :] = ubuf[tm:tm + halo, :]
    hb = (jnp.concatenate(parts, axis=-1) * pscale_ref[...]).astype(BF16)

    pa = _dot(ha_ref[...], pa_ref[...])
    pb = _dot(hb, pb_ref[...])
    merged = _sigmoid(ga_ref[...].astype(F32)) * pa + _sigmoid(gb_ref[...].astype(F32)) * pb
    y = _dot(merged.astype(BF16), wo_ref[...])
    out_ref[...] = x_ref[...] + mod_ref[2] * y


def _mixout(h_a, z, x, mod, pool_w, pool_scale, proj_a, proj_b, w_out, width, tm=512):
    b, t, d = x.shape
    p = pool_scale.shape[1]
    tm = min(tm, t)
    ga_blk = 4 * width // d
    u_blk = (4 * width + 2 * d) // p
    assert t % tm == 0 and (4 * width) % d == 0 and (4 * width + 2 * d) % p == 0
    const = lambda shape: pl.BlockSpec(shape, lambda bi, i: (0,) * len(shape))
    return pl.pallas_call(
        functools.partial(_mixout_kernel, windows=POOL_WINDOWS),
        grid=(b, t // tm),
        in_specs=[
            pl.BlockSpec((None, tm, width), lambda bi, i: (bi, i, 0)),
            pl.BlockSpec((None, tm, p), lambda bi, i: (bi, i, u_blk)),
            pl.BlockSpec((None, tm, d), lambda bi, i: (bi, i, ga_blk)),
            pl.BlockSpec((None, tm, d), lambda bi, i: (bi, i, ga_blk + 1)),
            pl.BlockSpec((None, tm, d), lambda bi, i: (bi, i, 0)),
            pl.BlockSpec((None, 6, 1, d), lambda bi, i: (bi, 0, 0, 0)),
            const(pool_w.shape), const(pool_scale.shape), const(proj_a.shape), const(proj_b.shape), const(w_out.shape),
        ],
        out_specs=pl.BlockSpec((None, tm, d), lambda bi, i: (bi, i, 0)),
        out_shape=jax.ShapeDtypeStruct((b, t, d), F32),
        scratch_shapes=[pltpu.VMEM((max(POOL_WINDOWS) + tm, p), F32)],
        compiler_params=_params(("arbitrary", "arbitrary")),
        name="mixout",
    )(h_a, z, z, z, x, mod, pool_w, pool_scale, proj_a, proj_b, w_out)


def _ffn_kernel(x_ref, g_ref, mod_ref, wg_ref, wu_ref, wd_ref, fin_ref, out_ref, h_ref, acc_ref, *, final_norm):
    j = pl.program_id(2)

    @pl.when(j == 0)
    def _():
        h_ref[...] = _norm_mod(x_ref[...], g_ref[...], mod_ref[3], mod_ref[4]).astype(BF16)
        acc_ref[...] = jnp.zeros_like(acc_ref)

    h = h_ref[...]
    gate = _dot(h, wg_ref[...])
    act = (gate * _sigmoid(gate) * _dot(h, wu_ref[...])).astype(BF16)
    acc_ref[...] += _dot(act, wd_ref[...])

    @pl.when(j == pl.num_programs(2) - 1)
    def _():
        y = x_ref[...] + mod_ref[5] * acc_ref[...]
        out_ref[...] = _rms(y, fin_ref[...]) if final_norm else y


def _ffn(x, g, mod, w_gate, w_up, w_down, fin, final_norm, tm=512, tf=1408):
    b, t, d = x.shape
    ff = w_gate.shape[1]
    tm = min(tm, t)
    tf = min(tf, ff)
    assert t % tm == 0 and ff % tf == 0
    return pl.pallas_call(
        functools.partial(_ffn_kernel, final_norm=final_norm),
        grid=(b, t // tm, ff // tf),
        in_specs=[
            pl.BlockSpec((None, tm, d), lambda bi, i, j: (bi, i, 0)),
            pl.BlockSpec((1, d), lambda bi, i, j: (0, 0)),
            pl.BlockSpec((None, 6, 1, d), lambda bi, i, j: (bi, 0, 0, 0)),
            pl.BlockSpec((d, tf), lambda bi, i, j: (0, j)),
            pl.BlockSpec((d, tf), lambda bi, i, j: (0, j)),
            pl.BlockSpec((tf, d), lambda bi, i, j: (j, 0)),
            pl.BlockSpec((1, d), lambda bi, i, j: (0, 0)),
        ],
        out_specs=pl.BlockSpec((None, tm, d), lambda bi, i, j: (bi, i, 0)),
        out_shape=jax.ShapeDtypeStruct((b, t, d), F32),
        scratch_shapes=[pltpu.VMEM((tm, d), BF16), pltpu.VMEM((tm, d), F32)],
        compiler_params=_params(("arbitrary", "arbitrary", "arbitrary")),
        name="ffn_dense",
    )(x, g, mod, w_gate, w_up, w_down, fin)


def _router_kernel(x_ref, g_ref, mod_ref, rw_ref, rb_ref, h_ref, cw_ref, *, n_experts):
    h = _norm_mod(x_ref[...], g_ref[...], mod_ref[3], mod_ref[4])
    h_ref[...] = h.astype(h_ref.dtype)
    lane = lax.broadcasted_iota(jnp.int32, (h.shape[0], LANES), 1)
    logits = jnp.where(lane < n_experts, _dot_f32(h, rw_ref[...]) + rb_ref[...], -jnp.inf)
    v1 = jnp.max(logits, axis=-1, keepdims=True)
    i1 = jnp.min(jnp.where(logits == v1, lane, LANES), axis=-1, keepdims=True)
    rest = jnp.where(lane == i1, -jnp.inf, logits)
    v2 = jnp.max(rest, axis=-1, keepdims=True)
    i2 = jnp.min(jnp.where(rest == v2, lane, LANES), axis=-1, keepdims=True)
    e2 = jnp.exp(v2 - v1)
    w1 = 1.0 / (1.0 + e2)
    w2 = e2 / (1.0 + e2)
    cw_ref[...] = jnp.where(lane == i1, w1, 0.0) + jnp.where(lane == i2, w2, 0.0)


def _router(x, g, mod, router_w, router_b, tm=512):
    b, t, d = x.shape
    n_experts = router_w.shape[1]
    tm = min(tm, t)
    assert t % tm == 0 and n_experts <= LANES
    rw = jnp.zeros((d, LANES), F32).at[:, :n_experts].set(router_w)
    rb = jnp.zeros((1, LANES), F32).at[:, :n_experts].set(router_b[None, :])
    return pl.pallas_call(
        functools.partial(_router_kernel, n_experts=n_experts),
        grid=(b, t // tm),
        in_specs=[
            pl.BlockSpec((None, tm, d), lambda bi, i: (bi, i, 0)),
            pl.BlockSpec((1, d), lambda bi, i: (0, 0)),
            pl.BlockSpec((None, 6, 1, d), lambda bi, i: (bi, 0, 0, 0)),
            pl.BlockSpec((d, LANES), lambda bi, i: (0, 0)),
            pl.BlockSpec((1, LANES), lambda bi, i: (0, 0)),
        ],
        out_specs=[
            pl.BlockSpec((None, tm, d), lambda bi, i: (bi, i, 0)),
            pl.BlockSpec((None, tm, LANES), lambda bi, i: (bi, i, 0)),
        ],
        out_shape=[jax.ShapeDtypeStruct((b, t, d), BF16), jax.ShapeDtypeStruct((b, t, LANES), F32)],
        compiler_params=_params(("arbitrary", "arbitrary")),
        name="router",
    )(x, g, mod, rw, rb)


def _moe_kernel(x_ref, h_ref, cw_ref, mod_ref, wg_ref, wu_ref, wd_ref, fin_ref, out_ref, acc_ref, *, final_norm):
    e = pl.program_id(2)
    j = pl.program_id(3)

    @pl.when((e == 0) & (j == 0))
    def _():
        acc_ref[...] = jnp.zeros_like(acc_ref)

    h = h_ref[...]
    lane = lax.broadcasted_iota(jnp.int32, cw_ref.shape, 1)
    cw = jnp.sum(jnp.where(lane == e, cw_ref[...], 0.0), axis=-1, keepdims=True)
    gate = _dot(h, wg_ref[...].astype(BF16))
    act = (cw * (gate * _sigmoid(gate) * _dot(h, wu_ref[...].astype(BF16)))).astype(BF16)
    acc_ref[...] += _dot(act, wd_ref[...].astype(BF16))

    @pl.when((e == pl.num_programs(2) - 1) & (j == pl.num_programs(3) - 1))
    def _():
        y = x_ref[...] + mod_ref[5] * acc_ref[...]
        out_ref[...] = _rms(y, fin_ref[...]) if final_norm else y


def _moe(x, h, cw, mod, w_gate, w_up, w_down, fin, final_norm, tm=512, tf=896):
    b, t, d = x.shape
    n_experts, _, ff = w_gate.shape
    tm = min(tm, t)
    tf = min(tf, ff)
    assert t % tm == 0 and ff % tf == 0
    return pl.pallas_call(
        functools.partial(_moe_kernel, final_norm=final_norm),
        grid=(b, t // tm, n_experts, ff // tf),
        in_specs=[
            pl.BlockSpec((None, tm, d), lambda bi, i, e, j: (bi, i, 0)),
            pl.BlockSpec((None, tm, d), lambda bi, i, e, j: (bi, i, 0)),
            pl.BlockSpec((None, tm, LANES), lambda bi, i, e, j: (bi, i, 0)),
            pl.BlockSpec((None, 6, 1, d), lambda bi, i, e, j: (bi, 0, 0, 0)),
            pl.BlockSpec((None, d, tf), lambda bi, i, e, j: (e, 0, j)),
            pl.BlockSpec((None, d, tf), lambda bi, i, e, j: (e, 0, j)),
            pl.BlockSpec((None, tf, d), lambda bi, i, e, j: (e, j, 0)),
            pl.BlockSpec((1, d), lambda bi, i, e, j: (0, 0)),
        ],
        out_specs=pl.BlockSpec((None, tm, d), lambda bi, i, e, j: (bi, i, 0)),
        out_shape=jax.ShapeDtypeStruct((b, t, d), F32),
        scratch_shapes=[pltpu.VMEM((tm, d), F32)],
        compiler_params=_params(("arbitrary", "arbitrary", "arbitrary", "arbitrary")),
        name="moe_dense",
    )(x, h, cw, mod, w_gate, w_up, w_down, fin)


def kernel(x, c, norm_mix, norm_ffn, w_ada, b_ada, w_in, conv_w, i_bias, f_bias, head_gain, pool_w, pool_scale,
           proj_a, proj_b, w_out, ffn_w_gate, ffn_w_up, ffn_w_down, router_w, router_b, moe_w_gate, moe_w_up,
           moe_w_down, final_norm):
    depth = w_in.shape[0]
    b, t, d = x.shape
    heads = i_bias.shape[1]
    width = head_gain.shape[1]
    p = pool_scale.shape[1]
    ng = 2 * heads
    qkvo = 4 * width

    mod_all = _adaln(c, w_ada, b_ada).reshape(depth, b, 6, 1, d)
    fin = final_norm.reshape(1, d)

    for l in range(depth):
        mod = mod_all[l]
        w = w_in[l]
        w_main = jnp.concatenate([w[:, :qkvo], w[:, qkvo + ng + p:], w[:, qkvo + ng:qkvo + ng + p]], axis=1).astype(BF16)
        w_gates = w[:, qkvo:qkvo + ng]
        w_if = jnp.zeros((d, LANES), BF16).at[:, :ng].set(w_gates.astype(BF16))
        w_ift = w_gates.T.astype(BF16)
        bias = jnp.concatenate([i_bias[l], f_bias[l]])
        bias_col = jnp.zeros((1, LANES), F32).at[0, :ng].set(bias)
        bias_row = bias.reshape(ng, 1)

        z, gcol, grow = _inproj(x, norm_mix[l].reshape(1, d), mod, w_main, w_if, w_ift)
        h_a = _mlstm(z, gcol, grow, conv_w[l], bias_col, bias_row, head_gain[l].reshape(1, width), width, heads)
        x = _mixout(h_a, z, x, mod, pool_w[l].astype(BF16), pool_scale[l].reshape(1, p), proj_a[l].astype(BF16),
                    proj_b[l].astype(BF16), w_out[l].astype(BF16), width)

        last = l == depth - 1
        j = l // 2
        g_ffn = norm_ffn[l].reshape(1, d)
        if l % 2 == 0:
            x = _ffn(x, g_ffn, mod, ffn_w_gate[j].astype(BF16), ffn_w_up[j].astype(BF16),
                     ffn_w_down[j].astype(BF16), fin, last)
        else:
            h, cw = _router(x, g_ffn, mod, router_w[j], router_b[j])
            x = _moe(x, h, cw, mod, moe_w_gate[j], moe_w_up[j], moe_w_down[j], fin, last)
    return x
```

```python
import functools

import jax
import jax.numpy as jnp
from jax import lax
from jax.experimental import pallas as pl
from jax.experimental.pallas import tpu as pltpu

F32 = jnp.float32
BF16 = jnp.bfloat16

EPS = 1e-6
MLSTM_CHUNK = 128
POOL_WINDOWS = (2, 4, 8, 16)
TOP_K = 2
LANES = 128
SUBLANES = 8
VMEM_LIMIT = 56 * 1024 * 1024

_NT = (((1,), (1,)), ((), ()))
_TN = (((0,), (0,)), ((), ()))


def _params(sem):
    return pltpu.CompilerParams(dimension_semantics=sem, vmem_limit_bytes=VMEM_LIMIT)


def _sigmoid(x):
    return 1.0 / (1.0 + jnp.exp(-x))


def _log_sigmoid(x):
    return jnp.minimum(x, 0.0) - jnp.log(1.0 + jnp.exp(-jnp.abs(x)))


def _rms(x, g):
    return x * lax.rsqrt(jnp.mean(x * x, axis=-1, keepdims=True) + EPS) * g


def _norm_mod(x, g, shift, scale):
    return _rms(x, g) * (1.0 + scale) + shift


def _dot(a, b):
    return jnp.dot(a, b, preferred_element_type=F32)


def _dot_f32(a, b, dims=None):
    dims = dims or (((1,), (0,)), ((), ()))
    return lax.dot_general(a, b, dims, precision=lax.Precision.HIGHEST, preferred_element_type=F32)


def _adaln_kernel(c_ref, w_ref, b_ref, o_ref):
    c = c_ref[...]
    o_ref[...] = _dot_f32(c * _sigmoid(c), w_ref[...]) + b_ref[...]


def _adaln(c, w_ada, b_ada, tn=1536):
    depth, d, n = w_ada.shape
    b = c.shape[0]
    assert n % tn == 0
    return pl.pallas_call(
        _adaln_kernel,
        grid=(depth, n // tn),
        in_specs=[
            pl.BlockSpec((b, d), lambda l, j: (0, 0)),
            pl.BlockSpec((None, d, tn), lambda l, j: (l, 0, j)),
            pl.BlockSpec((None, 1, tn), lambda l, j: (l, 0, j)),
        ],
        out_specs=pl.BlockSpec((None, b, tn), lambda l, j: (l, 0, j)),
        out_shape=jax.ShapeDtypeStruct((depth, b, n), F32),
        compiler_params=_params(("arbitrary", "arbitrary")),
        name="adaln",
    )(c, w_ada, b_ada.reshape(depth, 1, n))


def _inproj_kernel(x_ref, g_ref, mod_ref, w_ref, wif_ref, wift_ref, z_ref, gcol_ref, grow_ref, h_ref):
    @pl.when(pl.program_id(2) == 0)
    def _():
        h = _norm_mod(x_ref[...], g_ref[...], mod_ref[0], mod_ref[1]).astype(BF16)
        h_ref[...] = h
        gcol_ref[...] = _dot(h, wif_ref[...])
        grow_ref[...] = lax.dot_general(wift_ref[...], h, _NT, preferred_element_type=F32)

    z_ref[...] = _dot(h_ref[...], w_ref[...]).astype(z_ref.dtype)


def _inproj(x, g, mod, w_main, w_if, w_ift, tm=1024, tn=1664):
    b, t, d = x.shape
    n = w_main.shape[1]
    ng = w_ift.shape[0]
    tm = min(tm, t)
    assert t % tm == 0 and n % tn == 0
    return pl.pallas_call(
        _inproj_kernel,
        grid=(b, t // tm, n // tn),
        in_specs=[
            pl.BlockSpec((None, tm, d), lambda bi, i, j: (bi, i, 0)),
            pl.BlockSpec((1, d), lambda bi, i, j: (0, 0)),
            pl.BlockSpec((None, 6, 1, d), lambda bi, i, j: (bi, 0, 0, 0)),
            pl.BlockSpec((d, tn), lambda bi, i, j: (0, j)),
            pl.BlockSpec((d, LANES), lambda bi, i, j: (0, 0)),
            pl.BlockSpec((ng, d), lambda bi, i, j: (0, 0)),
        ],
        out_specs=[
            pl.BlockSpec((None, tm, tn), lambda bi, i, j: (bi, i, j)),
            pl.BlockSpec((None, tm, LANES), lambda bi, i, j: (bi, i, 0)),
            pl.BlockSpec((None, ng, tm), lambda bi, i, j: (bi, 0, i)),
        ],
        out_shape=[
            jax.ShapeDtypeStruct((b, t, n), BF16),
            jax.ShapeDtypeStruct((b, t, LANES), F32),
            jax.ShapeDtypeStruct((b, ng, t), F32),
        ],
        scratch_shapes=[pltpu.VMEM((tm, d), BF16)],
        compiler_params=_params(("arbitrary", "arbitrary", "arbitrary")),
        name="inproj",
    )(x, g, mod, w_main, w_if, w_ift)


def _mlstm_kernel(q_ref, k_ref, v_ref, o_ref, gcol_ref, grow_ref, convw_ref, bcol_ref, brow_ref, gain_ref,
                  out_ref, cbuf, c_st, n_st, m_st, *, heads):
    L, W = q_ref.shape
    dh = W // heads
    taps = convw_ref.shape[0]
    halo = SUBLANES

    @pl.when(pl.program_id(1) == 0)
    def _():
        cbuf[0:halo, :] = jnp.zeros((halo, 2 * W), F32)
        c_st[...] = jnp.zeros_like(c_st)
        n_st[...] = jnp.zeros_like(n_st)
        m_st[...] = jnp.zeros_like(m_st)

    cbuf[halo:halo + L, 0:W] = q_ref[...].astype(F32)
    cbuf[halo:halo + L, W:2 * W] = k_ref[...].astype(F32)
    acc = convw_ref[taps - 1:taps, :] * cbuf[halo:halo + L, :]
    for j in range(taps - 1):
        off = halo - (taps - 1) + j
        acc = acc + convw_ref[j:j + 1, :] * cbuf[off:off + L, :]
    cbuf[0:halo, :] = cbuf[L:L + halo, :]
    qk = acc * _sigmoid(acc)
    q_all = qk[:, 0:W] * (dh ** -0.5)
    k_all = qk[:, W:2 * W]

    gc = gcol_ref[...] + bcol_ref[...]
    gr = grow_ref[...] + brow_ref[...]
    row = lax.broadcasted_iota(jnp.int32, (L, L), 0)
    col = lax.broadcasted_iota(jnp.int32, (L, L), 1)
    causal = row >= col
    tri_low = causal.astype(F32)
    tri_up = (row <= col).astype(F32)
    b_cols = _dot_f32(tri_low, _log_sigmoid(gc))
    b_rows = _dot_f32(_log_sigmoid(gr), tri_up)

    for h in range(heads):
        hs = slice(h * dh, (h + 1) * dh)
        q = q_all[:, hs]
        k = k_all[:, hs]
        v = v_ref[:, hs].astype(F32)
        qb = q.astype(BF16)
        kb = k.astype(BF16)
        li_c = gc[:, h:h + 1]
        b_c = b_cols[:, heads + h:heads + h + 1]
        li_r = gr[h:h + 1, :]
        b_r = b_rows[heads + h:heads + h + 1, :]
        b_tot = b_r[:, L - 1:L]
        c_prev = c_st[h]
        n_prev = n_st[h]
        m_prev = m_st[h][:, 0:1]

        d = jnp.where(causal, b_c - b_r + li_r, -jnp.inf)
        inter_log = b_c + m_prev
        m_comb = jnp.maximum(inter_log, jnp.max(d, axis=-1, keepdims=True))
        s = lax.dot_general(qb, kb, _NT, preferred_element_type=F32) * jnp.exp(d - m_comb)
        w_inter = jnp.exp(inter_log - m_comb)
        num = _dot(s.astype(BF16), v.astype(BF16)) + w_inter * lax.dot_general(
            qb, c_prev.astype(BF16), _NT, preferred_element_type=F32)
        den = jnp.sum(s, axis=-1, keepdims=True) + w_inter * jnp.sum(q * n_prev, axis=-1, keepdims=True)
        den = jnp.maximum(jnp.abs(den), jnp.exp(-m_comb))
        hh = num / den
        hh = hh * lax.rsqrt(jnp.mean(hh * hh, axis=-1, keepdims=True) + EPS)
        gate = _sigmoid(o_ref[:, hs].astype(F32))
        out_ref[:, hs] = (hh * gain_ref[:, hs] * gate).astype(out_ref.dtype)

        a = b_tot - b_c + li_c
        m_loc = jnp.max(a, axis=0, keepdims=True)
        w = jnp.exp(a - m_loc)
        c_loc = lax.dot_general((w * v).astype(BF16), kb, _TN, preferred_element_type=F32)
        n_loc = jnp.sum(w * k, axis=0, keepdims=True)
        m_new = jnp.maximum(b_tot + m_prev, m_loc)
        s_old = jnp.exp(b_tot + m_prev - m_new)
        s_loc = jnp.exp(m_loc - m_new)
        c_st[h] = s_old * c_prev + s_loc * c_loc
        n_st[h] = s_old * n_prev + s_loc * n_loc
        m_st[h] = jnp.broadcast_to(m_new, (1, LANES))


def _mlstm(z, gcol, grow, conv_w, bias_col, bias_row, head_gain, width, heads):
    b, t, _ = z.shape
    L = MLSTM_CHUNK
    dh = width // heads
    ng = grow.shape[1]
    assert t % L == 0 and conv_w.shape[0] - 1 <= SUBLANES
    zspec = lambda blk: pl.BlockSpec((None, L, width), lambda bi, c, blk=blk: (bi, c, blk))
    return pl.pallas_call(
        functools.partial(_mlstm_kernel, heads=heads),
        grid=(b, t // L),
        in_specs=[
            zspec(0), zspec(1), zspec(2), zspec(3),
            pl.BlockSpec((None, L, LANES), lambda bi, c: (bi, c, 0)),
            pl.BlockSpec((None, ng, L), lambda bi, c: (bi, 0, c)),
            pl.BlockSpec(conv_w.shape, lambda bi, c: (0, 0)),
            pl.BlockSpec((1, LANES), lambda bi, c: (0, 0)),
            pl.BlockSpec((ng, 1), lambda bi, c: (0, 0)),
            pl.BlockSpec((1, width), lambda bi, c: (0, 0)),
        ],
        out_specs=pl.BlockSpec((None, L, width), lambda bi, c: (bi, c, 0)),
        out_shape=jax.ShapeDtypeStruct((b, t, width), BF16),
        scratch_shapes=[
            pltpu.VMEM((SUBLANES + L, 2 * width), F32),
            pltpu.VMEM((heads, dh, dh), F32),
            pltpu.VMEM((heads, 1, dh), F32),
            pltpu.VMEM((heads, 1, LANES), F32),
        ],
        compiler_params=_params(("arbitrary", "arbitrary")),
        name="mlstm",
    )(z, z, z, z, gcol, grow, conv_w, bias_col, bias_row, head_gain)


def _mixout_kernel(ha_ref, u_ref, ga_ref, gb_ref, x_ref, mod_ref, poolw_ref, pscale_ref, pa_ref, pb_ref, wo_ref,
                   out_ref, ubuf, *, windows):
    tm = x_ref.shape[0]
    gd = poolw_ref.shape[1]
    halo = max(windows)
    i = pl.program_id(1)

    @pl.when(i == 0)
    def _():
        ubuf[0:halo, :] = jnp.zeros((halo, ubuf.shape[1]), F32)

    ubuf[halo:halo + tm, :] = u_ref[...].astype(F32)
    tpos = i * tm + lax.broadcasted_iota(jnp.int32, (tm, 1), 0)
    parts = []
    for g, win in enumerate(windows):
        cs = slice(g * gd, (g + 1) * gd)
        cur = ubuf[halo:halo + tm, cs]
        wsum = cur
        for j in range(1, win):
            wsum = wsum + ubuf[halo - j:halo - j + tm, cs]
        count = jnp.minimum(tpos + 1, win).astype(F32)
        pooled = wsum / count - cur
        parts.append(_dot(pooled.astype(BF16), poolw_ref[g]))
    ubuf[0:halo, :] = ubuf[tm:tm + halo, :]
    hb = (jnp.concatenate(parts, axis=-1) * pscale_ref[...]).astype(BF16)

    pa = _dot(ha_ref[...], pa_ref[...])
    pb = _dot(hb, pb_ref[...])
    merged = _sigmoid(ga_ref[...].astype(F32)) * pa + _sigmoid(gb_ref[...].astype(F32)) * pb
    y = _dot(merged.astype(BF16), wo_ref[...])
    out_ref[...] = x_ref[...] + mod_ref[2] * y


def _mixout(h_a, z, x, mod, pool_w, pool_scale, proj_a, proj_b, w_out, width, tm=512):
    b, t, d = x.shape
    p = pool_scale.shape[1]
    tm = min(tm, t)
    ga_blk = 4 * width // d
    u_blk = (4 * width + 2 * d) // p
    assert t % tm == 0 and (4 * width) % d == 0 and (4 * width + 2 * d) % p == 0
    const = lambda shape: pl.BlockSpec(shape, lambda bi, i: (0,) * len(shape))
    return pl.pallas_call(
        functools.partial(_mixout_kernel, windows=POOL_WINDOWS),
        grid=(b, t // tm),
        in_specs=[
            pl.BlockSpec((None, tm, width), lambda bi, i: (bi, i, 0)),
            pl.BlockSpec((None, tm, p), lambda bi, i: (bi, i, u_blk)),
            pl.BlockSpec((None, tm, d), lambda bi, i: (bi, i, ga_blk)),
            pl.BlockSpec((None, tm, d), lambda bi, i: (bi, i, ga_blk + 1)),
            pl.BlockSpec((None, tm, d), lambda bi, i: (bi, i, 0)),
            pl.BlockSpec((None, 6, 1, d), lambda bi, i: (bi, 0, 0, 0)),
            const(pool_w.shape), const(pool_scale.shape), const(proj_a.shape), const(proj_b.shape), const(w_out.shape),
        ],
        out_specs=pl.BlockSpec((None, tm, d), lambda bi, i: (bi, i, 0)),
        out_shape=jax.ShapeDtypeStruct((b, t, d), F32),
        scratch_shapes=[pltpu.VMEM((max(POOL_WINDOWS) + tm, p), F32)],
        compiler_params=_params(("arbitrary", "arbitrary")),
        name="mixout",
    )(h_a, z, z, z, x, mod, pool_w, pool_scale, proj_a, proj_b, w_out)


def _ffn_kernel(x_ref, g_ref, mod_ref, wg_ref, wu_ref, wd_ref, fin_ref, out_ref, h_ref, acc_ref, *, final_norm):
    j = pl.program_id(2)

    @pl.when(j == 0)
    def _():
        h_ref[...] = _norm_mod(x_ref[...], g_ref[...], mod_ref[3], mod_ref[4]).astype(BF16)
        acc_ref[...] = jnp.zeros_like(acc_ref)

    h = h_ref[...]
    gate = _dot(h, wg_ref[...])
    act = (gate * _sigmoid(gate) * _dot(h, wu_ref[...])).astype(BF16)
    acc_ref[...] += _dot(act, wd_ref[...])

    @pl.when(j == pl.num_programs(2) - 1)
    def _():
        y = x_ref[...] + mod_ref[5] * acc_ref[...]
        out_ref[...] = _rms(y, fin_ref[...]) if final_norm else y


def _ffn(x, g, mod, w_gate, w_up, w_down, fin, final_norm, tm=512, tf=1408):
    b, t, d = x.shape
    ff = w_gate.shape[1]
    tm = min(tm, t)
    tf = min(tf, ff)
    assert t % tm == 0 and ff % tf == 0
    return pl.pallas_call(
        functools.partial(_ffn_kernel, final_norm=final_norm),
        grid=(b, t // tm, ff // tf),
        in_specs=[
            pl.BlockSpec((None, tm, d), lambda bi, i, j: (bi, i, 0)),
            pl.BlockSpec((1, d), lambda bi, i, j: (0, 0)),
            pl.BlockSpec((None, 6, 1, d), lambda bi, i, j: (bi, 0, 0, 0)),
            pl.BlockSpec((d, tf), lambda bi, i, j: (0, j)),
            pl.BlockSpec((d, tf), lambda bi, i, j: (0, j)),
            pl.BlockSpec((tf, d), lambda bi, i, j: (j, 0)),
            pl.BlockSpec((1, d), lambda bi, i, j: (0, 0)),
        ],
        out_specs=pl.BlockSpec((None, tm, d), lambda bi, i, j: (bi, i, 0)),
        out_shape=jax.ShapeDtypeStruct((b, t, d), F32),
        scratch_shapes=[pltpu.VMEM((tm, d), BF16), pltpu.VMEM((tm, d), F32)],
        compiler_params=_params(("arbitrary", "arbitrary", "arbitrary")),
        name="ffn_dense",
    )(x, g, mod, w_gate, w_up, w_down, fin)


def _router_kernel(x_ref, g_ref, mod_ref, rw_ref, rb_ref, h_ref, info_ref, cnt_ref, carry_ref, *, n_experts, cap):
    first = (pl.program_id(0) == 0) & (pl.program_id(1) == 0)

    @pl.when(first)
    def _():
        carry_ref[...] = jnp.zeros_like(carry_ref)

    h = _norm_mod(x_ref[...], g_ref[...], mod_ref[3], mod_ref[4])
    h_ref[...] = h
    tm = h.shape[0]
    lane = lax.broadcasted_iota(jnp.int32, (tm, LANES), 1)
    logits = jnp.where(lane < n_experts, _dot_f32(h, rw_ref[...]) + rb_ref[...], -jnp.inf)
    v1 = jnp.max(logits, axis=-1, keepdims=True)
    i1 = jnp.min(jnp.where(logits == v1, lane, LANES), axis=-1, keepdims=True)
    rest = jnp.where(lane == i1, -jnp.inf, logits)
    v2 = jnp.max(rest, axis=-1, keepdims=True)
    i2 = jnp.min(jnp.where(rest == v2, lane, LANES), axis=-1, keepdims=True)
    e2 = jnp.exp(v2 - v1)
    w1 = 1.0 / (1.0 + e2)
    w2 = e2 / (1.0 + e2)
    sel1 = lane == i1
    sel2 = lane == i2
    picked = jnp.where(sel1 | sel2, 1.0, 0.0)
    row = lax.broadcasted_iota(jnp.int32, (tm, tm), 0)
    col = lax.broadcasted_iota(jnp.int32, (tm, tm), 1)
    before = jnp.where(col < row, 1.0, 0.0).astype(BF16)
    ex = _dot(before, picked.astype(BF16)) + carry_ref[...]
    pos1 = i1.astype(F32) * cap + jnp.sum(jnp.where(sel1, ex, 0.0), axis=-1, keepdims=True)
    pos2 = i2.astype(F32) * cap + jnp.sum(jnp.where(sel2, ex, 0.0), axis=-1, keepdims=True)
    carry = carry_ref[...] + jnp.sum(picked, axis=0, keepdims=True)
    carry_ref[...] = carry
    cnt_ref[...] = carry
    info_ref[...] = (jnp.where(lane == 0, pos1, 0.0) + jnp.where(lane == 1, pos2, 0.0)
                     + jnp.where(lane == 2, w1, 0.0) + jnp.where(lane == 3, w2, 0.0))


def _router(x, g, mod, router_w, router_b, tm=512):
    b, t, d = x.shape
    n_experts = router_w.shape[1]
    tm = min(tm, t)
    assert t % tm == 0 and n_experts <= LANES and n_experts * b * t < 2 ** 24
    rw = jnp.zeros((d, LANES), F32).at[:, :n_experts].set(router_w)
    rb = jnp.zeros((1, LANES), F32).at[:, :n_experts].set(router_b[None, :])
    return pl.pallas_call(
        functools.partial(_router_kernel, n_experts=n_experts, cap=b * t),
        grid=(b, t // tm),
        in_specs=[
            pl.BlockSpec((None, tm, d), lambda bi, i: (bi, i, 0)),
            pl.BlockSpec((1, d), lambda bi, i: (0, 0)),
            pl.BlockSpec((None, 6, 1, d), lambda bi, i: (bi, 0, 0, 0)),
            pl.BlockSpec((d, LANES), lambda bi, i: (0, 0)),
            pl.BlockSpec((1, LANES), lambda bi, i: (0, 0)),
        ],
        out_specs=[
            pl.BlockSpec((None, tm, d), lambda bi, i: (bi, i, 0)),
            pl.BlockSpec((None, tm, LANES), lambda bi, i: (bi, i, 0)),
            pl.BlockSpec((1, LANES), lambda bi, i: (0, 0)),
        ],
        out_shape=[jax.ShapeDtypeStruct((b, t, d), F32), jax.ShapeDtypeStruct((b, t, LANES), F32),
                   jax.ShapeDtypeStruct((1, LANES), F32)],
        scratch_shapes=[pltpu.VMEM((1, LANES), F32)],
        compiler_params=_params(("arbitrary", "arbitrary")),
        name="router",
    )(x, g, mod, rw, rb)


def _row_copy(src, src_row, dst, dst_row, sem):
    return pltpu.make_async_copy(src.at[pl.ds(src_row, 1)], dst.at[pl.ds(dst_row, 1)], sem)


def _scatter_kernel(pos_ref, h_hbm, hs_hbm, sem, *, rows):
    base = pl.program_id(0) * rows

    def copies(r):
        t = base + r
        return (_row_copy(h_hbm, t, hs_hbm, pos_ref[TOP_K * t], sem),
                _row_copy(h_hbm, t, hs_hbm, pos_ref[TOP_K * t + 1], sem))

    def start(r, carry):
        for cp in copies(r):
            cp.start()
        return carry

    def wait(r, carry):
        for cp in copies(r):
            cp.wait()
        return carry

    lax.fori_loop(0, rows, start, 0, unroll=8)
    lax.fori_loop(0, rows, wait, 0, unroll=8)


def _scatter_rows(pos, h, n_rows_out, rows=512):
    n, d = h.shape
    rows = min(rows, n)
    assert n % rows == 0
    return pl.pallas_call(
        functools.partial(_scatter_kernel, rows=rows),
        grid_spec=pltpu.PrefetchScalarGridSpec(
            num_scalar_prefetch=1,
            grid=(n // rows,),
            in_specs=[pl.BlockSpec(memory_space=pl.ANY)],
            out_specs=pl.BlockSpec(memory_space=pl.ANY),
            scratch_shapes=[pltpu.SemaphoreType.DMA],
        ),
        out_shape=jax.ShapeDtypeStruct((n_rows_out, d), h.dtype),
        compiler_params=_params(("arbitrary",)),
        name="moe_scatter",
    )(pos, h)


def _moe_ffn_kernel(te_ref, tb_ref, tv_ref, hs_ref, wg_ref, wu_ref, wd_ref, y_ref, xb_ref):
    i = pl.program_id(0)
    j = pl.program_id(1)
    valid = tv_ref[i]

    @pl.when(valid > 0)
    def _():
        @pl.when(j == 0)
        def _():
            row = lax.broadcasted_iota(jnp.int32, (hs_ref.shape[0], 1), 0)
            xb_ref[...] = jnp.where(row < valid, hs_ref[...], 0.0).astype(BF16)
            y_ref[...] = jnp.zeros_like(y_ref)

        h = xb_ref[...]
        gate = _dot(h, wg_ref[...].astype(BF16))
        act = (gate * _sigmoid(gate) * _dot(h, wu_ref[...].astype(BF16))).astype(BF16)
        y_ref[...] += _dot(act, wd_ref[...].astype(BF16))


def _moe_ffn(tile_e, tile_blk, tile_valid, hs, w_gate, w_up, w_down, tm, tf=512):
    _, d = hs.shape
    n_experts, _, ff = w_gate.shape
    tf = min(tf, ff)
    nf = ff // tf
    assert ff % tf == 0
    n_tiles = tile_e.shape[0]
    jj = lambda i, j, tv: jnp.where(tv[i] > 0, j, nf - 1)
    return pl.pallas_call(
        _moe_ffn_kernel,
        grid_spec=pltpu.PrefetchScalarGridSpec(
            num_scalar_prefetch=3,
            grid=(n_tiles, nf),
            in_specs=[
                pl.BlockSpec((tm, d), lambda i, j, te, tb, tv: (tb[i], 0)),
                pl.BlockSpec((None, d, tf), lambda i, j, te, tb, tv: (te[i], 0, jj(i, j, tv))),
                pl.BlockSpec((None, d, tf), lambda i, j, te, tb, tv: (te[i], 0, jj(i, j, tv))),
                pl.BlockSpec((None, tf, d), lambda i, j, te, tb, tv: (te[i], jj(i, j, tv), 0)),
            ],
            out_specs=pl.BlockSpec((tm, d), lambda i, j, te, tb, tv: (tb[i], 0)),
            scratch_shapes=[pltpu.VMEM((tm, d), BF16)],
        ),
        out_shape=jax.ShapeDtypeStruct(hs.shape, F32),
        compiler_params=_params(("arbitrary", "arbitrary")),
        name="moe_ffn",
    )(tile_e, tile_blk, tile_valid, hs, w_gate, w_up, w_down)


def _combine_kernel(pos_ref, x_ref, info_ref, mod_ref, fin_ref, y_hbm, out_ref, ybuf, sem, *, final_norm):
    rows = x_ref.shape[0]
    base = pl.program_id(0) * rows

    def copies(r):
        t = base + r
        return (_row_copy(y_hbm, pos_ref[TOP_K * t], ybuf.at[0], r, sem),
                _row_copy(y_hbm, pos_ref[TOP_K * t + 1], ybuf.at[1], r, sem))

    def start(r, carry):
        for cp in copies(r):
            cp.start()
        return carry

    def wait(r, carry):
        for cp in copies(r):
            cp.wait()
        return carry

    lax.fori_loop(0, rows, start, 0, unroll=8)
    lax.fori_loop(0, rows, wait, 0, unroll=8)
    info = info_ref[...]
    f = info[:, 2:3] * ybuf[0] + info[:, 3:4] * ybuf[1]
    y = x_ref[...] + mod_ref[5] * f
    out_ref[...] = _rms(y, fin_ref[...]) if final_norm else y


def _combine(pos, x, info, mod, fin, y, final_norm, rows=256):
    b, t, d = x.shape
    n = b * t
    rows = min(rows, t)
    assert t % rows == 0
    per_b = t // rows
    return pl.pallas_call(
        functools.partial(_combine_kernel, final_norm=final_norm),
        grid_spec=pltpu.PrefetchScalarGridSpec(
            num_scalar_prefetch=1,
            grid=(n // rows,),
            in_specs=[
                pl.BlockSpec((rows, d), lambda i, pos: (i, 0)),
                pl.BlockSpec((rows, LANES), lambda i, pos: (i, 0)),
                pl.BlockSpec((None, 6, 1, d), lambda i, pos: (i // per_b, 0, 0, 0)),
                pl.BlockSpec((1, d), lambda i, pos: (0, 0)),
                pl.BlockSpec(memory_space=pl.ANY),
            ],
            out_specs=pl.BlockSpec((rows, d), lambda i, pos: (i, 0)),
            scratch_shapes=[pltpu.VMEM((TOP_K, rows, d), F32), pltpu.SemaphoreType.DMA],
        ),
        out_shape=jax.ShapeDtypeStruct((n, d), F32),
        compiler_params=_params(("arbitrary",)),
        name="moe_combine",
    )(pos, x.reshape(n, d), info.reshape(n, LANES), mod, fin, y).reshape(b, t, d)


def _tile_tables(counts, n_experts, cap, tm, n_tiles):
    counts = counts.astype(jnp.int32)
    tiles_per = (counts + tm - 1) // tm
    ends = jnp.cumsum(tiles_per)
    used = ends[-1]
    i = jnp.minimum(jnp.arange(n_tiles, dtype=jnp.int32), used - 1)
    e = jnp.sum((i[:, None] >= ends[None, :]).astype(jnp.int32), axis=1)
    k = i - (ends - tiles_per)[e]
    valid = jnp.where(jnp.arange(n_tiles) < used, jnp.minimum(counts[e] - k * tm, tm), 0)
    return e, e * (cap // tm) + k, valid.astype(jnp.int32)


def _moe(x, g, mod, router_w, router_b, w_gate, w_up, w_down, fin, final_norm, tm=1024):
    b, t, d = x.shape
    n = b * t
    n_experts = router_w.shape[1]
    tm = min(tm, n)
    assert n % tm == 0
    h, info, counts = _router(x, g, mod, router_w, router_b)
    pos = info[:, :, :TOP_K].astype(jnp.int32).reshape(n * TOP_K)
    hs = _scatter_rows(pos, h.reshape(n, d), n_experts * n)
    n_tiles = TOP_K * n // tm + n_experts
    tile_e, tile_blk, tile_valid = _tile_tables(counts[0, :n_experts], n_experts, n, tm, n_tiles)
    y = _moe_ffn(tile_e, tile_blk, tile_valid, hs, w_gate, w_up, w_down, tm)
    return _combine(pos, x, info, mod, fin, y, final_norm)


def kernel(x, c, norm_mix, norm_ffn, w_ada, b_ada, w_in, conv_w, i_bias, f_bias, head_gain, pool_w, pool_scale,
           proj_a, proj_b, w_out, ffn_w_gate, ffn_w_up, ffn_w_down, router_w, router_b, moe_w_gate, moe_w_up,
           moe_w_down, final_norm):
    depth = w_in.shape[0]
    b, t, d = x.shape
    heads = i_bias.shape[1]
    width = head_gain.shape[1]
    p = pool_scale.shape[1]
    ng = 2 * heads
    qkvo = 4 * width

    mod_all = _adaln(c, w_ada, b_ada).reshape(depth, b, 6, 1, d)
    fin = final_norm.reshape(1, d)

    for l in range(depth):
        mod = mod_all[l]
        w = w_in[l]
        w_main = jnp.concatenate([w[:, :qkvo], w[:, qkvo + ng + p:], w[:, qkvo + ng:qkvo + ng + p]], axis=1).astype(BF16)
        w_gates = w[:, qkvo:qkvo + ng]
        w_if = jnp.zeros((d, LANES), BF16).at[:, :ng].set(w_gates.astype(BF16))
        w_ift = w_gates.T.astype(BF16)
        bias = jnp.concatenate([i_bias[l], f_bias[l]])
        bias_col = jnp.zeros((1, LANES), F32).at[0, :ng].set(bias)
        bias_row = bias.reshape(ng, 1)

        z, gcol, grow = _inproj(x, norm_mix[l].reshape(1, d), mod, w_main, w_if, w_ift)
        h_a = _mlstm(z, gcol, grow, conv_w[l], bias_col, bias_row, head_gain[l].reshape(1, width), width, heads)
        x = _mixout(h_a, z, x, mod, pool_w[l].astype(BF16), pool_scale[l].reshape(1, p), proj_a[l].astype(BF16),
                    proj_b[l].astype(BF16), w_out[l].astype(BF16), width)

        last = l == depth - 1
        j = l // 2
        g_ffn = norm_ffn[l].reshape(1, d)
        if l % 2 == 0:
            x = _ffn(x, g_ffn, mod, ffn_w_gate[j].astype(BF16), ffn_w_up[j].astype(BF16),
                     ffn_w_down[j].astype(BF16), fin, last)
        else:
            x = _moe(x, g_ffn, mod, router_w[j], router_b[j], moe_w_gate[j], moe_w_up[j], moe_w_down[j], fin, last)
    return x
```

```python
import functools

import jax
import jax.numpy as jnp
from jax import lax
from jax.experimental import pallas as pl
from jax.experimental.pallas import tpu as pltpu

F32 = jnp.float32
BF16 = jnp.bfloat16

EPS = 1e-6
MLSTM_CHUNK = 128
POOL_WINDOWS = (2, 4, 8, 16)
TOP_K = 2
MOE_TILE_PARTS = 4
LANES = 128
SUBLANES = 8
VMEM_LIMIT = 56 * 1024 * 1024

_NT = (((1,), (1,)), ((), ()))
_TN = (((0,), (0,)), ((), ()))


def _params(sem):
    return pltpu.CompilerParams(dimension_semantics=sem, vmem_limit_bytes=VMEM_LIMIT)


def _sigmoid(x):
    return 1.0 / (1.0 + jnp.exp(-x))


def _log_sigmoid(x):
    return jnp.minimum(x, 0.0) - jnp.log(1.0 + jnp.exp(-jnp.abs(x)))


def _rms(x, g):
    return x * lax.rsqrt(jnp.mean(x * x, axis=-1, keepdims=True) + EPS) * g


def _norm_mod(x, g, shift, scale):
    return _rms(x, g) * (1.0 + scale) + shift


def _dot(a, b):
    return jnp.dot(a, b, preferred_element_type=F32)


def _dot_f32(a, b, dims=None):
    dims = dims or (((1,), (0,)), ((), ()))
    return lax.dot_general(a, b, dims, precision=lax.Precision.HIGHEST, preferred_element_type=F32)


def _adaln_kernel(c_ref, w_ref, b_ref, o_ref):
    c = c_ref[...]
    o_ref[...] = _dot_f32(c * _sigmoid(c), w_ref[...]) + b_ref[...]


def _adaln(c, w_ada, b_ada, tn=1536):
    depth, d, n = w_ada.shape
    b = c.shape[0]
    assert n % tn == 0
    return pl.pallas_call(
        _adaln_kernel,
        grid=(depth, n // tn),
        in_specs=[
            pl.BlockSpec((b, d), lambda l, j: (0, 0)),
            pl.BlockSpec((None, d, tn), lambda l, j: (l, 0, j)),
            pl.BlockSpec((None, 1, tn), lambda l, j: (l, 0, j)),
        ],
        out_specs=pl.BlockSpec((None, b, tn), lambda l, j: (l, 0, j)),
        out_shape=jax.ShapeDtypeStruct((depth, b, n), F32),
        compiler_params=_params(("arbitrary", "arbitrary")),
        name="adaln",
    )(c, w_ada, b_ada.reshape(depth, 1, n))


def _inproj_kernel(x_ref, g_ref, mod_ref, w_ref, wif_ref, wift_ref, z_ref, gcol_ref, grow_ref, h_ref):
    @pl.when(pl.program_id(2) == 0)
    def _():
        h = _norm_mod(x_ref[...], g_ref[...], mod_ref[0], mod_ref[1]).astype(BF16)
        h_ref[...] = h
        gcol_ref[...] = _dot(h, wif_ref[...])
        grow_ref[...] = lax.dot_general(wift_ref[...], h, _NT, preferred_element_type=F32)

    z_ref[...] = _dot(h_ref[...], w_ref[...]).astype(z_ref.dtype)


def _inproj(x, g, mod, w_main, w_if, w_ift, tm=1024, tn=1664):
    b, t, d = x.shape
    n = w_main.shape[1]
    ng = w_ift.shape[0]
    tm = min(tm, t)
    assert t % tm == 0 and n % tn == 0
    return pl.pallas_call(
        _inproj_kernel,
        grid=(b, t // tm, n // tn),
        in_specs=[
            pl.BlockSpec((None, tm, d), lambda bi, i, j: (bi, i, 0)),
            pl.BlockSpec((1, d), lambda bi, i, j: (0, 0)),
            pl.BlockSpec((None, 6, 1, d), lambda bi, i, j: (bi, 0, 0, 0)),
            pl.BlockSpec((d, tn), lambda bi, i, j: (0, j)),
            pl.BlockSpec((d, LANES), lambda bi, i, j: (0, 0)),
            pl.BlockSpec((ng, d), lambda bi, i, j: (0, 0)),
        ],
        out_specs=[
            pl.BlockSpec((None, tm, tn), lambda bi, i, j: (bi, i, j)),
            pl.BlockSpec((None, tm, LANES), lambda bi, i, j: (bi, i, 0)),
            pl.BlockSpec((None, ng, tm), lambda bi, i, j: (bi, 0, i)),
        ],
        out_shape=[
            jax.ShapeDtypeStruct((b, t, n), BF16),
            jax.ShapeDtypeStruct((b, t, LANES), F32),
            jax.ShapeDtypeStruct((b, ng, t), F32),
        ],
        scratch_shapes=[pltpu.VMEM((tm, d), BF16)],
        compiler_params=_params(("arbitrary", "arbitrary", "arbitrary")),
        name="inproj",
    )(x, g, mod, w_main, w_if, w_ift)


def _mlstm_kernel(q_ref, k_ref, v_ref, o_ref, gcol_ref, grow_ref, convw_ref, bcol_ref, brow_ref, gain_ref,
                  out_ref, cbuf, c_st, n_st, m_st, *, heads):
    L, W = q_ref.shape
    dh = W // heads
    taps = convw_ref.shape[0]
    halo = SUBLANES

    @pl.when(pl.program_id(1) == 0)
    def _():
        cbuf[:, 0:halo, :] = jnp.zeros((cbuf.shape[0], halo, LANES), F32)
        c_st[...] = jnp.zeros_like(c_st)
        n_st[...] = jnp.zeros_like(n_st)
        m_st[...] = jnp.zeros_like(m_st)

    def conv_silu(src_ref, slab0, scale):
        outs = []
        for cb in range(W // LANES):
            slab = slab0 + cb
            cs = slice(slab * LANES, (slab + 1) * LANES)
            cbuf[slab, halo:halo + L, :] = src_ref[:, cb * LANES:(cb + 1) * LANES].astype(F32)
            acc = convw_ref[taps - 1:taps, cs] * cbuf[slab, halo:halo + L, :]
            for j in range(taps - 1):
                off = halo - (taps - 1) + j
                acc = acc + convw_ref[j:j + 1, cs] * cbuf[slab, off:off + L, :]
            cbuf[slab, 0:halo, :] = cbuf[slab, L:L + halo, :]
            outs.append(acc * _sigmoid(acc) * scale if scale != 1.0 else acc * _sigmoid(acc))
        return outs

    q_slabs = conv_silu(q_ref, 0, dh ** -0.5)
    k_slabs = conv_silu(k_ref, W // LANES, 1.0)
    per_head = dh // LANES

    gc = gcol_ref[...] + bcol_ref[...]
    gr = grow_ref[...] + brow_ref[...]
    row = lax.broadcasted_iota(jnp.int32, (L, L), 0)
    col = lax.broadcasted_iota(jnp.int32, (L, L), 1)
    causal = row >= col
    tri_low = causal.astype(F32)
    tri_up = (row <= col).astype(F32)
    b_cols = _dot_f32(tri_low, _log_sigmoid(gc))
    b_rows = _dot_f32(_log_sigmoid(gr), tri_up)

    for h in range(heads):
        hs = slice(h * dh, (h + 1) * dh)
        q = jnp.concatenate(q_slabs[h * per_head:(h + 1) * per_head], axis=-1)
        k = jnp.concatenate(k_slabs[h * per_head:(h + 1) * per_head], axis=-1)
        v = v_ref[:, hs].astype(F32)
        qb = q.astype(BF16)
        kb = k.astype(BF16)
        li_c = gc[:, h:h + 1]
        b_c = b_cols[:, heads + h:heads + h + 1]
        li_r = gr[h:h + 1, :]
        b_r = b_rows[heads + h:heads + h + 1, :]
        b_tot = b_r[:, L - 1:L]
        c_prev = c_st[h]
        n_prev = n_st[h]
        m_prev = m_st[h][:, 0:1]

        d = jnp.where(causal, b_c - b_r + li_r, -jnp.inf)
        inter_log = b_c + m_prev
        m_comb = jnp.maximum(inter_log, jnp.max(d, axis=-1, keepdims=True))
        s = lax.dot_general(qb, kb, _NT, preferred_element_type=F32) * jnp.exp(d - m_comb)
        w_inter = jnp.exp(inter_log - m_comb)
        num = _dot(s.astype(BF16), v.astype(BF16)) + w_inter * lax.dot_general(
            qb, c_prev.astype(BF16), _NT, preferred_element_type=F32)
        den = jnp.sum(s, axis=-1, keepdims=True) + w_inter * jnp.sum(q * n_prev, axis=-1, keepdims=True)
        den = jnp.maximum(jnp.abs(den), jnp.exp(-m_comb))
        hh = num / den
        hh = hh * lax.rsqrt(jnp.mean(hh * hh, axis=-1, keepdims=True) + EPS)
        gate = _sigmoid(o_ref[:, hs].astype(F32))
        out_ref[:, hs] = (hh * gain_ref[:, hs] * gate).astype(out_ref.dtype)

        a = b_tot - b_c + li_c
        m_loc = jnp.max(a, axis=0, keepdims=True)
        w = jnp.exp(a - m_loc)
        c_loc = lax.dot_general((w * v).astype(BF16), kb, _TN, preferred_element_type=F32)
        n_loc = jnp.sum(w * k, axis=0, keepdims=True)
        m_new = jnp.maximum(b_tot + m_prev, m_loc)
        s_old = jnp.exp(b_tot + m_prev - m_new)
        s_loc = jnp.exp(m_loc - m_new)
        c_st[h] = s_old * c_prev + s_loc * c_loc
        n_st[h] = s_old * n_prev + s_loc * n_loc
        m_st[h] = jnp.broadcast_to(m_new, (1, LANES))


def _mlstm(z, gcol, grow, conv_w, bias_col, bias_row, head_gain, width, heads):
    b, t, _ = z.shape
    L = MLSTM_CHUNK
    dh = width // heads
    ng = grow.shape[1]
    assert t % L == 0 and conv_w.shape[0] - 1 <= SUBLANES and dh % LANES == 0
    zspec = lambda blk: pl.BlockSpec((None, L, width), lambda bi, c, blk=blk: (bi, c, blk))
    return pl.pallas_call(
        functools.partial(_mlstm_kernel, heads=heads),
        grid=(b, t // L),
        in_specs=[
            zspec(0), zspec(1), zspec(2), zspec(3),
            pl.BlockSpec((None, L, LANES), lambda bi, c: (bi, c, 0)),
            pl.BlockSpec((None, ng, L), lambda bi, c: (bi, 0, c)),
            pl.BlockSpec(conv_w.shape, lambda bi, c: (0, 0)),
            pl.BlockSpec((1, LANES), lambda bi, c: (0, 0)),
            pl.BlockSpec((ng, 1), lambda bi, c: (0, 0)),
            pl.BlockSpec((1, width), lambda bi, c: (0, 0)),
        ],
        out_specs=pl.BlockSpec((None, L, width), lambda bi, c: (bi, c, 0)),
        out_shape=jax.ShapeDtypeStruct((b, t, width), BF16),
        scratch_shapes=[
            pltpu.VMEM((2 * width // LANES, SUBLANES + L, LANES), F32),
            pltpu.VMEM((heads, dh, dh), F32),
            pltpu.VMEM((heads, 1, dh), F32),
            pltpu.VMEM((heads, 1, LANES), F32),
        ],
        compiler_params=_params(("arbitrary", "arbitrary")),
        name="mlstm",
    )(z, z, z, z, gcol, grow, conv_w, bias_col, bias_row, head_gain)


def _mixout_kernel(ha_ref, u_ref, ga_ref, gb_ref, x_ref, mod_ref, poolw_ref, pscale_ref, pa_ref, pb_ref, wo_ref,
                   out_ref, ubuf, *, windows):
    tm = x_ref.shape[0]
    gd = poolw_ref.shape[1]
    halo = max(windows)
    i = pl.program_id(1)

    @pl.when(i == 0)
    def _():
        ubuf[:, 0:halo, :] = jnp.zeros((ubuf.shape[0], halo, gd), F32)

    tpos = i * tm + lax.broadcasted_iota(jnp.int32, (tm, 1), 0)
    parts = []
    for g, win in enumerate(windows):
        cur = u_ref[:, g * gd:(g + 1) * gd].astype(F32)
        ubuf[g, halo:halo + tm, :] = cur
        wsum = cur
        for j in range(1, win):
            wsum = wsum + ubuf[g, halo - j:halo - j + tm, :]
        ubuf[g, 0:halo, :] = ubuf[g, tm:tm + halo, :]
        count = jnp.minimum(tpos + 1, win).astype(F32)
        pooled = wsum / count - cur
        parts.append(_dot(pooled.astype(BF16), poolw_ref[g]))
    hb = (jnp.concatenate(parts, axis=-1) * pscale_ref[...]).astype(BF16)

    pa = _dot(ha_ref[...], pa_ref[...])
    pb = _dot(hb, pb_ref[...])
    merged = _sigmoid(ga_ref[...].astype(F32)) * pa + _sigmoid(gb_ref[...].astype(F32)) * pb
    y = _dot(merged.astype(BF16), wo_ref[...])
    out_ref[...] = x_ref[...] + mod_ref[2] * y


def _mixout(h_a, z, x, mod, pool_w, pool_scale, proj_a, proj_b, w_out, width, tm=512):
    b, t, d = x.shape
    p = pool_scale.shape[1]
    tm = min(tm, t)
    ga_blk = 4 * width // d
    u_blk = (4 * width + 2 * d) // p
    assert t % tm == 0 and (4 * width) % d == 0 and (4 * width + 2 * d) % p == 0
    const = lambda shape: pl.BlockSpec(shape, lambda bi, i: (0,) * len(shape))
    return pl.pallas_call(
        functools.partial(_mixout_kernel, windows=POOL_WINDOWS),
        grid=(b, t // tm),
        in_specs=[
            pl.BlockSpec((None, tm, width), lambda bi, i: (bi, i, 0)),
            pl.BlockSpec((None, tm, p), lambda bi, i: (bi, i, u_blk)),
            pl.BlockSpec((None, tm, d), lambda bi, i: (bi, i, ga_blk)),
            pl.BlockSpec((None, tm, d), lambda bi, i: (bi, i, ga_blk + 1)),
            pl.BlockSpec((None, tm, d), lambda bi, i: (bi, i, 0)),
            pl.BlockSpec((None, 6, 1, d), lambda bi, i: (bi, 0, 0, 0)),
            const(pool_w.shape), const(pool_scale.shape), const(proj_a.shape), const(proj_b.shape), const(w_out.shape),
        ],
        out_specs=pl.BlockSpec((None, tm, d), lambda bi, i: (bi, i, 0)),
        out_shape=jax.ShapeDtypeStruct((b, t, d), F32),
        scratch_shapes=[pltpu.VMEM((pool_w.shape[0], max(POOL_WINDOWS) + tm, pool_w.shape[1]), F32)],
        compiler_params=_params(("arbitrary", "arbitrary")),
        name="mixout",
    )(h_a, z, z, z, x, mod, pool_w, pool_scale, proj_a, proj_b, w_out)


def _ffn_kernel(x_ref, g_ref, mod_ref, wg_ref, wu_ref, wd_ref, fin_ref, out_ref, h_ref, acc_ref, *, final_norm):
    j = pl.program_id(2)

    @pl.when(j == 0)
    def _():
        h_ref[...] = _norm_mod(x_ref[...], g_ref[...], mod_ref[3], mod_ref[4]).astype(BF16)
        acc_ref[...] = jnp.zeros_like(acc_ref)

    h = h_ref[...]
    gate = _dot(h, wg_ref[...])
    act = (gate * _sigmoid(gate) * _dot(h, wu_ref[...])).astype(BF16)
    acc_ref[...] += _dot(act, wd_ref[...])

    @pl.when(j == pl.num_programs(2) - 1)
    def _():
        y = x_ref[...] + mod_ref[5] * acc_ref[...]
        out_ref[...] = _rms(y, fin_ref[...]) if final_norm else y


def _ffn(x, g, mod, w_gate, w_up, w_down, fin, final_norm, tm=512, tf=1408):
    b, t, d = x.shape
    ff = w_gate.shape[1]
    tm = min(tm, t)
    tf = min(tf, ff)
    assert t % tm == 0 and ff % tf == 0
    return pl.pallas_call(
        functools.partial(_ffn_kernel, final_norm=final_norm),
        grid=(b, t // tm, ff // tf),
        in_specs=[
            pl.BlockSpec((None, tm, d), lambda bi, i, j: (bi, i, 0)),
            pl.BlockSpec((1, d), lambda bi, i, j: (0, 0)),
            pl.BlockSpec((None, 6, 1, d), lambda bi, i, j: (bi, 0, 0, 0)),
            pl.BlockSpec((d, tf), lambda bi, i, j: (0, j)),
            pl.BlockSpec((d, tf), lambda bi, i, j: (0, j)),
            pl.BlockSpec((tf, d), lambda bi, i, j: (j, 0)),
            pl.BlockSpec((1, d), lambda bi, i, j: (0, 0)),
        ],
        out_specs=pl.BlockSpec((None, tm, d), lambda bi, i, j: (bi, i, 0)),
        out_shape=jax.ShapeDtypeStruct((b, t, d), F32),
        scratch_shapes=[pltpu.VMEM((tm, d), BF16), pltpu.VMEM((tm, d), F32)],
        compiler_params=_params(("arbitrary", "arbitrary", "arbitrary")),
        name="ffn_dense",
    )(x, g, mod, w_gate, w_up, w_down, fin)


def _router_kernel(x_ref, g_ref, mod_ref, rw_ref, rb_ref, h_ref, info_ref, cnt_ref, carry_ref, *, n_experts, cap):
    first = (pl.program_id(0) == 0) & (pl.program_id(1) == 0)

    @pl.when(first)
    def _():
        carry_ref[...] = jnp.zeros_like(carry_ref)

    h = _norm_mod(x_ref[...], g_ref[...], mod_ref[3], mod_ref[4])
    h_ref[...] = h
    tm = h.shape[0]
    lane = lax.broadcasted_iota(jnp.int32, (tm, LANES), 1)
    logits = jnp.where(lane < n_experts, _dot_f32(h, rw_ref[...]) + rb_ref[...], -jnp.inf)
    v1 = jnp.max(logits, axis=-1, keepdims=True)
    i1 = jnp.min(jnp.where(logits == v1, lane, LANES), axis=-1, keepdims=True)
    rest = jnp.where(lane == i1, -jnp.inf, logits)
    v2 = jnp.max(rest, axis=-1, keepdims=True)
    i2 = jnp.min(jnp.where(rest == v2, lane, LANES), axis=-1, keepdims=True)
    e2 = jnp.exp(v2 - v1)
    w1 = 1.0 / (1.0 + e2)
    w2 = e2 / (1.0 + e2)
    sel1 = lane == i1
    sel2 = lane == i2
    picked = jnp.where(sel1 | sel2, 1.0, 0.0)
    row = lax.broadcasted_iota(jnp.int32, (tm, tm), 0)
    col = lax.broadcasted_iota(jnp.int32, (tm, tm), 1)
    before = jnp.where(col < row, 1.0, 0.0).astype(BF16)
    ex = _dot(before, picked.astype(BF16)) + carry_ref[...]
    pos1 = i1.astype(F32) * cap + jnp.sum(jnp.where(sel1, ex, 0.0), axis=-1, keepdims=True)
    pos2 = i2.astype(F32) * cap + jnp.sum(jnp.where(sel2, ex, 0.0), axis=-1, keepdims=True)
    carry = carry_ref[...] + jnp.sum(picked, axis=0, keepdims=True)
    carry_ref[...] = carry
    cnt_ref[...] = carry
    info_ref[...] = (jnp.where(lane == 0, pos1, 0.0) + jnp.where(lane == 1, pos2, 0.0)
                     + jnp.where(lane == 2, w1, 0.0) + jnp.where(lane == 3, w2, 0.0))


def _router(x, g, mod, router_w, router_b, tm=512):
    b, t, d = x.shape
    n_experts = router_w.shape[1]
    tm = min(tm, t)
    assert t % tm == 0 and n_experts <= LANES and n_experts * b * t < 2 ** 24
    rw = jnp.zeros((d, LANES), F32).at[:, :n_experts].set(router_w)
    rb = jnp.zeros((1, LANES), F32).at[:, :n_experts].set(router_b[None, :])
    return pl.pallas_call(
        functools.partial(_router_kernel, n_experts=n_experts, cap=b * t),
        grid=(b, t // tm),
        in_specs=[
            pl.BlockSpec((None, tm, d), lambda bi, i: (bi, i, 0)),
            pl.BlockSpec((1, d), lambda bi, i: (0, 0)),
            pl.BlockSpec((None, 6, 1, d), lambda bi, i: (bi, 0, 0, 0)),
            pl.BlockSpec((d, LANES), lambda bi, i: (0, 0)),
            pl.BlockSpec((1, LANES), lambda bi, i: (0, 0)),
        ],
        out_specs=[
            pl.BlockSpec((None, tm, d), lambda bi, i: (bi, i, 0)),
            pl.BlockSpec((None, tm, LANES), lambda bi, i: (bi, i, 0)),
            pl.BlockSpec((1, LANES), lambda bi, i: (0, 0)),
        ],
        out_shape=[jax.ShapeDtypeStruct((b, t, d), F32), jax.ShapeDtypeStruct((b, t, LANES), F32),
                   jax.ShapeDtypeStruct((1, LANES), F32)],
        scratch_shapes=[pltpu.VMEM((1, LANES), F32)],
        compiler_params=_params(("arbitrary", "arbitrary")),
        name="router",
    )(x, g, mod, rw, rb)


def _row_copy(src, src_row, dst, dst_row, sem):
    return pltpu.make_async_copy(src.at[pl.ds(src_row, 1)], dst.at[pl.ds(dst_row, 1)], sem)


def _scatter_kernel(pos_ref, h_ref, hs_hbm, sem):
    rows = h_ref.shape[0]
    base = pl.program_id(0) * rows

    def copies(r):
        t = base + r
        return (_row_copy(h_ref, r, hs_hbm, pos_ref[TOP_K * t], sem),
                _row_copy(h_ref, r, hs_hbm, pos_ref[TOP_K * t + 1], sem))

    def start(r, carry):
        for cp in copies(r):
            cp.start()
        return carry

    def wait(r, carry):
        for cp in copies(r):
            cp.wait()
        return carry

    lax.fori_loop(0, rows, start, 0, unroll=8)
    lax.fori_loop(0, rows, wait, 0, unroll=8)


def _scatter_rows(pos, h, n_rows_out, rows=512):
    n, d = h.shape
    rows = min(rows, n)
    assert n % rows == 0
    return pl.pallas_call(
        _scatter_kernel,
        grid_spec=pltpu.PrefetchScalarGridSpec(
            num_scalar_prefetch=1,
            grid=(n // rows,),
            in_specs=[pl.BlockSpec((rows, d), lambda i, pos: (i, 0))],
            out_specs=pl.BlockSpec(memory_space=pl.ANY),
            scratch_shapes=[pltpu.SemaphoreType.DMA],
        ),
        out_shape=jax.ShapeDtypeStruct((n_rows_out, d), h.dtype),
        compiler_params=_params(("arbitrary",)),
        name="moe_scatter",
    )(pos, h)


def _moe_ffn_kernel(te_ref, tb_ref, tv_ref, hs_ref, wg_ref, wu_ref, wd_ref, y_ref, xb_ref):
    i = pl.program_id(0)
    j = pl.program_id(1)
    valid = tv_ref[i]

    @pl.when(valid > 0)
    def _():
        @pl.when(j == 0)
        def _():
            row = lax.broadcasted_iota(jnp.int32, (hs_ref.shape[0], 1), 0)
            xb_ref[...] = jnp.where(row < valid, hs_ref[...], 0.0).astype(BF16)
            y_ref[...] = jnp.zeros_like(y_ref)

        tm = xb_ref.shape[0]
        step = tm // MOE_TILE_PARTS
        for part in range(1, MOE_TILE_PARTS + 1):
            rows = part * step

            @pl.when((valid > rows - step) & (valid <= rows))
            def _(rows=rows):
                h = xb_ref[0:rows, :]
                gate = _dot(h, wg_ref[...].astype(BF16))
                act = (gate * _sigmoid(gate) * _dot(h, wu_ref[...].astype(BF16))).astype(BF16)
                y_ref[0:rows, :] += _dot(act, wd_ref[...].astype(BF16))


def _moe_ffn(tile_e, tile_blk, tile_valid, hs, w_gate, w_up, w_down, tm, tf=512):
    _, d = hs.shape
    n_experts, _, ff = w_gate.shape
    tf = min(tf, ff)
    nf = ff // tf
    assert ff % tf == 0
    n_tiles = tile_e.shape[0]
    jj = lambda i, j, tv: jnp.where(tv[i] > 0, j, nf - 1)
    return pl.pallas_call(
        _moe_ffn_kernel,
        grid_spec=pltpu.PrefetchScalarGridSpec(
            num_scalar_prefetch=3,
            grid=(n_tiles, nf),
            in_specs=[
                pl.BlockSpec((tm, d), lambda i, j, te, tb, tv: (tb[i], 0)),
                pl.BlockSpec((None, d, tf), lambda i, j, te, tb, tv: (te[i], 0, jj(i, j, tv))),
                pl.BlockSpec((None, d, tf), lambda i, j, te, tb, tv: (te[i], 0, jj(i, j, tv))),
                pl.BlockSpec((None, tf, d), lambda i, j, te, tb, tv: (te[i], jj(i, j, tv), 0)),
            ],
            out_specs=pl.BlockSpec((tm, d), lambda i, j, te, tb, tv: (tb[i], 0)),
            scratch_shapes=[pltpu.VMEM((tm, d), BF16)],
        ),
        out_shape=jax.ShapeDtypeStruct(hs.shape, F32),
        compiler_params=_params(("arbitrary", "arbitrary")),
        name="moe_ffn",
    )(tile_e, tile_blk, tile_valid, hs, w_gate, w_up, w_down)


def _combine_kernel(pos_ref, x_ref, info_ref, mod_ref, fin_ref, y_hbm, out_ref, ybuf, sem, *, final_norm):
    rows = x_ref.shape[0]
    base = pl.program_id(0) * rows

    def copies(r):
        t = base + r
        return (_row_copy(y_hbm, pos_ref[TOP_K * t], ybuf.at[0], r, sem),
                _row_copy(y_hbm, pos_ref[TOP_K * t + 1], ybuf.at[1], r, sem))

    def start(r, carry):
        for cp in copies(r):
            cp.start()
        return carry

    def wait(r, carry):
        for cp in copies(r):
            cp.wait()
        return carry

    lax.fori_loop(0, rows, start, 0, unroll=8)
    lax.fori_loop(0, rows, wait, 0, unroll=8)
    info = info_ref[...]
    f = info[:, 2:3] * ybuf[0] + info[:, 3:4] * ybuf[1]
    y = x_ref[...] + mod_ref[5] * f
    out_ref[...] = _rms(y, fin_ref[...]) if final_norm else y


def _combine(pos, x, info, mod, fin, y, final_norm, rows=256):
    b, t, d = x.shape
    n = b * t
    rows = min(rows, t)
    assert t % rows == 0
    per_b = t // rows
    return pl.pallas_call(
        functools.partial(_combine_kernel, final_norm=final_norm),
        grid_spec=pltpu.PrefetchScalarGridSpec(
            num_scalar_prefetch=1,
            grid=(n // rows,),
            in_specs=[
                pl.BlockSpec((rows, d), lambda i, pos: (i, 0)),
                pl.BlockSpec((rows, LANES), lambda i, pos: (i, 0)),
                pl.BlockSpec((None, 6, 1, d), lambda i, pos: (i // per_b, 0, 0, 0)),
                pl.BlockSpec((1, d), lambda i, pos: (0, 0)),
                pl.BlockSpec(memory_space=pl.ANY),
            ],
            out_specs=pl.BlockSpec((rows, d), lambda i, pos: (i, 0)),
            scratch_shapes=[pltpu.VMEM((TOP_K, rows, d), F32), pltpu.SemaphoreType.DMA],
        ),
        out_shape=jax.ShapeDtypeStruct((n, d), F32),
        compiler_params=_params(("arbitrary",)),
        name="moe_combine",
    )(pos, x.reshape(n, d), info.reshape(n, LANES), mod, fin, y).reshape(b, t, d)


def _tile_tables(counts, n_experts, cap, tm, n_tiles):
    counts = counts.astype(jnp.int32)
    tiles_per = (counts + tm - 1) // tm
    ends = jnp.cumsum(tiles_per)
    used = ends[-1]
    i = jnp.minimum(jnp.arange(n_tiles, dtype=jnp.int32), used - 1)
    e = jnp.sum((i[:, None] >= ends[None, :]).astype(jnp.int32), axis=1)
    k = i - (ends - tiles_per)[e]
    valid = jnp.where(jnp.arange(n_tiles) < used, jnp.minimum(counts[e] - k * tm, tm), 0)
    return e, e * (cap // tm) + k, valid.astype(jnp.int32)


def _moe(x, g, mod, router_w, router_b, w_gate, w_up, w_down, fin, final_norm, tm=1024):
    b, t, d = x.shape
    n = b * t
    n_experts = router_w.shape[1]
    tm = min(tm, n)
    assert n % tm == 0 and tm % (MOE_TILE_PARTS * 2 * SUBLANES) == 0
    h, info, counts = _router(x, g, mod, router_w, router_b)
    pos = info[:, :, :TOP_K].astype(jnp.int32).reshape(n * TOP_K)
    hs = _scatter_rows(pos, h.reshape(n, d), n_experts * n)
    n_tiles = TOP_K * n // tm + n_experts
    tile_e, tile_blk, tile_valid = _tile_tables(counts[0, :n_experts], n_experts, n, tm, n_tiles)
    y = _moe_ffn(tile_e, tile_blk, tile_valid, hs, w_gate, w_up, w_down, tm)
    return _combine(pos, x, info, mod, fin, y, final_norm)


def kernel(x, c, norm_mix, norm_ffn, w_ada, b_ada, w_in, conv_w, i_bias, f_bias, head_gain, pool_w, pool_scale,
           proj_a, proj_b, w_out, ffn_w_gate, ffn_w_up, ffn_w_down, router_w, router_b, moe_w_gate, moe_w_up,
           moe_w_down, final_norm):
    depth = w_in.shape[0]
    b, t, d = x.shape
    heads = i_bias.shape[1]
    width = head_gain.shape[1]
    p = pool_scale.shape[1]
    ng = 2 * heads
    qkvo = 4 * width

    mod_all = _adaln(c, w_ada, b_ada).reshape(depth, b, 6, 1, d)
    fin = final_norm.reshape(1, d)

    for l in range(depth):
        mod = mod_all[l]
        w = w_in[l]
        w_main = jnp.concatenate([w[:, :qkvo], w[:, qkvo + ng + p:], w[:, qkvo + ng:qkvo + ng + p]], axis=1).astype(BF16)
        w_gates = w[:, qkvo:qkvo + ng]
        w_if = jnp.zeros((d, LANES), BF16).at[:, :ng].set(w_gates.astype(BF16))
        w_ift = w_gates.T.astype(BF16)
        bias = jnp.concatenate([i_bias[l], f_bias[l]])
        bias_col = jnp.zeros((1, LANES), F32).at[0, :ng].set(bias)
        bias_row = bias.reshape(ng, 1)

        z, gcol, grow = _inproj(x, norm_mix[l].reshape(1, d), mod, w_main, w_if, w_ift)
        h_a = _mlstm(z, gcol, grow, conv_w[l], bias_col, bias_row, head_gain[l].reshape(1, width), width, heads)
        x = _mixout(h_a, z, x, mod, pool_w[l].astype(BF16), pool_scale[l].reshape(1, p), proj_a[l].astype(BF16),
                    proj_b[l].astype(BF16), w_out[l].astype(BF16), width)

        last = l == depth - 1
        j = l // 2
        g_ffn = norm_ffn[l].reshape(1, d)
        if l % 2 == 0:
            x = _ffn(x, g_ffn, mod, ffn_w_gate[j].astype(BF16), ffn_w_up[j].astype(BF16),
                     ffn_w_down[j].astype(BF16), fin, last)
        else:
            x = _moe(x, g_ffn, mod, router_w[j], router_b[j], moe_w_gate[j], moe_w_up[j], moe_w_down[j], fin, last)
    return x
```

```python
import functools

import jax
import jax.numpy as jnp
from jax import lax
from jax.experimental import pallas as pl
from jax.experimental.pallas import tpu as pltpu

F32 = jnp.float32
BF16 = jnp.bfloat16

EPS = 1e-6
MLSTM_CHUNK = 128
POOL_WINDOWS = (2, 4, 8, 16)
TOP_K = 2
MOE_TILE_PARTS = 4
LANES = 128
SUBLANES = 8
VMEM_LIMIT = 56 * 1024 * 1024

_NT = (((1,), (1,)), ((), ()))
_TN = (((0,), (0,)), ((), ()))


def _params(sem):
    return pltpu.CompilerParams(dimension_semantics=sem, vmem_limit_bytes=VMEM_LIMIT)


def _sigmoid(x):
    return 1.0 / (1.0 + jnp.exp(-x))


def _log_sigmoid(x):
    return jnp.minimum(x, 0.0) - jnp.log(1.0 + jnp.exp(-jnp.abs(x)))


def _rms(x, g):
    return x * lax.rsqrt(jnp.mean(x * x, axis=-1, keepdims=True) + EPS) * g


def _norm_mod(x, g, shift, scale):
    return _rms(x, g) * (1.0 + scale) + shift


def _dot(a, b):
    return jnp.dot(a, b, preferred_element_type=F32)


def _dot_f32(a, b, dims=None):
    dims = dims or (((1,), (0,)), ((), ()))
    return lax.dot_general(a, b, dims, precision=lax.Precision.HIGHEST, preferred_element_type=F32)


def _adaln_kernel(c_ref, w_ref, b_ref, o_ref):
    c = c_ref[...]
    o_ref[...] = _dot_f32(c * _sigmoid(c), w_ref[...]) + b_ref[...]


def _adaln(c, w_ada, b_ada, tn=1536):
    depth, d, n = w_ada.shape
    b = c.shape[0]
    assert n % tn == 0
    return pl.pallas_call(
        _adaln_kernel,
        grid=(depth, n // tn),
        in_specs=[
            pl.BlockSpec((b, d), lambda l, j: (0, 0)),
            pl.BlockSpec((None, d, tn), lambda l, j: (l, 0, j)),
            pl.BlockSpec((None, 1, tn), lambda l, j: (l, 0, j)),
        ],
        out_specs=pl.BlockSpec((None, b, tn), lambda l, j: (l, 0, j)),
        out_shape=jax.ShapeDtypeStruct((depth, b, n), F32),
        compiler_params=_params(("arbitrary", "arbitrary")),
        name="adaln",
    )(c, w_ada, b_ada.reshape(depth, 1, n))


def _mlstm_chunk(zq, gc, gr, convw_ref, gain_ref, out_ref, r0, cbuf, c_st, n_st, m_st, *, heads, fresh):
    L = MLSTM_CHUNK
    W = out_ref.shape[1]
    dh = W // heads
    taps = convw_ref.shape[0]
    halo = SUBLANES
    rows = slice(r0, r0 + L)

    def carried(x):
        return x if fresh is None else jnp.where(fresh, 0.0, x)

    def conv_silu(col0, slab0, scale):
        outs = []
        for cb in range(W // LANES):
            slab = slab0 + cb
            cs = slice(slab * LANES, (slab + 1) * LANES)
            if fresh is not None:
                cbuf[slab, 0:halo, :] = carried(cbuf[slab, 0:halo, :])
            cbuf[slab, halo:halo + L, :] = zq[rows, col0 + cb * LANES:col0 + (cb + 1) * LANES].astype(F32)
            acc = convw_ref[taps - 1:taps, cs] * cbuf[slab, halo:halo + L, :]
            for j in range(taps - 1):
                off = halo - (taps - 1) + j
                acc = acc + convw_ref[j:j + 1, cs] * cbuf[slab, off:off + L, :]
            cbuf[slab, 0:halo, :] = cbuf[slab, L:L + halo, :]
            outs.append(acc * _sigmoid(acc) * scale if scale != 1.0 else acc * _sigmoid(acc))
        return outs

    q_slabs = conv_silu(0, 0, dh ** -0.5)
    k_slabs = conv_silu(W, W // LANES, 1.0)
    per_head = dh // LANES

    row = lax.broadcasted_iota(jnp.int32, (L, L), 0)
    col = lax.broadcasted_iota(jnp.int32, (L, L), 1)
    causal = row >= col
    tri_low = causal.astype(F32)
    tri_up = (row <= col).astype(F32)
    b_cols = _dot_f32(tri_low, _log_sigmoid(gc))
    b_rows = _dot_f32(_log_sigmoid(gr), tri_up)

    for h in range(heads):
        hs = slice(h * dh, (h + 1) * dh)
        q = jnp.concatenate(q_slabs[h * per_head:(h + 1) * per_head], axis=-1)
        k = jnp.concatenate(k_slabs[h * per_head:(h + 1) * per_head], axis=-1)
        vb = zq[rows, 2 * W + h * dh:2 * W + (h + 1) * dh]
        v = vb.astype(F32)
        qb = q.astype(BF16)
        kb = k.astype(BF16)
        li_c = gc[:, h:h + 1]
        b_c = b_cols[:, heads + h:heads + h + 1]
        li_r = gr[h:h + 1, :]
        b_r = b_rows[heads + h:heads + h + 1, :]
        b_tot = b_r[:, L - 1:L]
        c_prev = carried(c_st[h])
        n_prev = carried(n_st[h])
        m_prev = carried(m_st[h][:, 0:1])

        d = jnp.where(causal, b_c - b_r + li_r, -jnp.inf)
        inter_log = b_c + m_prev
        m_comb = jnp.maximum(inter_log, jnp.max(d, axis=-1, keepdims=True))
        s = lax.dot_general(qb, kb, _NT, preferred_element_type=F32) * jnp.exp(d - m_comb)
        w_inter = jnp.exp(inter_log - m_comb)
        num = _dot(s.astype(BF16), vb) + w_inter * lax.dot_general(
            qb, c_prev.astype(BF16), _NT, preferred_element_type=F32)
        den = jnp.sum(s, axis=-1, keepdims=True) + w_inter * jnp.sum(q * n_prev, axis=-1, keepdims=True)
        den = jnp.maximum(jnp.abs(den), jnp.exp(-m_comb))
        hh = num / den
        hh = hh * lax.rsqrt(jnp.mean(hh * hh, axis=-1, keepdims=True) + EPS)
        gate = _sigmoid(zq[rows, 3 * W + h * dh:3 * W + (h + 1) * dh].astype(F32))
        out_ref[rows, hs] = (hh * gain_ref[:, hs] * gate).astype(out_ref.dtype)

        a = b_tot - b_c + li_c
        m_loc = jnp.max(a, axis=0, keepdims=True)
        w = jnp.exp(a - m_loc)
        c_loc = lax.dot_general((w * v).astype(BF16), kb, _TN, preferred_element_type=F32)
        n_loc = jnp.sum(w * k, axis=0, keepdims=True)
        m_new = jnp.maximum(b_tot + m_prev, m_loc)
        s_old = jnp.exp(b_tot + m_prev - m_new)
        s_loc = jnp.exp(m_loc - m_new)
        c_st[h] = s_old * c_prev + s_loc * c_loc
        n_st[h] = s_old * n_prev + s_loc * n_loc
        m_st[h] = jnp.broadcast_to(m_new, (1, LANES))


def _mixin_kernel(x_ref, g_ref, mod_ref, w_ref, wif_ref, wift_ref, convw_ref, bcol_ref, brow_ref, gain_ref,
                  ha_ref, zr_ref, h_ref, zring, gcring, grring, cbuf, c_st, n_st, m_st,
                  *, heads, tiles_per_seq, col_chunk):
    s = pl.program_id(0)
    tm = x_ref.shape[0]
    W = ha_ref.shape[1]
    L = MLSTM_CHUNK
    n_chunks = tm // L
    qkvo = 4 * W
    slot_a = lax.rem(s, 2)
    slot_b = 1 - slot_a

    @pl.when(s == 0)
    def _():
        zring[1] = jnp.zeros(zring.shape[1:], zring.dtype)
        gcring[1] = jnp.zeros(gcring.shape[1:], F32)
        grring[1] = jnp.zeros(grring.shape[1:], F32)
        cbuf[...] = jnp.zeros_like(cbuf)
        c_st[...] = jnp.zeros_like(c_st)
        n_st[...] = jnp.zeros_like(n_st)
        m_st[...] = jnp.zeros_like(m_st)

    h = _norm_mod(x_ref[...], g_ref[...], mod_ref[0], mod_ref[1]).astype(BF16)
    h_ref[...] = h
    gcring[slot_a] = _dot(h, wif_ref[...])
    grring[slot_a] = lax.dot_general(wift_ref[...], h, _NT, preferred_element_type=F32)

    fresh = lax.rem(s + tiles_per_seq - 1, tiles_per_seq) == 0
    zq = zring.at[slot_b]
    col_starts = list(range(0, w_ref.shape[1], col_chunk))
    share = -(-len(col_starts) // n_chunks)
    for c in range(n_chunks):
        for c0 in col_starts[c * share:(c + 1) * share]:
            zc = _dot(h_ref[...], w_ref[:, c0:c0 + col_chunk]).astype(BF16)
            if c0 < qkvo:
                zring[slot_a, :, c0:c0 + col_chunk] = zc
            else:
                zr_ref[:, c0 - qkvo:c0 - qkvo + col_chunk] = zc
        r0 = c * L
        gc = gcring[slot_b, r0:r0 + L, :] + bcol_ref[...]
        gr = grring[slot_b, :, r0:r0 + L] + brow_ref[...]
        _mlstm_chunk(zq, gc, gr, convw_ref, gain_ref, ha_ref, r0, cbuf, c_st, n_st, m_st,
                     heads=heads, fresh=fresh if c == 0 else None)


def _mixin(x, g, mod, w_main, w_if, w_ift, conv_w, bias_col, bias_row, head_gain, width, heads, tm=512, col_chunk=512):
    b, t, d = x.shape
    n = b * t
    ncols = w_main.shape[1]
    ng = w_ift.shape[0]
    dh = width // heads
    tm = min(tm, t)
    nt = t // tm
    n_tiles = b * nt
    qkvo = 4 * width
    assert t % tm == 0 and tm % MLSTM_CHUNK == 0 and conv_w.shape[0] - 1 <= SUBLANES and dh % LANES == 0
    assert ncols % col_chunk == 0 and qkvo % col_chunk == 0
    cur = lambda s: jnp.minimum(s, n_tiles - 1)
    prev = lambda s: jnp.maximum(s - 1, 0)
    const = lambda shape: pl.BlockSpec(shape, lambda s: (0,) * len(shape))
    return pl.pallas_call(
        functools.partial(_mixin_kernel, heads=heads, tiles_per_seq=nt, col_chunk=col_chunk),
        grid=(n_tiles + 1,),
        in_specs=[
            pl.BlockSpec((tm, d), lambda s: (cur(s), 0)),
            const((1, d)),
            pl.BlockSpec((None, 6, 1, d), lambda s: (cur(s) // nt, 0, 0, 0)),
            pl.BlockSpec((d, ncols), lambda s: (0, 0), pipeline_mode=pl.Buffered(1)),
            const((d, LANES)), const((ng, d)), const(conv_w.shape), const((1, LANES)), const((ng, 1)), const((1, width)),
        ],
        out_specs=[
            pl.BlockSpec((tm, width), lambda s: (prev(s), 0)),
            pl.BlockSpec((tm, ncols - qkvo), lambda s: (cur(s), 0)),
        ],
        out_shape=[
            jax.ShapeDtypeStruct((n, width), BF16),
            jax.ShapeDtypeStruct((n, ncols - qkvo), BF16),
        ],
        scratch_shapes=[
            pltpu.VMEM((tm, d), BF16),
            pltpu.VMEM((2, tm, qkvo), BF16),
            pltpu.VMEM((2, tm, LANES), F32),
            pltpu.VMEM((2, ng, tm), F32),
            pltpu.VMEM((2 * width // LANES, SUBLANES + MLSTM_CHUNK, LANES), F32),
            pltpu.VMEM((heads, dh, dh), F32),
            pltpu.VMEM((heads, 1, dh), F32),
            pltpu.VMEM((heads, 1, LANES), F32),
        ],
        compiler_params=_params(("arbitrary",)),
        name="mixin",
    )(x.reshape(n, d), g, mod, w_main, w_if, w_ift, conv_w, bias_col, bias_row, head_gain)


def _mixout_kernel(ha_ref, u_ref, ga_ref, gb_ref, x_ref, mod_ref, poolw_ref, pscale_ref, pa_ref, pb_ref, wo_ref,
                   out_ref, ubuf, *, windows):
    tm = x_ref.shape[0]
    gd = poolw_ref.shape[1]
    halo = max(windows)
    i = pl.program_id(1)

    @pl.when(i == 0)
    def _():
        ubuf[:, 0:halo, :] = jnp.zeros((ubuf.shape[0], halo, gd), F32)

    tpos = i * tm + lax.broadcasted_iota(jnp.int32, (tm, 1), 0)
    parts = []
    for g, win in enumerate(windows):
        cur = u_ref[:, g * gd:(g + 1) * gd].astype(F32)
        ubuf[g, halo:halo + tm, :] = cur
        wsum = cur
        for j in range(1, win):
            wsum = wsum + ubuf[g, halo - j:halo - j + tm, :]
        ubuf[g, 0:halo, :] = ubuf[g, tm:tm + halo, :]
        count = jnp.minimum(tpos + 1, win).astype(F32)
        pooled = wsum / count - cur
        parts.append(_dot(pooled.astype(BF16), poolw_ref[g]))
    hb = (jnp.concatenate(parts, axis=-1) * pscale_ref[...]).astype(BF16)

    pa = _dot(ha_ref[...], pa_ref[...])
    pb = _dot(hb, pb_ref[...])
    merged = _sigmoid(ga_ref[...].astype(F32)) * pa + _sigmoid(gb_ref[...].astype(F32)) * pb
    y = _dot(merged.astype(BF16), wo_ref[...])
    out_ref[...] = x_ref[...] + mod_ref[2] * y


def _mixout(h_a, zr, x, mod, pool_w, pool_scale, proj_a, proj_b, w_out, tm=512):
    b, t, d = x.shape
    p = pool_scale.shape[1]
    width = h_a.shape[2]
    tm = min(tm, t)
    ga_blk = 0
    u_blk = 2 * d // p
    assert t % tm == 0 and (2 * d) % p == 0
    const = lambda shape: pl.BlockSpec(shape, lambda bi, i: (0,) * len(shape))
    return pl.pallas_call(
        functools.partial(_mixout_kernel, windows=POOL_WINDOWS),
        grid=(b, t // tm),
        in_specs=[
            pl.BlockSpec((None, tm, width), lambda bi, i: (bi, i, 0)),
            pl.BlockSpec((None, tm, p), lambda bi, i: (bi, i, u_blk)),
            pl.BlockSpec((None, tm, d), lambda bi, i: (bi, i, ga_blk)),
            pl.BlockSpec((None, tm, d), lambda bi, i: (bi, i, ga_blk + 1)),
            pl.BlockSpec((None, tm, d), lambda bi, i: (bi, i, 0)),
            pl.BlockSpec((None, 6, 1, d), lambda bi, i: (bi, 0, 0, 0)),
            const(pool_w.shape), const(pool_scale.shape), const(proj_a.shape), const(proj_b.shape), const(w_out.shape),
        ],
        out_specs=pl.BlockSpec((None, tm, d), lambda bi, i: (bi, i, 0)),
        out_shape=jax.ShapeDtypeStruct((b, t, d), F32),
        scratch_shapes=[pltpu.VMEM((pool_w.shape[0], max(POOL_WINDOWS) + tm, pool_w.shape[1]), F32)],
        compiler_params=_params(("arbitrary", "arbitrary")),
        name="mixout",
    )(h_a, zr, zr, zr, x, mod, pool_w, pool_scale, proj_a, proj_b, w_out)


def _ffn_kernel(x_ref, g_ref, mod_ref, wg_ref, wu_ref, wd_ref, fin_ref, out_ref, h_ref, acc_ref, *, final_norm):
    j = pl.program_id(2)

    @pl.when(j == 0)
    def _():
        h_ref[...] = _norm_mod(x_ref[...], g_ref[...], mod_ref[3], mod_ref[4]).astype(BF16)
        acc_ref[...] = jnp.zeros_like(acc_ref)

    h = h_ref[...]
    gate = _dot(h, wg_ref[...])
    act = (gate * _sigmoid(gate) * _dot(h, wu_ref[...])).astype(BF16)
    acc_ref[...] += _dot(act, wd_ref[...])

    @pl.when(j == pl.num_programs(2) - 1)
    def _():
        y = x_ref[...] + mod_ref[5] * acc_ref[...]
        out_ref[...] = _rms(y, fin_ref[...]) if final_norm else y


def _ffn(x, g, mod, w_gate, w_up, w_down, fin, final_norm, tm=512, tf=1408):
    b, t, d = x.shape
    ff = w_gate.shape[1]
    tm = min(tm, t)
    tf = min(tf, ff)
    assert t % tm == 0 and ff % tf == 0
    return pl.pallas_call(
        functools.partial(_ffn_kernel, final_norm=final_norm),
        grid=(b, t // tm, ff // tf),
        in_specs=[
            pl.BlockSpec((None, tm, d), lambda bi, i, j: (bi, i, 0)),
            pl.BlockSpec((1, d), lambda bi, i, j: (0, 0)),
            pl.BlockSpec((None, 6, 1, d), lambda bi, i, j: (bi, 0, 0, 0)),
            pl.BlockSpec((d, tf), lambda bi, i, j: (0, j)),
            pl.BlockSpec((d, tf), lambda bi, i, j: (0, j)),
            pl.BlockSpec((tf, d), lambda bi, i, j: (j, 0)),
            pl.BlockSpec((1, d), lambda bi, i, j: (0, 0)),
        ],
        out_specs=pl.BlockSpec((None, tm, d), lambda bi, i, j: (bi, i, 0)),
        out_shape=jax.ShapeDtypeStruct((b, t, d), F32),
        scratch_shapes=[pltpu.VMEM((tm, d), BF16), pltpu.VMEM((tm, d), F32)],
        compiler_params=_params(("arbitrary", "arbitrary", "arbitrary")),
        name="ffn_dense",
    )(x, g, mod, w_gate, w_up, w_down, fin)


def _router_kernel(x_ref, g_ref, mod_ref, rw_ref, rb_ref, h_ref, info_ref, cnt_ref, carry_ref, *, n_experts, cap):
    first = (pl.program_id(0) == 0) & (pl.program_id(1) == 0)

    @pl.when(first)
    def _():
        carry_ref[...] = jnp.zeros_like(carry_ref)

    h = _norm_mod(x_ref[...], g_ref[...], mod_ref[3], mod_ref[4])
    h_ref[...] = h
    tm = h.shape[0]
    lane = lax.broadcasted_iota(jnp.int32, (tm, LANES), 1)
    logits = jnp.where(lane < n_experts, _dot_f32(h, rw_ref[...]) + rb_ref[...], -jnp.inf)
    v1 = jnp.max(logits, axis=-1, keepdims=True)
    i1 = jnp.min(jnp.where(logits == v1, lane, LANES), axis=-1, keepdims=True)
    rest = jnp.where(lane == i1, -jnp.inf, logits)
    v2 = jnp.max(rest, axis=-1, keepdims=True)
    i2 = jnp.min(jnp.where(rest == v2, lane, LANES), axis=-1, keepdims=True)
    e2 = jnp.exp(v2 - v1)
    w1 = 1.0 / (1.0 + e2)
    w2 = e2 / (1.0 + e2)
    sel1 = lane == i1
    sel2 = lane == i2
    picked = jnp.where(sel1 | sel2, 1.0, 0.0)
    row = lax.broadcasted_iota(jnp.int32, (tm, tm), 0)
    col = lax.broadcasted_iota(jnp.int32, (tm, tm), 1)
    before = jnp.where(col < row, 1.0, 0.0).astype(BF16)
    ex = _dot(before, picked.astype(BF16)) + carry_ref[...]
    pos1 = i1.astype(F32) * cap + jnp.sum(jnp.where(sel1, ex, 0.0), axis=-1, keepdims=True)
    pos2 = i2.astype(F32) * cap + jnp.sum(jnp.where(sel2, ex, 0.0), axis=-1, keepdims=True)
    carry = carry_ref[...] + jnp.sum(picked, axis=0, keepdims=True)
    carry_ref[...] = carry
    cnt_ref[...] = carry
    info_ref[...] = (jnp.where(lane == 0, pos1, 0.0) + jnp.where(lane == 1, pos2, 0.0)
                     + jnp.where(lane == 2, w1, 0.0) + jnp.where(lane == 3, w2, 0.0))


def _router(x, g, mod, router_w, router_b, tm=512):
    b, t, d = x.shape
    n_experts = router_w.shape[1]
    tm = min(tm, t)
    assert t % tm == 0 and n_experts <= LANES and n_experts * b * t < 2 ** 24
    rw = jnp.zeros((d, LANES), F32).at[:, :n_experts].set(router_w)
    rb = jnp.zeros((1, LANES), F32).at[:, :n_experts].set(router_b[None, :])
    return pl.pallas_call(
        functools.partial(_router_kernel, n_experts=n_experts, cap=b * t),
        grid=(b, t // tm),
        in_specs=[
            pl.BlockSpec((None, tm, d), lambda bi, i: (bi, i, 0)),
            pl.BlockSpec((1, d), lambda bi, i: (0, 0)),
            pl.BlockSpec((None, 6, 1, d), lambda bi, i: (bi, 0, 0, 0)),
            pl.BlockSpec((d, LANES), lambda bi, i: (0, 0)),
            pl.BlockSpec((1, LANES), lambda bi, i: (0, 0)),
        ],
        out_specs=[
            pl.BlockSpec((None, tm, d), lambda bi, i: (bi, i, 0)),
            pl.BlockSpec((None, tm, LANES), lambda bi, i: (bi, i, 0)),
            pl.BlockSpec((1, LANES), lambda bi, i: (0, 0)),
        ],
        out_shape=[jax.ShapeDtypeStruct((b, t, d), F32), jax.ShapeDtypeStruct((b, t, LANES), F32),
                   jax.ShapeDtypeStruct((1, LANES), F32)],
        scratch_shapes=[pltpu.VMEM((1, LANES), F32)],
        compiler_params=_params(("arbitrary", "arbitrary")),
        name="router",
    )(x, g, mod, rw, rb)


def _row_copy(src, src_row, dst, dst_row, sem):
    return pltpu.make_async_copy(src.at[pl.ds(src_row, 1)], dst.at[pl.ds(dst_row, 1)], sem)


def _scatter_kernel(pos_ref, h_ref, hs_hbm, sem):
    rows = h_ref.shape[0]
    base = pl.program_id(0) * rows

    def copies(r):
        t = base + r
        return (_row_copy(h_ref, r, hs_hbm, pos_ref[TOP_K * t], sem),
                _row_copy(h_ref, r, hs_hbm, pos_ref[TOP_K * t + 1], sem))

    def start(r, carry):
        for cp in copies(r):
            cp.start()
        return carry

    def wait(r, carry):
        for cp in copies(r):
            cp.wait()
        return carry

    lax.fori_loop(0, rows, start, 0, unroll=8)
    lax.fori_loop(0, rows, wait, 0, unroll=8)


def _scatter_rows(pos, h, n_rows_out, rows=512):
    n, d = h.shape
    rows = min(rows, n)
    assert n % rows == 0
    return pl.pallas_call(
        _scatter_kernel,
        grid_spec=pltpu.PrefetchScalarGridSpec(
            num_scalar_prefetch=1,
            grid=(n // rows,),
            in_specs=[pl.BlockSpec((rows, d), lambda i, pos: (i, 0))],
            out_specs=pl.BlockSpec(memory_space=pl.ANY),
            scratch_shapes=[pltpu.SemaphoreType.DMA],
        ),
        out_shape=jax.ShapeDtypeStruct((n_rows_out, d), h.dtype),
        compiler_params=_params(("arbitrary",)),
        name="moe_scatter",
    )(pos, h)


def _moe_ffn_kernel(te_ref, tb_ref, tv_ref, hs_ref, wg_ref, wu_ref, wd_ref, y_ref, xb_ref):
    i = pl.program_id(0)
    j = pl.program_id(1)
    valid = tv_ref[i]

    @pl.when(valid > 0)
    def _():
        @pl.when(j == 0)
        def _():
            row = lax.broadcasted_iota(jnp.int32, (hs_ref.shape[0], 1), 0)
            xb_ref[...] = jnp.where(row < valid, hs_ref[...], 0.0).astype(BF16)
            y_ref[...] = jnp.zeros_like(y_ref)

        tm = xb_ref.shape[0]
        step = tm // MOE_TILE_PARTS
        for part in range(1, MOE_TILE_PARTS + 1):
            rows = part * step

            @pl.when((valid > rows - step) & (valid <= rows))
            def _(rows=rows):
                h = xb_ref[0:rows, :]
                gate = _dot(h, wg_ref[...].astype(BF16))
                act = (gate * _sigmoid(gate) * _dot(h, wu_ref[...].astype(BF16))).astype(BF16)
                y_ref[0:rows, :] += _dot(act, wd_ref[...].astype(BF16))


def _moe_ffn(tile_e, tile_blk, tile_valid, hs, w_gate, w_up, w_down, tm, tf=512):
    _, d = hs.shape
    n_experts, _, ff = w_gate.shape
    tf = min(tf, ff)
    nf = ff // tf
    assert ff % tf == 0
    n_tiles = tile_e.shape[0]
    jj = lambda i, j, tv: jnp.where(tv[i] > 0, j, nf - 1)
    return pl.pallas_call(
        _moe_ffn_kernel,
        grid_spec=pltpu.PrefetchScalarGridSpec(
            num_scalar_prefetch=3,
            grid=(n_tiles, nf),
            in_specs=[
                pl.BlockSpec((tm, d), lambda i, j, te, tb, tv: (tb[i], 0)),
                pl.BlockSpec((None, d, tf), lambda i, j, te, tb, tv: (te[i], 0, jj(i, j, tv))),
                pl.BlockSpec((None, d, tf), lambda i, j, te, tb, tv: (te[i], 0, jj(i, j, tv))),
                pl.BlockSpec((None, tf, d), lambda i, j, te, tb, tv: (te[i], jj(i, j, tv), 0)),
            ],
            out_specs=pl.BlockSpec((tm, d), lambda i, j, te, tb, tv: (tb[i], 0)),
            scratch_shapes=[pltpu.VMEM((tm, d), BF16)],
        ),
        out_shape=jax.ShapeDtypeStruct(hs.shape, F32),
        compiler_params=_params(("arbitrary", "arbitrary")),
        name="moe_ffn",
    )(tile_e, tile_blk, tile_valid, hs, w_gate, w_up, w_down)


def _combine_kernel(pos_ref, x_ref, info_ref, mod_ref, fin_ref, y_hbm, out_ref, ybuf, sem, *, final_norm):
    rows = x_ref.shape[0]
    base = pl.program_id(0) * rows

    def copies(r):
        t = base + r
        return (_row_copy(y_hbm, pos_ref[TOP_K * t], ybuf.at[0], r, sem),
                _row_copy(y_hbm, pos_ref[TOP_K * t + 1], ybuf.at[1], r, sem))

    def start(r, carry):
        for cp in copies(r):
            cp.start()
        return carry

    def wait(r, carry):
        for cp in copies(r):
            cp.wait()
        return carry

    lax.fori_loop(0, rows, start, 0, unroll=8)
    lax.fori_loop(0, rows, wait, 0, unroll=8)
    info = info_ref[...]
    f = info[:, 2:3] * ybuf[0] + info[:, 3:4] * ybuf[1]
    y = x_ref[...] + mod_ref[5] * f
    out_ref[...] = _rms(y, fin_ref[...]) if final_norm else y


def _combine(pos, x, info, mod, fin, y, final_norm, rows=256):
    b, t, d = x.shape
    n = b * t
    rows = min(rows, t)
    assert t % rows == 0
    per_b = t // rows
    return pl.pallas_call(
        functools.partial(_combine_kernel, final_norm=final_norm),
        grid_spec=pltpu.PrefetchScalarGridSpec(
            num_scalar_prefetch=1,
            grid=(n // rows,),
            in_specs=[
                pl.BlockSpec((rows, d), lambda i, pos: (i, 0)),
                pl.BlockSpec((rows, LANES), lambda i, pos: (i, 0)),
                pl.BlockSpec((None, 6, 1, d), lambda i, pos: (i // per_b, 0, 0, 0)),
                pl.BlockSpec((1, d), lambda i, pos: (0, 0)),
                pl.BlockSpec(memory_space=pl.ANY),
            ],
            out_specs=pl.BlockSpec((rows, d), lambda i, pos: (i, 0)),
            scratch_shapes=[pltpu.VMEM((TOP_K, rows, d), F32), pltpu.SemaphoreType.DMA],
        ),
        out_shape=jax.ShapeDtypeStruct((n, d), F32),
        compiler_params=_params(("arbitrary",)),
        name="moe_combine",
    )(pos, x.reshape(n, d), info.reshape(n, LANES), mod, fin, y).reshape(b, t, d)


def _tile_tables(counts, n_experts, cap, tm, n_tiles):
    counts = counts.astype(jnp.int32)
    tiles_per = (counts + tm - 1) // tm
    ends = jnp.cumsum(tiles_per)
    used = ends[-1]
    i = jnp.minimum(jnp.arange(n_tiles, dtype=jnp.int32), used - 1)
    e = jnp.sum((i[:, None] >= ends[None, :]).astype(jnp.int32), axis=1)
    k = i - (ends - tiles_per)[e]
    valid = jnp.where(jnp.arange(n_tiles) < used, jnp.minimum(counts[e] - k * tm, tm), 0)
    return e, e * (cap // tm) + k, valid.astype(jnp.int32)


def _moe(x, g, mod, router_w, router_b, w_gate, w_up, w_down, fin, final_norm, tm=1024):
    b, t, d = x.shape
    n = b * t
    n_experts = router_w.shape[1]
    tm = min(tm, n)
    assert n % tm == 0 and tm % (MOE_TILE_PARTS * 2 * SUBLANES) == 0
    h, info, counts = _router(x, g, mod, router_w, router_b)
    pos = info[:, :, :TOP_K].astype(jnp.int32).reshape(n * TOP_K)
    hs = _scatter_rows(pos, h.reshape(n, d), n_experts * n)
    n_tiles = TOP_K * n // tm + n_experts
    tile_e, tile_blk, tile_valid = _tile_tables(counts[0, :n_experts], n_experts, n, tm, n_tiles)
    y = _moe_ffn(tile_e, tile_blk, tile_valid, hs, w_gate, w_up, w_down, tm)
    return _combine(pos, x, info, mod, fin, y, final_norm)


def kernel(x, c, norm_mix, norm_ffn, w_ada, b_ada, w_in, conv_w, i_bias, f_bias, head_gain, pool_w, pool_scale,
           proj_a, proj_b, w_out, ffn_w_gate, ffn_w_up, ffn_w_down, router_w, router_b, moe_w_gate, moe_w_up,
           moe_w_down, final_norm):
    depth = w_in.shape[0]
    b, t, d = x.shape
    heads = i_bias.shape[1]
    width = head_gain.shape[1]
    p = pool_scale.shape[1]
    ng = 2 * heads
    qkvo = 4 * width

    mod_all = _adaln(c, w_ada, b_ada).reshape(depth, b, 6, 1, d)
    fin = final_norm.reshape(1, d)

    for l in range(depth):
        mod = mod_all[l]
        w = w_in[l]
        w_main = jnp.concatenate([w[:, :qkvo], w[:, qkvo + ng + p:], w[:, qkvo + ng:qkvo + ng + p]], axis=1).astype(BF16)
        w_gates = w[:, qkvo:qkvo + ng]
        w_if = jnp.zeros((d, LANES), BF16).at[:, :ng].set(w_gates.astype(BF16))
        w_ift = w_gates.T.astype(BF16)
        bias = jnp.concatenate([i_bias[l], f_bias[l]])
        bias_col = jnp.zeros((1, LANES), F32).at[0, :ng].set(bias)
        bias_row = bias.reshape(ng, 1)

        h_a, zr = _mixin(x, norm_mix[l].reshape(1, d), mod, w_main, w_if, w_ift, conv_w[l], bias_col, bias_row,
                         head_gain[l].reshape(1, width), width, heads)
        x = _mixout(h_a.reshape(b, t, width), zr.reshape(b, t, -1), x, mod, pool_w[l].astype(BF16),
                    pool_scale[l].reshape(1, p), proj_a[l].astype(BF16), proj_b[l].astype(BF16), w_out[l].astype(BF16))

        last = l == depth - 1
        j = l // 2
        g_ffn = norm_ffn[l].reshape(1, d)
        if l % 2 == 0:
            x = _ffn(x, g_ffn, mod, ffn_w_gate[j].astype(BF16), ffn_w_up[j].astype(BF16),
                     ffn_w_down[j].astype(BF16), fin, last)
        else:
            x = _moe(x, g_ffn, mod, router_w[j], router_b[j], moe_w_gate[j], moe_w_up[j], moe_w_down[j], fin, last)
    return x
```

```python
import functools

import jax
import jax.numpy as jnp
from jax import lax
from jax.experimental import pallas as pl
from jax.experimental.pallas import tpu as pltpu

F32 = jnp.float32
BF16 = jnp.bfloat16

EPS = 1e-6
MLSTM_CHUNK = 128
POOL_WINDOWS = (2, 4, 8, 16)
TOP_K = 2
MOE_TILE_PARTS = 4
LANES = 128
SUBLANES = 8
VMEM_LIMIT = 56 * 1024 * 1024

_NT = (((1,), (1,)), ((), ()))
_TN = (((0,), (0,)), ((), ()))


def _params(sem):
    return pltpu.CompilerParams(dimension_semantics=sem, vmem_limit_bytes=VMEM_LIMIT)


def _sigmoid(x):
    return 1.0 / (1.0 + jnp.exp(-x))


def _log_sigmoid(x):
    return jnp.minimum(x, 0.0) - jnp.log(1.0 + jnp.exp(-jnp.abs(x)))


def _rms(x, g):
    return x * lax.rsqrt(jnp.mean(x * x, axis=-1, keepdims=True) + EPS) * g


def _norm_mod(x, g, shift, scale):
    return _rms(x, g) * (1.0 + scale) + shift


def _dot(a, b):
    return jnp.dot(a, b, preferred_element_type=F32)


def _dot_f32(a, b, dims=None):
    dims = dims or (((1,), (0,)), ((), ()))
    return lax.dot_general(a, b, dims, precision=lax.Precision.HIGHEST, preferred_element_type=F32)


def _adaln_kernel(c_ref, w_ref, b_ref, o_ref):
    c = c_ref[...]
    o_ref[...] = _dot_f32(c * _sigmoid(c), w_ref[...]) + b_ref[...]


def _adaln(c, w_ada, b_ada, tn=1536):
    depth, d, n = w_ada.shape
    b = c.shape[0]
    assert n % tn == 0
    return pl.pallas_call(
        _adaln_kernel,
        grid=(depth, n // tn),
        in_specs=[
            pl.BlockSpec((b, d), lambda l, j: (0, 0)),
            pl.BlockSpec((None, d, tn), lambda l, j: (l, 0, j)),
            pl.BlockSpec((None, 1, tn), lambda l, j: (l, 0, j)),
        ],
        out_specs=pl.BlockSpec((None, b, tn), lambda l, j: (l, 0, j)),
        out_shape=jax.ShapeDtypeStruct((depth, b, n), F32),
        compiler_params=_params(("arbitrary", "arbitrary")),
        name="adaln",
    )(c, w_ada, b_ada.reshape(depth, 1, n))


def _mlstm_chunk(zq, gc, gr, convw_ref, gain_ref, out_ref, r0, cbuf, c_st, n_st, m_st, *, heads, fresh):
    L = MLSTM_CHUNK
    W = out_ref.shape[1]
    dh = W // heads
    taps = convw_ref.shape[0]
    halo = SUBLANES
    rows = slice(r0, r0 + L)

    def carried(x):
        return x if fresh is None else jnp.where(fresh, 0.0, x)

    def conv_silu(col0, slab0, scale):
        outs = []
        for cb in range(W // LANES):
            slab = slab0 + cb
            cs = slice(slab * LANES, (slab + 1) * LANES)
            if fresh is not None:
                cbuf[slab, 0:halo, :] = carried(cbuf[slab, 0:halo, :])
            cbuf[slab, halo:halo + L, :] = zq[rows, col0 + cb * LANES:col0 + (cb + 1) * LANES].astype(F32)
            acc = convw_ref[taps - 1:taps, cs] * cbuf[slab, halo:halo + L, :]
            for j in range(taps - 1):
                off = halo - (taps - 1) + j
                acc = acc + convw_ref[j:j + 1, cs] * cbuf[slab, off:off + L, :]
            cbuf[slab, 0:halo, :] = cbuf[slab, L:L + halo, :]
            outs.append(acc * _sigmoid(acc) * scale if scale != 1.0 else acc * _sigmoid(acc))
        return outs

    q_slabs = conv_silu(0, 0, dh ** -0.5)
    k_slabs = conv_silu(W, W // LANES, 1.0)
    per_head = dh // LANES

    row = lax.broadcasted_iota(jnp.int32, (L, L), 0)
    col = lax.broadcasted_iota(jnp.int32, (L, L), 1)
    causal = row >= col
    tri_low = causal.astype(F32)
    tri_up = (row <= col).astype(F32)
    b_cols = _dot_f32(tri_low, _log_sigmoid(gc))
    b_rows = _dot_f32(_log_sigmoid(gr), tri_up)

    for h in range(heads):
        hs = slice(h * dh, (h + 1) * dh)
        q = jnp.concatenate(q_slabs[h * per_head:(h + 1) * per_head], axis=-1)
        k = jnp.concatenate(k_slabs[h * per_head:(h + 1) * per_head], axis=-1)
        vb = zq[rows, 2 * W + h * dh:2 * W + (h + 1) * dh]
        v = vb.astype(F32)
        qb = q.astype(BF16)
        kb = k.astype(BF16)
        li_c = gc[:, h:h + 1]
        b_c = b_cols[:, heads + h:heads + h + 1]
        li_r = gr[h:h + 1, :]
        b_r = b_rows[heads + h:heads + h + 1, :]
        b_tot = b_r[:, L - 1:L]
        c_prev = carried(c_st[h])
        n_prev = carried(n_st[h])
        m_prev = carried(m_st[h][:, 0:1])

        d = jnp.where(causal, b_c - b_r + li_r, -jnp.inf)
        inter_log = b_c + m_prev
        m_comb = jnp.maximum(inter_log, jnp.max(d, axis=-1, keepdims=True))
        s = lax.dot_general(qb, kb, _NT, preferred_element_type=F32) * jnp.exp(d - m_comb)
        w_inter = jnp.exp(inter_log - m_comb)
        num = _dot(s.astype(BF16), vb) + w_inter * lax.dot_general(
            qb, c_prev.astype(BF16), _NT, preferred_element_type=F32)
        den = jnp.sum(s, axis=-1, keepdims=True) + w_inter * jnp.sum(q * n_prev, axis=-1, keepdims=True)
        den = jnp.maximum(jnp.abs(den), jnp.exp(-m_comb))
        hh = num / den
        hh = hh * lax.rsqrt(jnp.mean(hh * hh, axis=-1, keepdims=True) + EPS)
        gate = _sigmoid(zq[rows, 3 * W + h * dh:3 * W + (h + 1) * dh].astype(F32))
        out_ref[rows, hs] = (hh * gain_ref[:, hs] * gate).astype(out_ref.dtype)

        a = b_tot - b_c + li_c
        m_loc = jnp.max(a, axis=0, keepdims=True)
        w = jnp.exp(a - m_loc)
        c_loc = lax.dot_general((w * v).astype(BF16), kb, _TN, preferred_element_type=F32)
        n_loc = jnp.sum(w * k, axis=0, keepdims=True)
        m_new = jnp.maximum(b_tot + m_prev, m_loc)
        s_old = jnp.exp(b_tot + m_prev - m_new)
        s_loc = jnp.exp(m_loc - m_new)
        c_st[h] = s_old * c_prev + s_loc * c_loc
        n_st[h] = s_old * n_prev + s_loc * n_loc
        m_st[h] = jnp.broadcast_to(m_new, (1, LANES))


def _mixin_kernel(x_ref, g_ref, mod_ref, w_ref, wif_ref, wift_ref, convw_ref, bcol_ref, brow_ref, gain_ref,
                  ha_ref, zr_ref, h_ref, zring, gcring, grring, cbuf, c_st, n_st, m_st,
                  *, heads, tiles_per_seq, col_chunk):
    s = pl.program_id(0)
    tm = x_ref.shape[0]
    W = ha_ref.shape[1]
    L = MLSTM_CHUNK
    n_chunks = tm // L
    qkvo = 4 * W
    slot_a = lax.rem(s, 2)
    slot_b = 1 - slot_a

    @pl.when(s == 0)
    def _():
        zring[1] = jnp.zeros(zring.shape[1:], zring.dtype)
        gcring[1] = jnp.zeros(gcring.shape[1:], F32)
        grring[1] = jnp.zeros(grring.shape[1:], F32)
        cbuf[...] = jnp.zeros_like(cbuf)
        c_st[...] = jnp.zeros_like(c_st)
        n_st[...] = jnp.zeros_like(n_st)
        m_st[...] = jnp.zeros_like(m_st)

    h = _norm_mod(x_ref[...], g_ref[...], mod_ref[0], mod_ref[1]).astype(BF16)
    h_ref[...] = h
    gcring[slot_a] = _dot(h, wif_ref[...])
    grring[slot_a] = lax.dot_general(wift_ref[...], h, _NT, preferred_element_type=F32)

    fresh = lax.rem(s + tiles_per_seq - 1, tiles_per_seq) == 0
    zq = zring.at[slot_b]
    col_starts = list(range(0, w_ref.shape[1], col_chunk))
    share = -(-len(col_starts) // n_chunks)
    for c in range(n_chunks):
        for c0 in col_starts[c * share:(c + 1) * share]:
            zc = _dot(h_ref[...], w_ref[:, c0:c0 + col_chunk]).astype(BF16)
            if c0 < qkvo:
                zring[slot_a, :, c0:c0 + col_chunk] = zc
            else:
                zr_ref[:, c0 - qkvo:c0 - qkvo + col_chunk] = zc
        r0 = c * L
        gc = gcring[slot_b, r0:r0 + L, :] + bcol_ref[...]
        gr = grring[slot_b, :, r0:r0 + L] + brow_ref[...]
        _mlstm_chunk(zq, gc, gr, convw_ref, gain_ref, ha_ref, r0, cbuf, c_st, n_st, m_st,
                     heads=heads, fresh=fresh if c == 0 else None)


def _mixin(x, g, mod, w_main, layer, w_if, w_ift, conv_w, bias_col, bias_row, head_gain, width, heads, tm=512,
           col_chunk=512):
    b, t, d = x.shape
    n = b * t
    ncols = w_main.shape[2]
    ng = w_ift.shape[0]
    dh = width // heads
    tm = min(tm, t)
    nt = t // tm
    n_tiles = b * nt
    qkvo = 4 * width
    assert t % tm == 0 and tm % MLSTM_CHUNK == 0 and conv_w.shape[0] - 1 <= SUBLANES and dh % LANES == 0
    assert ncols % col_chunk == 0 and qkvo % col_chunk == 0
    cur = lambda s: jnp.minimum(s, n_tiles - 1)
    prev = lambda s: jnp.maximum(s - 1, 0)
    const = lambda shape: pl.BlockSpec(shape, lambda s: (0,) * len(shape))
    return pl.pallas_call(
        functools.partial(_mixin_kernel, heads=heads, tiles_per_seq=nt, col_chunk=col_chunk),
        grid=(n_tiles + 1,),
        in_specs=[
            pl.BlockSpec((tm, d), lambda s: (cur(s), 0)),
            const((1, d)),
            pl.BlockSpec((None, 6, 1, d), lambda s: (cur(s) // nt, 0, 0, 0)),
            pl.BlockSpec((None, d, ncols), lambda s: (layer, 0, 0), pipeline_mode=pl.Buffered(1)),
            const((d, LANES)), const((ng, d)), const(conv_w.shape), const((1, LANES)), const((ng, 1)), const((1, width)),
        ],
        out_specs=[
            pl.BlockSpec((tm, width), lambda s: (prev(s), 0)),
            pl.BlockSpec((tm, ncols - qkvo), lambda s: (cur(s), 0)),
        ],
        out_shape=[
            jax.ShapeDtypeStruct((n, width), BF16),
            jax.ShapeDtypeStruct((n, ncols - qkvo), BF16),
        ],
        scratch_shapes=[
            pltpu.VMEM((tm, d), BF16),
            pltpu.VMEM((2, tm, qkvo), BF16),
            pltpu.VMEM((2, tm, LANES), F32),
            pltpu.VMEM((2, ng, tm), F32),
            pltpu.VMEM((2 * width // LANES, SUBLANES + MLSTM_CHUNK, LANES), F32),
            pltpu.VMEM((heads, dh, dh), F32),
            pltpu.VMEM((heads, 1, dh), F32),
            pltpu.VMEM((heads, 1, LANES), F32),
        ],
        compiler_params=_params(("arbitrary",)),
        name="mixin",
    )(x.reshape(n, d), g, mod, w_main, w_if, w_ift, conv_w, bias_col, bias_row, head_gain)


def _mixout_kernel(ha_ref, u_ref, ga_ref, gb_ref, x_ref, mod_ref, poolw_ref, pscale_ref, pa_ref, pb_ref, wo_ref,
                   out_ref, ubuf, *, windows):
    tm = x_ref.shape[0]
    gd = poolw_ref.shape[1]
    halo = max(windows)
    i = pl.program_id(1)

    @pl.when(i == 0)
    def _():
        ubuf[:, 0:halo, :] = jnp.zeros((ubuf.shape[0], halo, gd), F32)

    tpos = i * tm + lax.broadcasted_iota(jnp.int32, (tm, 1), 0)
    parts = []
    for g, win in enumerate(windows):
        cur = u_ref[:, g * gd:(g + 1) * gd].astype(F32)
        ubuf[g, halo:halo + tm, :] = cur
        wsum = cur
        for j in range(1, win):
            wsum = wsum + ubuf[g, halo - j:halo - j + tm, :]
        ubuf[g, 0:halo, :] = ubuf[g, tm:tm + halo, :]
        count = jnp.minimum(tpos + 1, win).astype(F32)
        pooled = wsum / count - cur
        parts.append(_dot(pooled.astype(BF16), poolw_ref[g]))
    hb = (jnp.concatenate(parts, axis=-1) * pscale_ref[...]).astype(BF16)

    pa = _dot(ha_ref[...], pa_ref[...])
    pb = _dot(hb, pb_ref[...])
    merged = _sigmoid(ga_ref[...].astype(F32)) * pa + _sigmoid(gb_ref[...].astype(F32)) * pb
    y = _dot(merged.astype(BF16), wo_ref[...])
    out_ref[...] = x_ref[...] + mod_ref[2] * y


def _mixout(h_a, zr, x, mod, pool_w, pool_scale, proj_a, proj_b, w_out, tm=512):
    b, t, d = x.shape
    p = pool_scale.shape[1]
    width = h_a.shape[2]
    tm = min(tm, t)
    ga_blk = 0
    u_blk = 2 * d // p
    assert t % tm == 0 and (2 * d) % p == 0
    const = lambda shape: pl.BlockSpec(shape, lambda bi, i: (0,) * len(shape))
    return pl.pallas_call(
        functools.partial(_mixout_kernel, windows=POOL_WINDOWS),
        grid=(b, t // tm),
        in_specs=[
            pl.BlockSpec((None, tm, width), lambda bi, i: (bi, i, 0)),
            pl.BlockSpec((None, tm, p), lambda bi, i: (bi, i, u_blk)),
            pl.BlockSpec((None, tm, d), lambda bi, i: (bi, i, ga_blk)),
            pl.BlockSpec((None, tm, d), lambda bi, i: (bi, i, ga_blk + 1)),
            pl.BlockSpec((None, tm, d), lambda bi, i: (bi, i, 0)),
            pl.BlockSpec((None, 6, 1, d), lambda bi, i: (bi, 0, 0, 0)),
            const(pool_w.shape), const(pool_scale.shape), const(proj_a.shape), const(proj_b.shape), const(w_out.shape),
        ],
        out_specs=pl.BlockSpec((None, tm, d), lambda bi, i: (bi, i, 0)),
        out_shape=jax.ShapeDtypeStruct((b, t, d), F32),
        scratch_shapes=[pltpu.VMEM((pool_w.shape[0], max(POOL_WINDOWS) + tm, pool_w.shape[1]), F32)],
        compiler_params=_params(("arbitrary", "arbitrary")),
        name="mixout",
    )(h_a, zr, zr, zr, x, mod, pool_w, pool_scale, proj_a, proj_b, w_out)


def _ffn_kernel(x_ref, g_ref, mod_ref, wg_ref, wu_ref, wd_ref, fin_ref, out_ref, *, final_norm, ff_chunk):
    x = x_ref[...]
    h = _norm_mod(x, g_ref[...], mod_ref[3], mod_ref[4]).astype(BF16)
    acc = None
    for c0 in range(0, wg_ref.shape[1], ff_chunk):
        gate = _dot(h, wg_ref[:, c0:c0 + ff_chunk])
        act = (gate * _sigmoid(gate) * _dot(h, wu_ref[:, c0:c0 + ff_chunk])).astype(BF16)
        part = _dot(act, wd_ref[c0:c0 + ff_chunk, :])
        acc = part if acc is None else acc + part
    y = x + mod_ref[5] * acc
    out_ref[...] = _rms(y, fin_ref[...]) if final_norm else y


def _ffn(x, g, mod, w_gate, w_up, w_down, fin, final_norm, tm=512, ff_chunk=1408):
    b, t, d = x.shape
    ff = w_gate.shape[1]
    tm = min(tm, t)
    ff_chunk = min(ff_chunk, ff)
    assert t % tm == 0 and ff % ff_chunk == 0
    const = lambda shape: pl.BlockSpec(shape, lambda bi, i: (0,) * len(shape))
    return pl.pallas_call(
        functools.partial(_ffn_kernel, final_norm=final_norm, ff_chunk=ff_chunk),
        grid=(b, t // tm),
        in_specs=[
            pl.BlockSpec((None, tm, d), lambda bi, i: (bi, i, 0)),
            const((1, d)),
            pl.BlockSpec((None, 6, 1, d), lambda bi, i: (bi, 0, 0, 0)),
            const((d, ff)), const((d, ff)), const((ff, d)), const((1, d)),
        ],
        out_specs=pl.BlockSpec((None, tm, d), lambda bi, i: (bi, i, 0)),
        out_shape=jax.ShapeDtypeStruct((b, t, d), F32),
        compiler_params=_params(("arbitrary", "arbitrary")),
        name="ffn_dense",
    )(x, g, mod, w_gate, w_up, w_down, fin)


def _to_slabs(ref, x):
    rows, d = x.shape
    per = d // LANES
    for k in range(per):
        ref[pl.ds(k, rows, stride=per), :] = x[:, k * LANES:(k + 1) * LANES]


def _from_slabs(ref, rows):
    per = ref.shape[0] // rows
    return jnp.concatenate([ref[pl.ds(k, rows, stride=per), :] for k in range(per)], axis=-1)


def _router_kernel(x_ref, g_ref, mod_ref, rw_ref, rb_ref, h_ref, info_ref, cnt_ref, carry_ref, *, n_experts, cap):
    first = (pl.program_id(0) == 0) & (pl.program_id(1) == 0)

    @pl.when(first)
    def _():
        carry_ref[...] = jnp.zeros_like(carry_ref)

    h = _norm_mod(x_ref[...], g_ref[...], mod_ref[3], mod_ref[4])
    _to_slabs(h_ref, h)
    tm = h.shape[0]
    lane = lax.broadcasted_iota(jnp.int32, (tm, LANES), 1)
    logits = jnp.where(lane < n_experts, _dot_f32(h, rw_ref[...]) + rb_ref[...], -jnp.inf)
    v1 = jnp.max(logits, axis=-1, keepdims=True)
    i1 = jnp.min(jnp.where(logits == v1, lane, LANES), axis=-1, keepdims=True)
    rest = jnp.where(lane == i1, -jnp.inf, logits)
    v2 = jnp.max(rest, axis=-1, keepdims=True)
    i2 = jnp.min(jnp.where(rest == v2, lane, LANES), axis=-1, keepdims=True)
    e2 = jnp.exp(v2 - v1)
    w1 = 1.0 / (1.0 + e2)
    w2 = e2 / (1.0 + e2)
    sel1 = lane == i1
    sel2 = lane == i2
    picked = jnp.where(sel1 | sel2, 1.0, 0.0)
    row = lax.broadcasted_iota(jnp.int32, (tm, tm), 0)
    col = lax.broadcasted_iota(jnp.int32, (tm, tm), 1)
    before = jnp.where(col < row, 1.0, 0.0).astype(BF16)
    ex = _dot(before, picked.astype(BF16)) + carry_ref[...]
    pos1 = i1.astype(F32) * cap + jnp.sum(jnp.where(sel1, ex, 0.0), axis=-1, keepdims=True)
    pos2 = i2.astype(F32) * cap + jnp.sum(jnp.where(sel2, ex, 0.0), axis=-1, keepdims=True)
    carry = carry_ref[...] + jnp.sum(picked, axis=0, keepdims=True)
    carry_ref[...] = carry
    cnt_ref[...] = carry
    info_ref[...] = (jnp.where(lane == 0, pos1, 0.0) + jnp.where(lane == 1, pos2, 0.0)
                     + jnp.where(lane == 2, w1, 0.0) + jnp.where(lane == 3, w2, 0.0))


def _router(x, g, mod, router_w, router_b, tm=512):
    b, t, d = x.shape
    n_experts = router_w.shape[1]
    tm = min(tm, t)
    nt = t // tm
    per = d // LANES
    assert t % tm == 0 and d % LANES == 0 and n_experts <= LANES and n_experts * b * t < 2 ** 24
    rw = jnp.zeros((d, LANES), F32).at[:, :n_experts].set(router_w)
    rb = jnp.zeros((1, LANES), F32).at[:, :n_experts].set(router_b[None, :])
    return pl.pallas_call(
        functools.partial(_router_kernel, n_experts=n_experts, cap=b * t),
        grid=(b, t // tm),
        in_specs=[
            pl.BlockSpec((None, tm, d), lambda bi, i: (bi, i, 0)),
            pl.BlockSpec((1, d), lambda bi, i: (0, 0)),
            pl.BlockSpec((None, 6, 1, d), lambda bi, i: (bi, 0, 0, 0)),
            pl.BlockSpec((d, LANES), lambda bi, i: (0, 0)),
            pl.BlockSpec((1, LANES), lambda bi, i: (0, 0)),
        ],
        out_specs=[
            pl.BlockSpec((tm * per, LANES), lambda bi, i: (bi * nt + i, 0)),
            pl.BlockSpec((None, tm, LANES), lambda bi, i: (bi, i, 0)),
            pl.BlockSpec((1, LANES), lambda bi, i: (0, 0)),
        ],
        out_shape=[jax.ShapeDtypeStruct((b * t * per, LANES), F32), jax.ShapeDtypeStruct((b, t, LANES), F32),
                   jax.ShapeDtypeStruct((1, LANES), F32)],
        scratch_shapes=[pltpu.VMEM((1, LANES), F32)],
        compiler_params=_params(("arbitrary", "arbitrary")),
        name="router",
    )(x, g, mod, rw, rb)


def _token_copy(src, src_tok, dst, dst_tok, sem, per):
    return pltpu.make_async_copy(src.at[pl.ds(pl.multiple_of(src_tok * per, per), per)],
                                 dst.at[pl.ds(pl.multiple_of(dst_tok * per, per), per)], sem)


def _scatter_kernel(pos_ref, h_ref, hs_hbm, sem, *, per):
    rows = h_ref.shape[0] // per
    base = pl.program_id(0) * rows

    def copies(r):
        t = base + r
        return (_token_copy(h_ref, r, hs_hbm, pos_ref[TOP_K * t], sem, per),
                _token_copy(h_ref, r, hs_hbm, pos_ref[TOP_K * t + 1], sem, per))

    def start(r, carry):
        for cp in copies(r):
            cp.start()
        return carry

    def wait(r, carry):
        for cp in copies(r):
            cp.wait()
        return carry

    lax.fori_loop(0, rows, start, 0, unroll=8)
    lax.fori_loop(0, rows, wait, 0, unroll=8)


def _scatter_rows(pos, h, n, n_slots_out, rows=512):
    per = h.shape[0] // n
    rows = min(rows, n)
    assert n % rows == 0
    return pl.pallas_call(
        functools.partial(_scatter_kernel, per=per),
        grid_spec=pltpu.PrefetchScalarGridSpec(
            num_scalar_prefetch=1,
            grid=(n // rows,),
            in_specs=[pl.BlockSpec((rows * per, LANES), lambda i, pos: (i, 0))],
            out_specs=pl.BlockSpec(memory_space=pl.ANY),
            scratch_shapes=[pltpu.SemaphoreType.DMA],
        ),
        out_shape=jax.ShapeDtypeStruct((n_slots_out * per, LANES), h.dtype),
        compiler_params=_params(("arbitrary",)),
        name="moe_scatter",
    )(pos, h)


def _moe_ffn_kernel(te_ref, tb_ref, tv_ref, hs_ref, wg_ref, wu_ref, wd_ref, y_ref, xb_ref, acc_ref):
    i = pl.program_id(0)
    j = pl.program_id(1)
    valid = tv_ref[i]

    @pl.when(valid > 0)
    def _():
        @pl.when(j == 0)
        def _():
            row = lax.broadcasted_iota(jnp.int32, (xb_ref.shape[0], 1), 0)
            xb_ref[...] = jnp.where(row < valid, _from_slabs(hs_ref, xb_ref.shape[0]), 0.0).astype(BF16)
            acc_ref[...] = jnp.zeros_like(acc_ref)

        tm = xb_ref.shape[0]
        step = tm // MOE_TILE_PARTS
        for part in range(1, MOE_TILE_PARTS + 1):
            rows = part * step

            @pl.when((valid > rows - step) & (valid <= rows))
            def _(rows=rows):
                h = xb_ref[0:rows, :]
                gate = _dot(h, wg_ref[...].astype(BF16))
                act = (gate * _sigmoid(gate) * _dot(h, wu_ref[...].astype(BF16))).astype(BF16)
                acc_ref[0:rows, :] += _dot(act, wd_ref[...].astype(BF16))

        @pl.when(j == pl.num_programs(1) - 1)
        def _():
            _to_slabs(y_ref, acc_ref[...])


def _moe_ffn(tile_e, tile_blk, tile_valid, hs, w_gate, w_up, w_down, tm, tf=512):
    n_experts, d, ff = w_gate.shape
    per = d // LANES
    tf = min(tf, ff)
    nf = ff // tf
    assert ff % tf == 0
    n_tiles = tile_e.shape[0]
    jj = lambda i, j, tv: jnp.where(tv[i] > 0, j, nf - 1)
    return pl.pallas_call(
        _moe_ffn_kernel,
        grid_spec=pltpu.PrefetchScalarGridSpec(
            num_scalar_prefetch=3,
            grid=(n_tiles, nf),
            in_specs=[
                pl.BlockSpec((tm * per, LANES), lambda i, j, te, tb, tv: (tb[i], 0)),
                pl.BlockSpec((None, d, tf), lambda i, j, te, tb, tv: (te[i], 0, jj(i, j, tv))),
                pl.BlockSpec((None, d, tf), lambda i, j, te, tb, tv: (te[i], 0, jj(i, j, tv))),
                pl.BlockSpec((None, tf, d), lambda i, j, te, tb, tv: (te[i], jj(i, j, tv), 0)),
            ],
            out_specs=pl.BlockSpec((tm * per, LANES), lambda i, j, te, tb, tv: (tb[i], 0)),
            scratch_shapes=[pltpu.VMEM((tm, d), BF16), pltpu.VMEM((tm, d), F32)],
        ),
        out_shape=jax.ShapeDtypeStruct(hs.shape, F32),
        compiler_params=_params(("arbitrary", "arbitrary")),
        name="moe_ffn",
    )(tile_e, tile_blk, tile_valid, hs, w_gate, w_up, w_down)


def _combine_kernel(pos_ref, x_ref, info_ref, mod_ref, fin_ref, y_hbm, out_ref, ybuf, sem, *, final_norm):
    rows = x_ref.shape[0]
    base = pl.program_id(0) * rows

    per = ybuf.shape[1] // rows

    def copies(r):
        t = base + r
        return (_token_copy(y_hbm, pos_ref[TOP_K * t], ybuf.at[0], r, sem, per),
                _token_copy(y_hbm, pos_ref[TOP_K * t + 1], ybuf.at[1], r, sem, per))

    def start(r, carry):
        for cp in copies(r):
            cp.start()
        return carry

    def wait(r, carry):
        for cp in copies(r):
            cp.wait()
        return carry

    lax.fori_loop(0, rows, start, 0, unroll=8)
    lax.fori_loop(0, rows, wait, 0, unroll=8)
    info = info_ref[...]
    f = info[:, 2:3] * _from_slabs(ybuf.at[0], rows) + info[:, 3:4] * _from_slabs(ybuf.at[1], rows)
    y = x_ref[...] + mod_ref[5] * f
    out_ref[...] = _rms(y, fin_ref[...]) if final_norm else y


def _combine(pos, x, info, mod, fin, y, final_norm, rows=256):
    b, t, d = x.shape
    n = b * t
    rows = min(rows, t)
    assert t % rows == 0
    per_b = t // rows
    return pl.pallas_call(
        functools.partial(_combine_kernel, final_norm=final_norm),
        grid_spec=pltpu.PrefetchScalarGridSpec(
            num_scalar_prefetch=1,
            grid=(n // rows,),
            in_specs=[
                pl.BlockSpec((rows, d), lambda i, pos: (i, 0)),
                pl.BlockSpec((rows, LANES), lambda i, pos: (i, 0)),
                pl.BlockSpec((None, 6, 1, d), lambda i, pos: (i // per_b, 0, 0, 0)),
                pl.BlockSpec((1, d), lambda i, pos: (0, 0)),
                pl.BlockSpec(memory_space=pl.ANY),
            ],
            out_specs=pl.BlockSpec((rows, d), lambda i, pos: (i, 0)),
            scratch_shapes=[pltpu.VMEM((TOP_K, rows * (d // LANES), LANES), F32), pltpu.SemaphoreType.DMA],
        ),
        out_shape=jax.ShapeDtypeStruct((n, d), F32),
        compiler_params=_params(("arbitrary",)),
        name="moe_combine",
    )(pos, x.reshape(n, d), info.reshape(n, LANES), mod, fin, y).reshape(b, t, d)


def _tile_tables(counts, n_experts, cap, tm, n_tiles):
    counts = counts.astype(jnp.int32)
    tiles_per = (counts + tm - 1) // tm
    ends = jnp.cumsum(tiles_per)
    used = ends[-1]
    i = jnp.minimum(jnp.arange(n_tiles, dtype=jnp.int32), used - 1)
    e = jnp.sum((i[:, None] >= ends[None, :]).astype(jnp.int32), axis=1)
    k = i - (ends - tiles_per)[e]
    valid = jnp.where(jnp.arange(n_tiles) < used, jnp.minimum(counts[e] - k * tm, tm), 0)
    return e, e * (cap // tm) + k, valid.astype(jnp.int32)


def _moe(x, g, mod, router_w, router_b, w_gate, w_up, w_down, fin, final_norm, tm=1024):
    b, t, d = x.shape
    n = b * t
    n_experts = router_w.shape[1]
    tm = min(tm, n)
    assert n % tm == 0 and tm % (MOE_TILE_PARTS * 2 * SUBLANES) == 0
    h, info, counts = _router(x, g, mod, router_w, router_b)
    pos = info[:, :, :TOP_K].astype(jnp.int32).reshape(n * TOP_K)
    hs = _scatter_rows(pos, h, n, n_experts * n)
    n_tiles = TOP_K * n // tm + n_experts
    tile_e, tile_blk, tile_valid = _tile_tables(counts[0, :n_experts], n_experts, n, tm, n_tiles)
    y = _moe_ffn(tile_e, tile_blk, tile_valid, hs, w_gate, w_up, w_down, tm)
    return _combine(pos, x, info, mod, fin, y, final_norm)


def kernel(x, c, norm_mix, norm_ffn, w_ada, b_ada, w_in, conv_w, i_bias, f_bias, head_gain, pool_w, pool_scale,
           proj_a, proj_b, w_out, ffn_w_gate, ffn_w_up, ffn_w_down, router_w, router_b, moe_w_gate, moe_w_up,
           moe_w_down, final_norm):
    depth = w_in.shape[0]
    b, t, d = x.shape
    heads = i_bias.shape[1]
    width = head_gain.shape[1]
    p = pool_scale.shape[1]
    ng = 2 * heads
    qkvo = 4 * width

    mod_all = _adaln(c, w_ada, b_ada).reshape(depth, b, 6, 1, d)
    fin = final_norm.reshape(1, d)

    for l in range(depth):
        mod = mod_all[l]
        w = w_in[l]
        w_main = jnp.concatenate([w[:, :qkvo], w[:, qkvo + ng + p:], w[:, qkvo + ng:qkvo + ng + p]], axis=1).astype(BF16)
        w_gates = w[:, qkvo:qkvo + ng]
        w_if = jnp.zeros((d, LANES), BF16).at[:, :ng].set(w_gates.astype(BF16))
        w_ift = w_gates.T.astype(BF16)
        bias = jnp.concatenate([i_bias[l], f_bias[l]])
        bias_col = jnp.zeros((1, LANES), F32).at[0, :ng].set(bias)
        bias_row = bias.reshape(ng, 1)

        h_a, zr = _mixin(x, norm_mix[l].reshape(1, d), mod, w_main[None], 0, w_if, w_ift, conv_w[l], bias_col, bias_row,
                         head_gain[l].reshape(1, width), width, heads)
        x = _mixout(h_a.reshape(b, t, width), zr.reshape(b, t, -1), x, mod, pool_w[l].astype(BF16),
                    pool_scale[l].reshape(1, p), proj_a[l].astype(BF16), proj_b[l].astype(BF16), w_out[l].astype(BF16))

        last = l == depth - 1
        j = l // 2
        g_ffn = norm_ffn[l].reshape(1, d)
        if l % 2 == 0:
            x = _ffn(x, g_ffn, mod, ffn_w_gate[j].astype(BF16), ffn_w_up[j].astype(BF16),
                     ffn_w_down[j].astype(BF16), fin, last)
        else:
            x = _moe(x, g_ffn, mod, router_w[j], router_b[j], moe_w_gate[j], moe_w_up[j], moe_w_down[j], fin, last)
    return x
```

```python
import functools

import jax
import jax.numpy as jnp
from jax import lax
from jax.experimental import pallas as pl
from jax.experimental.pallas import tpu as pltpu

F32 = jnp.float32
BF16 = jnp.bfloat16

EPS = 1e-6
MLSTM_CHUNK = 128
POOL_WINDOWS = (2, 4, 8, 16)
TOP_K = 2
MOE_TILE_PARTS = 4
LANES = 128
SUBLANES = 8
VMEM_LIMIT = 56 * 1024 * 1024

_NT = (((1,), (1,)), ((), ()))
_TN = (((0,), (0,)), ((), ()))


def _params(sem):
    return pltpu.CompilerParams(dimension_semantics=sem, vmem_limit_bytes=VMEM_LIMIT)


def _sigmoid(x):
    return 1.0 / (1.0 + jnp.exp(-x))


def _log_sigmoid(x):
    return jnp.minimum(x, 0.0) - jnp.log(1.0 + jnp.exp(-jnp.abs(x)))


def _rms(x, g):
    return x * lax.rsqrt(jnp.mean(x * x, axis=-1, keepdims=True) + EPS) * g


def _norm_mod(x, g, shift, scale):
    return _rms(x, g) * (1.0 + scale) + shift


def _dot(a, b):
    return jnp.dot(a, b, preferred_element_type=F32)


def _dot_f32(a, b, dims=None):
    dims = dims or (((1,), (0,)), ((), ()))
    return lax.dot_general(a, b, dims, precision=lax.Precision.HIGHEST, preferred_element_type=F32)


def _adaln_kernel(c_ref, w_ref, b_ref, o_ref):
    c = c_ref[...]
    o_ref[...] = _dot_f32(c * _sigmoid(c), w_ref[...]) + b_ref[...]


def _adaln(c, w_ada, b_ada, tn=1536):
    depth, d, n = w_ada.shape
    b = c.shape[0]
    assert n % tn == 0
    return pl.pallas_call(
        _adaln_kernel,
        grid=(depth, n // tn),
        in_specs=[
            pl.BlockSpec((b, d), lambda l, j: (0, 0)),
            pl.BlockSpec((None, d, tn), lambda l, j: (l, 0, j)),
            pl.BlockSpec((None, 1, tn), lambda l, j: (l, 0, j)),
        ],
        out_specs=pl.BlockSpec((None, b, tn), lambda l, j: (l, 0, j)),
        out_shape=jax.ShapeDtypeStruct((depth, b, n), F32),
        compiler_params=_params(("arbitrary", "arbitrary")),
        name="adaln",
    )(c, w_ada, b_ada.reshape(depth, 1, n))


def _mlstm_chunk(zq, gc, gr, convw_ref, gain_ref, out_ref, r0, cbuf, c_st, n_st, m_st, *, heads, fresh):
    L = MLSTM_CHUNK
    W = out_ref.shape[1]
    dh = W // heads
    taps = convw_ref.shape[0]
    halo = SUBLANES
    rows = slice(r0, r0 + L)

    def carried(x):
        return x if fresh is None else jnp.where(fresh, 0.0, x)

    def conv_silu(col0, slab0, scale):
        outs = []
        for cb in range(W // LANES):
            slab = slab0 + cb
            cs = slice(slab * LANES, (slab + 1) * LANES)
            if fresh is not None:
                cbuf[slab, 0:halo, :] = carried(cbuf[slab, 0:halo, :])
            cbuf[slab, halo:halo + L, :] = zq[rows, col0 + cb * LANES:col0 + (cb + 1) * LANES].astype(F32)
            acc = convw_ref[taps - 1:taps, cs] * cbuf[slab, halo:halo + L, :]
            for j in range(taps - 1):
                off = halo - (taps - 1) + j
                acc = acc + convw_ref[j:j + 1, cs] * cbuf[slab, off:off + L, :]
            cbuf[slab, 0:halo, :] = cbuf[slab, L:L + halo, :]
            outs.append(acc * _sigmoid(acc) * scale if scale != 1.0 else acc * _sigmoid(acc))
        return outs

    q_slabs = conv_silu(0, 0, dh ** -0.5)
    k_slabs = conv_silu(W, W // LANES, 1.0)
    per_head = dh // LANES

    row = lax.broadcasted_iota(jnp.int32, (L, L), 0)
    col = lax.broadcasted_iota(jnp.int32, (L, L), 1)
    causal = row >= col
    tri_low = causal.astype(F32)
    tri_up = (row <= col).astype(F32)
    b_cols = _dot_f32(tri_low, _log_sigmoid(gc))
    b_rows = _dot_f32(_log_sigmoid(gr), tri_up)

    for h in range(heads):
        hs = slice(h * dh, (h + 1) * dh)
        q = jnp.concatenate(q_slabs[h * per_head:(h + 1) * per_head], axis=-1)
        k = jnp.concatenate(k_slabs[h * per_head:(h + 1) * per_head], axis=-1)
        vb = zq[rows, 2 * W + h * dh:2 * W + (h + 1) * dh]
        v = vb.astype(F32)
        qb = q.astype(BF16)
        kb = k.astype(BF16)
        li_c = gc[:, h:h + 1]
        b_c = b_cols[:, heads + h:heads + h + 1]
        li_r = gr[h:h + 1, :]
        b_r = b_rows[heads + h:heads + h + 1, :]
        b_tot = b_r[:, L - 1:L]
        c_prev = carried(c_st[h])
        n_prev = carried(n_st[h])
        m_prev = carried(m_st[h][:, 0:1])

        d = jnp.where(causal, b_c - b_r + li_r, -jnp.inf)
        inter_log = b_c + m_prev
        m_comb = jnp.maximum(inter_log, jnp.max(d, axis=-1, keepdims=True))
        s = lax.dot_general(qb, kb, _NT, preferred_element_type=F32) * jnp.exp(d - m_comb)
        w_inter = jnp.exp(inter_log - m_comb)
        num = _dot(s.astype(BF16), vb) + w_inter * lax.dot_general(
            qb, c_prev.astype(BF16), _NT, preferred_element_type=F32)
        den = jnp.sum(s, axis=-1, keepdims=True) + w_inter * jnp.sum(q * n_prev, axis=-1, keepdims=True)
        den = jnp.maximum(jnp.abs(den), jnp.exp(-m_comb))
        hh = num / den
        hh = hh * lax.rsqrt(jnp.mean(hh * hh, axis=-1, keepdims=True) + EPS)
        gate = _sigmoid(zq[rows, 3 * W + h * dh:3 * W + (h + 1) * dh].astype(F32))
        out_ref[rows, hs] = (hh * gain_ref[:, hs] * gate).astype(out_ref.dtype)

        a = b_tot - b_c + li_c
        m_loc = jnp.max(a, axis=0, keepdims=True)
        w = jnp.exp(a - m_loc)
        c_loc = lax.dot_general((w * v).astype(BF16), kb, _TN, preferred_element_type=F32)
        n_loc = jnp.sum(w * k, axis=0, keepdims=True)
        m_new = jnp.maximum(b_tot + m_prev, m_loc)
        s_old = jnp.exp(b_tot + m_prev - m_new)
        s_loc = jnp.exp(m_loc - m_new)
        c_st[h] = s_old * c_prev + s_loc * c_loc
        n_st[h] = s_old * n_prev + s_loc * n_loc
        m_st[h] = jnp.broadcast_to(m_new, (1, LANES))


def _cast_kernel(src_ref, dst_ref):
    dst_ref[...] = src_ref[...].astype(dst_ref.dtype)


def _cast_leading_cols(w, cols, tn=512):
    depth, rows, _ = w.shape
    tn = min(tn, cols)
    assert cols % tn == 0
    return pl.pallas_call(
        _cast_kernel,
        grid=(depth, cols // tn),
        in_specs=[pl.BlockSpec((None, rows, tn), lambda l, j: (l, 0, j))],
        out_specs=pl.BlockSpec((None, rows, tn), lambda l, j: (l, 0, j)),
        out_shape=jax.ShapeDtypeStruct((depth, rows, cols), BF16),
        compiler_params=_params(("arbitrary", "arbitrary")),
        name="cast_cols",
    )(w)


def _mixin_kernel(x_ref, g_ref, mod_ref, wq_ref, wr_ref, wif_ref, wift_ref, convw_ref, bcol_ref, brow_ref, gain_ref,
                  ha_ref, zr_ref, h_ref, zring, gcring, grring, cbuf, c_st, n_st, m_st,
                  *, heads, tiles_per_seq, col_chunk):
    s = pl.program_id(0)
    tm = x_ref.shape[0]
    W = ha_ref.shape[1]
    L = MLSTM_CHUNK
    n_chunks = tm // L
    qkvo = 4 * W
    slot_a = lax.rem(s, 2)
    slot_b = 1 - slot_a

    @pl.when(s == 0)
    def _():
        zring[1] = jnp.zeros(zring.shape[1:], zring.dtype)
        gcring[1] = jnp.zeros(gcring.shape[1:], F32)
        grring[1] = jnp.zeros(grring.shape[1:], F32)
        cbuf[...] = jnp.zeros_like(cbuf)
        c_st[...] = jnp.zeros_like(c_st)
        n_st[...] = jnp.zeros_like(n_st)
        m_st[...] = jnp.zeros_like(m_st)

    h = _norm_mod(x_ref[...], g_ref[...], mod_ref[0], mod_ref[1]).astype(BF16)
    h_ref[...] = h
    gcring[slot_a] = _dot(h, wif_ref[...])
    grring[slot_a] = lax.dot_general(wift_ref[...], h, _NT, preferred_element_type=F32)

    fresh = lax.rem(s + tiles_per_seq - 1, tiles_per_seq) == 0
    zq = zring.at[slot_b]
    col_starts = list(range(0, qkvo + wr_ref.shape[1], col_chunk))
    share = -(-len(col_starts) // n_chunks)
    for c in range(n_chunks):
        for c0 in col_starts[c * share:(c + 1) * share]:
            if c0 < qkvo:
                zring[slot_a, :, c0:c0 + col_chunk] = _dot(h_ref[...], wq_ref[:, c0:c0 + col_chunk]).astype(BF16)
            else:
                cr = slice(c0 - qkvo, c0 - qkvo + col_chunk)
                zr_ref[:, cr] = _dot(h_ref[...], wr_ref[:, cr]).astype(BF16)
        r0 = c * L
        gc = gcring[slot_b, r0:r0 + L, :] + bcol_ref[...]
        gr = grring[slot_b, :, r0:r0 + L] + brow_ref[...]
        _mlstm_chunk(zq, gc, gr, convw_ref, gain_ref, ha_ref, r0, cbuf, c_st, n_st, m_st,
                     heads=heads, fresh=fresh if c == 0 else None)


def _mixin(x, g, mod, w_qkvo, layer, w_rest, w_if, w_ift, conv_w, bias_col, bias_row, head_gain, width, heads, tm=512,
           col_chunk=512):
    b, t, d = x.shape
    n = b * t
    ncols = w_qkvo.shape[2] + w_rest.shape[1]
    ng = w_ift.shape[0]
    dh = width // heads
    tm = min(tm, t)
    nt = t // tm
    n_tiles = b * nt
    qkvo = 4 * width
    assert t % tm == 0 and tm % MLSTM_CHUNK == 0 and conv_w.shape[0] - 1 <= SUBLANES and dh % LANES == 0
    assert ncols % col_chunk == 0 and qkvo % col_chunk == 0 and w_qkvo.shape[2] == qkvo
    cur = lambda s: jnp.minimum(s, n_tiles - 1)
    prev = lambda s: jnp.maximum(s - 1, 0)
    const = lambda shape: pl.BlockSpec(shape, lambda s: (0,) * len(shape))
    return pl.pallas_call(
        functools.partial(_mixin_kernel, heads=heads, tiles_per_seq=nt, col_chunk=col_chunk),
        grid=(n_tiles + 1,),
        in_specs=[
            pl.BlockSpec((tm, d), lambda s: (cur(s), 0)),
            const((1, d)),
            pl.BlockSpec((None, 6, 1, d), lambda s: (cur(s) // nt, 0, 0, 0)),
            pl.BlockSpec((None, d, qkvo), lambda s: (layer, 0, 0), pipeline_mode=pl.Buffered(1)),
            pl.BlockSpec((d, ncols - qkvo), lambda s: (0, 0), pipeline_mode=pl.Buffered(1)),
            const((d, LANES)), const((ng, d)), const(conv_w.shape), const((1, LANES)), const((ng, 1)), const((1, width)),
        ],
        out_specs=[
            pl.BlockSpec((tm, width), lambda s: (prev(s), 0)),
            pl.BlockSpec((tm, ncols - qkvo), lambda s: (cur(s), 0)),
        ],
        out_shape=[
            jax.ShapeDtypeStruct((n, width), BF16),
            jax.ShapeDtypeStruct((n, ncols - qkvo), BF16),
        ],
        scratch_shapes=[
            pltpu.VMEM((tm, d), BF16),
            pltpu.VMEM((2, tm, qkvo), BF16),
            pltpu.VMEM((2, tm, LANES), F32),
            pltpu.VMEM((2, ng, tm), F32),
            pltpu.VMEM((2 * width // LANES, SUBLANES + MLSTM_CHUNK, LANES), F32),
            pltpu.VMEM((heads, dh, dh), F32),
            pltpu.VMEM((heads, 1, dh), F32),
            pltpu.VMEM((heads, 1, LANES), F32),
        ],
        compiler_params=_params(("arbitrary",)),
        name="mixin",
    )(x.reshape(n, d), g, mod, w_qkvo, w_rest, w_if, w_ift, conv_w, bias_col, bias_row, head_gain)


def _mixout_kernel(ha_ref, u_ref, ga_ref, gb_ref, x_ref, mod_ref, poolw_ref, pscale_ref, pa_ref, pb_ref, wo_ref,
                   out_ref, ubuf, *, windows):
    tm = x_ref.shape[0]
    gd = poolw_ref.shape[1]
    halo = max(windows)
    i = pl.program_id(1)

    @pl.when(i == 0)
    def _():
        ubuf[:, 0:halo, :] = jnp.zeros((ubuf.shape[0], halo, gd), F32)

    tpos = i * tm + lax.broadcasted_iota(jnp.int32, (tm, 1), 0)
    parts = []
    for g, win in enumerate(windows):
        cur = u_ref[:, g * gd:(g + 1) * gd].astype(F32)
        ubuf[g, halo:halo + tm, :] = cur
        wsum = cur
        for j in range(1, win):
            wsum = wsum + ubuf[g, halo - j:halo - j + tm, :]
        ubuf[g, 0:halo, :] = ubuf[g, tm:tm + halo, :]
        count = jnp.minimum(tpos + 1, win).astype(F32)
        pooled = wsum / count - cur
        parts.append(_dot(pooled.astype(BF16), poolw_ref[g]))
    hb = (jnp.concatenate(parts, axis=-1) * pscale_ref[...]).astype(BF16)

    pa = _dot(ha_ref[...], pa_ref[...])
    pb = _dot(hb, pb_ref[...])
    merged = _sigmoid(ga_ref[...].astype(F32)) * pa + _sigmoid(gb_ref[...].astype(F32)) * pb
    y = _dot(merged.astype(BF16), wo_ref[...])
    out_ref[...] = x_ref[...] + mod_ref[2] * y


def _mixout(h_a, zr, x, mod, pool_w, pool_scale, proj_a, proj_b, w_out, tm=512):
    b, t, d = x.shape
    p = pool_scale.shape[1]
    width = h_a.shape[2]
    tm = min(tm, t)
    ga_blk = 0
    u_blk = 2 * d // p
    assert t % tm == 0 and (2 * d) % p == 0
    const = lambda shape: pl.BlockSpec(shape, lambda bi, i: (0,) * len(shape))
    return pl.pallas_call(
        functools.partial(_mixout_kernel, windows=POOL_WINDOWS),
        grid=(b, t // tm),
        in_specs=[
            pl.BlockSpec((None, tm, width), lambda bi, i: (bi, i, 0)),
            pl.BlockSpec((None, tm, p), lambda bi, i: (bi, i, u_blk)),
            pl.BlockSpec((None, tm, d), lambda bi, i: (bi, i, ga_blk)),
            pl.BlockSpec((None, tm, d), lambda bi, i: (bi, i, ga_blk + 1)),
            pl.BlockSpec((None, tm, d), lambda bi, i: (bi, i, 0)),
            pl.BlockSpec((None, 6, 1, d), lambda bi, i: (bi, 0, 0, 0)),
            const(pool_w.shape), const(pool_scale.shape), const(proj_a.shape), const(proj_b.shape), const(w_out.shape),
        ],
        out_specs=pl.BlockSpec((None, tm, d), lambda bi, i: (bi, i, 0)),
        out_shape=jax.ShapeDtypeStruct((b, t, d), F32),
        scratch_shapes=[pltpu.VMEM((pool_w.shape[0], max(POOL_WINDOWS) + tm, pool_w.shape[1]), F32)],
        compiler_params=_params(("arbitrary", "arbitrary")),
        name="mixout",
    )(h_a, zr, zr, zr, x, mod, pool_w, pool_scale, proj_a, proj_b, w_out)


def _ffn_kernel(x_ref, g_ref, mod_ref, wg_ref, wu_ref, wd_ref, fin_ref, out_ref, *, final_norm, ff_chunk):
    x = x_ref[...]
    h = _norm_mod(x, g_ref[...], mod_ref[3], mod_ref[4]).astype(BF16)
    acc = None
    for c0 in range(0, wg_ref.shape[1], ff_chunk):
        gate = _dot(h, wg_ref[:, c0:c0 + ff_chunk])
        act = (gate * _sigmoid(gate) * _dot(h, wu_ref[:, c0:c0 + ff_chunk])).astype(BF16)
        part = _dot(act, wd_ref[c0:c0 + ff_chunk, :])
        acc = part if acc is None else acc + part
    y = x + mod_ref[5] * acc
    out_ref[...] = _rms(y, fin_ref[...]) if final_norm else y


def _ffn(x, g, mod, w_gate, w_up, w_down, fin, final_norm, tm=512, ff_chunk=1408):
    b, t, d = x.shape
    ff = w_gate.shape[1]
    tm = min(tm, t)
    ff_chunk = min(ff_chunk, ff)
    assert t % tm == 0 and ff % ff_chunk == 0
    const = lambda shape: pl.BlockSpec(shape, lambda bi, i: (0,) * len(shape))
    return pl.pallas_call(
        functools.partial(_ffn_kernel, final_norm=final_norm, ff_chunk=ff_chunk),
        grid=(b, t // tm),
        in_specs=[
            pl.BlockSpec((None, tm, d), lambda bi, i: (bi, i, 0)),
            const((1, d)),
            pl.BlockSpec((None, 6, 1, d), lambda bi, i: (bi, 0, 0, 0)),
            const((d, ff)), const((d, ff)), const((ff, d)), const((1, d)),
        ],
        out_specs=pl.BlockSpec((None, tm, d), lambda bi, i: (bi, i, 0)),
        out_shape=jax.ShapeDtypeStruct((b, t, d), F32),
        compiler_params=_params(("arbitrary", "arbitrary")),
        name="ffn_dense",
    )(x, g, mod, w_gate, w_up, w_down, fin)


def _to_slabs(ref, x):
    rows, d = x.shape
    per = d // LANES
    for k in range(per):
        ref[pl.ds(k, rows, stride=per), :] = x[:, k * LANES:(k + 1) * LANES]


def _from_slabs(ref, rows):
    per = ref.shape[0] // rows
    return jnp.concatenate([ref[pl.ds(k, rows, stride=per), :] for k in range(per)], axis=-1)


def _router_kernel(x_ref, g_ref, mod_ref, rw_ref, rb_ref, h_ref, info_ref, cnt_ref, carry_ref, *, n_experts, cap):
    first = (pl.program_id(0) == 0) & (pl.program_id(1) == 0)

    @pl.when(first)
    def _():
        carry_ref[...] = jnp.zeros_like(carry_ref)

    h = _norm_mod(x_ref[...], g_ref[...], mod_ref[3], mod_ref[4])
    _to_slabs(h_ref, h)
    tm = h.shape[0]
    lane = lax.broadcasted_iota(jnp.int32, (tm, LANES), 1)
    logits = jnp.where(lane < n_experts, _dot_f32(h, rw_ref[...]) + rb_ref[...], -jnp.inf)
    v1 = jnp.max(logits, axis=-1, keepdims=True)
    i1 = jnp.min(jnp.where(logits == v1, lane, LANES), axis=-1, keepdims=True)
    rest = jnp.where(lane == i1, -jnp.inf, logits)
    v2 = jnp.max(rest, axis=-1, keepdims=True)
    i2 = jnp.min(jnp.where(rest == v2, lane, LANES), axis=-1, keepdims=True)
    e2 = jnp.exp(v2 - v1)
    w1 = 1.0 / (1.0 + e2)
    w2 = e2 / (1.0 + e2)
    sel1 = lane == i1
    sel2 = lane == i2
    picked = jnp.where(sel1 | sel2, 1.0, 0.0)
    row = lax.broadcasted_iota(jnp.int32, (tm, tm), 0)
    col = lax.broadcasted_iota(jnp.int32, (tm, tm), 1)
    before = jnp.where(col < row, 1.0, 0.0).astype(BF16)
    ex = _dot(before, picked.astype(BF16)) + carry_ref[...]
    pos1 = i1.astype(F32) * cap + jnp.sum(jnp.where(sel1, ex, 0.0), axis=-1, keepdims=True)
    pos2 = i2.astype(F32) * cap + jnp.sum(jnp.where(sel2, ex, 0.0), axis=-1, keepdims=True)
    carry = carry_ref[...] + jnp.sum(picked, axis=0, keepdims=True)
    carry_ref[...] = carry
    cnt_ref[...] = carry
    info_ref[...] = (jnp.where(lane == 0, pos1, 0.0) + jnp.where(lane == 1, pos2, 0.0)
                     + jnp.where(lane == 2, w1, 0.0) + jnp.where(lane == 3, w2, 0.0))


def _router(x, g, mod, router_w, router_b, tm=512):
    b, t, d = x.shape
    n_experts = router_w.shape[1]
    tm = min(tm, t)
    nt = t // tm
    per = d // LANES
    assert t % tm == 0 and d % LANES == 0 and n_experts <= LANES and n_experts * b * t < 2 ** 24
    rw = jnp.zeros((d, LANES), F32).at[:, :n_experts].set(router_w)
    rb = jnp.zeros((1, LANES), F32).at[:, :n_experts].set(router_b[None, :])
    return pl.pallas_call(
        functools.partial(_router_kernel, n_experts=n_experts, cap=b * t),
        grid=(b, t // tm),
        in_specs=[
            pl.BlockSpec((None, tm, d), lambda bi, i: (bi, i, 0)),
            pl.BlockSpec((1, d), lambda bi, i: (0, 0)),
            pl.BlockSpec((None, 6, 1, d), lambda bi, i: (bi, 0, 0, 0)),
            pl.BlockSpec((d, LANES), lambda bi, i: (0, 0)),
            pl.BlockSpec((1, LANES), lambda bi, i: (0, 0)),
        ],
        out_specs=[
            pl.BlockSpec((tm * per, LANES), lambda bi, i: (bi * nt + i, 0)),
            pl.BlockSpec((None, tm, LANES), lambda bi, i: (bi, i, 0)),
            pl.BlockSpec((1, LANES), lambda bi, i: (0, 0)),
        ],
        out_shape=[jax.ShapeDtypeStruct((b * t * per, LANES), F32), jax.ShapeDtypeStruct((b, t, LANES), F32),
                   jax.ShapeDtypeStruct((1, LANES), F32)],
        scratch_shapes=[pltpu.VMEM((1, LANES), F32)],
        compiler_params=_params(("arbitrary", "arbitrary")),
        name="router",
    )(x, g, mod, rw, rb)


def _token_copy(src, src_tok, dst, dst_tok, sem, per):
    return pltpu.make_async_copy(src.at[pl.ds(pl.multiple_of(src_tok * per, per), per)],
                                 dst.at[pl.ds(pl.multiple_of(dst_tok * per, per), per)], sem)


def _scatter_kernel(pos_ref, h_ref, hs_hbm, sem, *, per):
    rows = h_ref.shape[0] // per
    base = pl.program_id(0) * rows

    def copies(r):
        t = base + r
        return (_token_copy(h_ref, r, hs_hbm, pos_ref[TOP_K * t], sem, per),
                _token_copy(h_ref, r, hs_hbm, pos_ref[TOP_K * t + 1], sem, per))

    def start(r, carry):
        for slot, cp in enumerate(copies(r)):
            cp.start(priority=slot)
        return carry

    def wait(r, carry):
        for cp in copies(r):
            cp.wait()
        return carry

    lax.fori_loop(0, rows, start, 0, unroll=8)
    lax.fori_loop(0, rows, wait, 0, unroll=8)


def _scatter_rows(pos, h, n, n_slots_out, rows=512):
    per = h.shape[0] // n
    rows = min(rows, n)
    assert n % rows == 0
    return pl.pallas_call(
        functools.partial(_scatter_kernel, per=per),
        grid_spec=pltpu.PrefetchScalarGridSpec(
            num_scalar_prefetch=1,
            grid=(n // rows,),
            in_specs=[pl.BlockSpec((rows * per, LANES), lambda i, pos: (i, 0))],
            out_specs=pl.BlockSpec(memory_space=pl.ANY),
            scratch_shapes=[pltpu.SemaphoreType.DMA],
        ),
        out_shape=jax.ShapeDtypeStruct((n_slots_out * per, LANES), h.dtype),
        compiler_params=_params(("arbitrary",)),
        name="moe_scatter",
    )(pos, h)


def _moe_ffn_kernel(te_ref, tb_ref, tv_ref, hs_ref, wg_ref, wu_ref, wd_ref, y_ref, xb_ref, acc_ref):
    i = pl.program_id(0)
    j = pl.program_id(1)
    valid = tv_ref[i]

    @pl.when(valid > 0)
    def _():
        @pl.when(j == 0)
        def _():
            row = lax.broadcasted_iota(jnp.int32, (xb_ref.shape[0], 1), 0)
            xb_ref[...] = jnp.where(row < valid, _from_slabs(hs_ref, xb_ref.shape[0]), 0.0).astype(BF16)
            acc_ref[...] = jnp.zeros_like(acc_ref)

        tm = xb_ref.shape[0]
        step = tm // MOE_TILE_PARTS
        for part in range(1, MOE_TILE_PARTS + 1):
            rows = part * step

            @pl.when((valid > rows - step) & (valid <= rows))
            def _(rows=rows):
                h = xb_ref[0:rows, :]
                gate = _dot(h, wg_ref[...].astype(BF16))
                act = (gate * _sigmoid(gate) * _dot(h, wu_ref[...].astype(BF16))).astype(BF16)
                acc_ref[0:rows, :] += _dot(act, wd_ref[...].astype(BF16))

        @pl.when(j == pl.num_programs(1) - 1)
        def _():
            _to_slabs(y_ref, acc_ref[...])


def _moe_ffn(tile_e, tile_blk, tile_valid, hs, w_gate, w_up, w_down, tm, tf=512):
    n_experts, d, ff = w_gate.shape
    per = d // LANES
    tf = min(tf, ff)
    nf = ff // tf
    assert ff % tf == 0
    n_tiles = tile_e.shape[0]
    jj = lambda i, j, tv: jnp.where(tv[i] > 0, j, nf - 1)
    return pl.pallas_call(
        _moe_ffn_kernel,
        grid_spec=pltpu.PrefetchScalarGridSpec(
            num_scalar_prefetch=3,
            grid=(n_tiles, nf),
            in_specs=[
                pl.BlockSpec((tm * per, LANES), lambda i, j, te, tb, tv: (tb[i], 0)),
                pl.BlockSpec((None, d, tf), lambda i, j, te, tb, tv: (te[i], 0, jj(i, j, tv))),
                pl.BlockSpec((None, d, tf), lambda i, j, te, tb, tv: (te[i], 0, jj(i, j, tv))),
                pl.BlockSpec((None, tf, d), lambda i, j, te, tb, tv: (te[i], jj(i, j, tv), 0)),
            ],
            out_specs=pl.BlockSpec((tm * per, LANES), lambda i, j, te, tb, tv: (tb[i], 0)),
            scratch_shapes=[pltpu.VMEM((tm, d), BF16), pltpu.VMEM((tm, d), F32)],
        ),
        out_shape=jax.ShapeDtypeStruct(hs.shape, F32),
        compiler_params=_params(("arbitrary", "arbitrary")),
        name="moe_ffn",
    )(tile_e, tile_blk, tile_valid, hs, w_gate, w_up, w_down)


def _combine_kernel(pos_ref, x_ref, info_ref, mod_ref, fin_ref, y_hbm, out_ref, ybuf, sem, *, final_norm):
    rows = x_ref.shape[0]
    base = pl.program_id(0) * rows

    per = ybuf.shape[1] // rows

    def copies(r):
        t = base + r
        return (_token_copy(y_hbm, pos_ref[TOP_K * t], ybuf.at[0], r, sem, per),
                _token_copy(y_hbm, pos_ref[TOP_K * t + 1], ybuf.at[1], r, sem, per))

    def start(r, carry):
        for slot, cp in enumerate(copies(r)):
            cp.start(priority=slot)
        return carry

    def wait(r, carry):
        for cp in copies(r):
            cp.wait()
        return carry

    lax.fori_loop(0, rows, start, 0, unroll=8)
    lax.fori_loop(0, rows, wait, 0, unroll=8)
    info = info_ref[...]
    f = info[:, 2:3] * _from_slabs(ybuf.at[0], rows) + info[:, 3:4] * _from_slabs(ybuf.at[1], rows)
    y = x_ref[...] + mod_ref[5] * f
    out_ref[...] = _rms(y, fin_ref[...]) if final_norm else y


def _combine(pos, x, info, mod, fin, y, final_norm, rows=256):
    b, t, d = x.shape
    n = b * t
    rows = min(rows, t)
    assert t % rows == 0
    per_b = t // rows
    return pl.pallas_call(
        functools.partial(_combine_kernel, final_norm=final_norm),
        grid_spec=pltpu.PrefetchScalarGridSpec(
            num_scalar_prefetch=1,
            grid=(n // rows,),
            in_specs=[
                pl.BlockSpec((rows, d), lambda i, pos: (i, 0)),
                pl.BlockSpec((rows, LANES), lambda i, pos: (i, 0)),
                pl.BlockSpec((None, 6, 1, d), lambda i, pos: (i // per_b, 0, 0, 0)),
                pl.BlockSpec((1, d), lambda i, pos: (0, 0)),
                pl.BlockSpec(memory_space=pl.ANY),
            ],
            out_specs=pl.BlockSpec((rows, d), lambda i, pos: (i, 0)),
            scratch_shapes=[pltpu.VMEM((TOP_K, rows * (d // LANES), LANES), F32), pltpu.SemaphoreType.DMA],
        ),
        out_shape=jax.ShapeDtypeStruct((n, d), F32),
        compiler_params=_params(("arbitrary",)),
        name="moe_combine",
    )(pos, x.reshape(n, d), info.reshape(n, LANES), mod, fin, y).reshape(b, t, d)


def _tile_tables(counts, n_experts, cap, tm, n_tiles):
    counts = counts.astype(jnp.int32)
    tiles_per = (counts + tm - 1) // tm
    ends = jnp.cumsum(tiles_per)
    used = ends[-1]
    i = jnp.minimum(jnp.arange(n_tiles, dtype=jnp.int32), used - 1)
    e = jnp.sum((i[:, None] >= ends[None, :]).astype(jnp.int32), axis=1)
    k = i - (ends - tiles_per)[e]
    valid = jnp.where(jnp.arange(n_tiles) < used, jnp.minimum(counts[e] - k * tm, tm), 0)
    return e, e * (cap // tm) + k, valid.astype(jnp.int32)


def _moe(x, g, mod, router_w, router_b, w_gate, w_up, w_down, fin, final_norm, tm=1024):
    b, t, d = x.shape
    n = b * t
    n_experts = router_w.shape[1]
    tm = min(tm, n)
    assert n % tm == 0 and tm % (MOE_TILE_PARTS * 2 * SUBLANES) == 0
    h, info, counts = _router(x, g, mod, router_w, router_b)
    pos = info[:, :, :TOP_K].astype(jnp.int32).reshape(n * TOP_K)
    hs = _scatter_rows(pos, h, n, n_experts * n)
    n_tiles = TOP_K * n // tm + n_experts
    tile_e, tile_blk, tile_valid = _tile_tables(counts[0, :n_experts], n_experts, n, tm, n_tiles)
    y = _moe_ffn(tile_e, tile_blk, tile_valid, hs, w_gate, w_up, w_down, tm)
    return _combine(pos, x, info, mod, fin, y, final_norm)


def kernel(x, c, norm_mix, norm_ffn, w_ada, b_ada, w_in, conv_w, i_bias, f_bias, head_gain, pool_w, pool_scale,
           proj_a, proj_b, w_out, ffn_w_gate, ffn_w_up, ffn_w_down, router_w, router_b, moe_w_gate, moe_w_up,
           moe_w_down, final_norm):
    depth = w_in.shape[0]
    b, t, d = x.shape
    heads = i_bias.shape[1]
    width = head_gain.shape[1]
    p = pool_scale.shape[1]
    ng = 2 * heads
    qkvo = 4 * width

    mod_all = _adaln(c, w_ada, b_ada).reshape(depth, b, 6, 1, d)
    fin = final_norm.reshape(1, d)

    w_qkvo = _cast_leading_cols(w_in, qkvo)

    for l in range(depth):
        mod = mod_all[l]
        w_rest = jnp.concatenate([w_in[l, :, qkvo + ng + p:], w_in[l, :, qkvo + ng:qkvo + ng + p]], axis=1).astype(BF16)
        w_gates = w_in[l, :, qkvo:qkvo + ng]
        w_if = jnp.zeros((d, LANES), BF16).at[:, :ng].set(w_gates.astype(BF16))
        w_ift = w_gates.T.astype(BF16)
        bias = jnp.concatenate([i_bias[l], f_bias[l]])
        bias_col = jnp.zeros((1, LANES), F32).at[0, :ng].set(bias)
        bias_row = bias.reshape(ng, 1)

        h_a, zr = _mixin(x, norm_mix[l].reshape(1, d), mod, w_qkvo, l, w_rest, w_if, w_ift, conv_w[l], bias_col,
                         bias_row, head_gain[l].reshape(1, width), width, heads)
        x = _mixout(h_a.reshape(b, t, width), zr.reshape(b, t, -1), x, mod, pool_w[l].astype(BF16),
                    pool_scale[l].reshape(1, p), proj_a[l].astype(BF16), proj_b[l].astype(BF16), w_out[l].astype(BF16))

        last = l == depth - 1
        j = l // 2
        g_ffn = norm_ffn[l].reshape(1, d)
        if l % 2 == 0:
            x = _ffn(x, g_ffn, mod, ffn_w_gate[j].astype(BF16), ffn_w_up[j].astype(BF16),
                     ffn_w_down[j].astype(BF16), fin, last)
        else:
            x = _moe(x, g_ffn, mod, router_w[j], router_b[j], moe_w_gate[j], moe_w_up[j], moe_w_down[j], fin, last)
    return x
```

```python
import functools

import jax
import jax.numpy as jnp
from jax import lax
from jax.experimental import pallas as pl
from jax.experimental.pallas import tpu as pltpu

F32 = jnp.float32
BF16 = jnp.bfloat16

EPS = 1e-6
MLSTM_CHUNK = 256
POOL_WINDOWS = (2, 4, 8, 16)
TOP_K = 2
MOE_TILE_PARTS = 4
LANES = 128
SUBLANES = 8
VMEM_LIMIT = 56 * 1024 * 1024

_NT = (((1,), (1,)), ((), ()))
_TN = (((0,), (0,)), ((), ()))


def _params(sem):
    return pltpu.CompilerParams(dimension_semantics=sem, vmem_limit_bytes=VMEM_LIMIT)


def _sigmoid(x):
    return 1.0 / (1.0 + jnp.exp(-x))


def _log_sigmoid(x):
    return jnp.minimum(x, 0.0) - jnp.log(1.0 + jnp.exp(-jnp.abs(x)))


def _rms(x, g):
    return x * lax.rsqrt(jnp.mean(x * x, axis=-1, keepdims=True) + EPS) * g


def _norm_mod(x, g, shift, scale):
    return _rms(x, g) * (1.0 + scale) + shift


def _dot(a, b):
    return jnp.dot(a, b, preferred_element_type=F32)


def _dot_f32(a, b, dims=None):
    dims = dims or (((1,), (0,)), ((), ()))
    return lax.dot_general(a, b, dims, precision=lax.Precision.HIGHEST, preferred_element_type=F32)


def _adaln_kernel(c_ref, w_ref, b_ref, o_ref):
    c = c_ref[...]
    o_ref[...] = _dot_f32(c * _sigmoid(c), w_ref[...]) + b_ref[...]


def _adaln(c, w_ada, b_ada, tn=1536):
    depth, d, n = w_ada.shape
    b = c.shape[0]
    assert n % tn == 0
    return pl.pallas_call(
        _adaln_kernel,
        grid=(depth, n // tn),
        in_specs=[
            pl.BlockSpec((b, d), lambda l, j: (0, 0)),
            pl.BlockSpec((None, d, tn), lambda l, j: (l, 0, j)),
            pl.BlockSpec((None, 1, tn), lambda l, j: (l, 0, j)),
        ],
        out_specs=pl.BlockSpec((None, b, tn), lambda l, j: (l, 0, j)),
        out_shape=jax.ShapeDtypeStruct((depth, b, n), F32),
        compiler_params=_params(("arbitrary", "arbitrary")),
        name="adaln",
    )(c, w_ada, b_ada.reshape(depth, 1, n))


def _mlstm_chunk(zq, gc, gr, convw_ref, gain_ref, out_ref, r0, cbuf, c_st, n_st, m_st, *, heads, fresh):
    L = MLSTM_CHUNK
    W = out_ref.shape[1]
    dh = W // heads
    taps = convw_ref.shape[0]
    halo = SUBLANES
    rows = slice(r0, r0 + L)

    def carried(x):
        return x if fresh is None else jnp.where(fresh, 0.0, x)

    def conv_silu(col0, slab0, scale):
        outs = []
        for cb in range(W // LANES):
            slab = slab0 + cb
            cs = slice(slab * LANES, (slab + 1) * LANES)
            if fresh is not None:
                cbuf[slab, 0:halo, :] = carried(cbuf[slab, 0:halo, :])
            cbuf[slab, halo:halo + L, :] = zq[rows, col0 + cb * LANES:col0 + (cb + 1) * LANES].astype(F32)
            acc = convw_ref[taps - 1:taps, cs] * cbuf[slab, halo:halo + L, :]
            for j in range(taps - 1):
                off = halo - (taps - 1) + j
                acc = acc + convw_ref[j:j + 1, cs] * cbuf[slab, off:off + L, :]
            cbuf[slab, 0:halo, :] = cbuf[slab, L:L + halo, :]
            outs.append(acc * _sigmoid(acc) * scale if scale != 1.0 else acc * _sigmoid(acc))
        return outs

    q_slabs = conv_silu(0, 0, dh ** -0.5)
    k_slabs = conv_silu(W, W // LANES, 1.0)
    per_head = dh // LANES

    row = lax.broadcasted_iota(jnp.int32, (L, L), 0)
    col = lax.broadcasted_iota(jnp.int32, (L, L), 1)
    causal = row >= col
    tri_low = causal.astype(F32)
    tri_up = (row <= col).astype(F32)
    b_cols = _dot_f32(tri_low, _log_sigmoid(gc))
    b_rows = _dot_f32(_log_sigmoid(gr), tri_up)

    for h in range(heads):
        hs = slice(h * dh, (h + 1) * dh)
        q = jnp.concatenate(q_slabs[h * per_head:(h + 1) * per_head], axis=-1)
        k = jnp.concatenate(k_slabs[h * per_head:(h + 1) * per_head], axis=-1)
        vb = zq[rows, 2 * W + h * dh:2 * W + (h + 1) * dh]
        v = vb.astype(F32)
        qb = q.astype(BF16)
        kb = k.astype(BF16)
        li_c = gc[:, h:h + 1]
        b_c = b_cols[:, heads + h:heads + h + 1]
        li_r = gr[h:h + 1, :]
        b_r = b_rows[heads + h:heads + h + 1, :]
        b_tot = b_r[:, L - 1:L]
        c_prev = carried(c_st[h])
        n_prev = carried(n_st[h])
        m_prev = carried(m_st[h][:, 0:1])

        d = jnp.where(causal, b_c - b_r + li_r, -jnp.inf)
        inter_log = b_c + m_prev
        m_comb = jnp.maximum(inter_log, jnp.max(d, axis=-1, keepdims=True))
        s = lax.dot_general(qb, kb, _NT, preferred_element_type=F32) * jnp.exp(d - m_comb)
        w_inter = jnp.exp(inter_log - m_comb)
        num = _dot(s.astype(BF16), vb) + w_inter * lax.dot_general(
            qb, c_prev.astype(BF16), _NT, preferred_element_type=F32)
        den = jnp.sum(s, axis=-1, keepdims=True) + w_inter * jnp.sum(q * n_prev, axis=-1, keepdims=True)
        den = jnp.maximum(jnp.abs(den), jnp.exp(-m_comb))
        hh = num / den
        hh = hh * lax.rsqrt(jnp.mean(hh * hh, axis=-1, keepdims=True) + EPS)
        gate = _sigmoid(zq[rows, 3 * W + h * dh:3 * W + (h + 1) * dh].astype(F32))
        out_ref[rows, hs] = (hh * gain_ref[:, hs] * gate).astype(out_ref.dtype)

        a = b_tot - b_c + li_c
        m_loc = jnp.max(a, axis=0, keepdims=True)
        w = jnp.exp(a - m_loc)
        c_loc = lax.dot_general((w * v).astype(BF16), kb, _TN, preferred_element_type=F32)
        n_loc = jnp.sum(w * k, axis=0, keepdims=True)
        m_new = jnp.maximum(b_tot + m_prev, m_loc)
        s_old = jnp.exp(b_tot + m_prev - m_new)
        s_loc = jnp.exp(m_loc - m_new)
        c_st[h] = s_old * c_prev + s_loc * c_loc
        n_st[h] = s_old * n_prev + s_loc * n_loc
        m_st[h] = jnp.broadcast_to(m_new, (1, LANES))


def _mixin_kernel(x_ref, g_ref, mod_ref, wq_ref, wr_ref, wif_ref, wift_ref, convw_ref, bcol_ref, brow_ref, gain_ref,
                  ha_ref, zr_ref, h_ref, zring, gcring, grring, cbuf, c_st, n_st, m_st,
                  *, heads, tiles_per_seq, col_chunk):
    s = pl.program_id(0)
    tm = x_ref.shape[0]
    W = ha_ref.shape[1]
    L = MLSTM_CHUNK
    n_chunks = tm // L
    qkvo = 4 * W
    slot_a = lax.rem(s, 2)
    slot_b = 1 - slot_a

    @pl.when(s == 0)
    def _():
        zring[1] = jnp.zeros(zring.shape[1:], zring.dtype)
        gcring[1] = jnp.zeros(gcring.shape[1:], F32)
        grring[1] = jnp.zeros(grring.shape[1:], F32)
        cbuf[...] = jnp.zeros_like(cbuf)
        c_st[...] = jnp.zeros_like(c_st)
        n_st[...] = jnp.zeros_like(n_st)
        m_st[...] = jnp.zeros_like(m_st)

    h = _norm_mod(x_ref[...], g_ref[...], mod_ref[0], mod_ref[1]).astype(BF16)
    h_ref[...] = h
    gcring[slot_a] = _dot(h, wif_ref[...])
    grring[slot_a] = lax.dot_general(wift_ref[...], h, _NT, preferred_element_type=F32)

    fresh = lax.rem(s + tiles_per_seq - 1, tiles_per_seq) == 0
    zq = zring.at[slot_b]
    col_starts = list(range(0, qkvo + wr_ref.shape[1], col_chunk))
    share = -(-len(col_starts) // n_chunks)
    for c in range(n_chunks):
        for c0 in col_starts[c * share:(c + 1) * share]:
            if c0 < qkvo:
                zring[slot_a, :, c0:c0 + col_chunk] = _dot(h_ref[...], wq_ref[:, c0:c0 + col_chunk]).astype(BF16)
            else:
                cr = slice(c0 - qkvo, c0 - qkvo + col_chunk)
                zr_ref[:, cr] = _dot(h_ref[...], wr_ref[:, cr]).astype(BF16)
        r0 = c * L
        gc = gcring[slot_b, r0:r0 + L, :] + bcol_ref[...]
        gr = grring[slot_b, :, r0:r0 + L] + brow_ref[...]
        _mlstm_chunk(zq, gc, gr, convw_ref, gain_ref, ha_ref, r0, cbuf, c_st, n_st, m_st,
                     heads=heads, fresh=fresh if c == 0 else None)


def _mixin(x, g, mod, w_qkvo, layer, w_rest, w_if, w_ift, conv_w, bias_col, bias_row, head_gain, width, heads, tm=512,
           col_chunk=256):
    b, t, d = x.shape
    n = b * t
    ncols = w_qkvo.shape[2] + w_rest.shape[1]
    ng = w_ift.shape[0]
    dh = width // heads
    tm = min(tm, t)
    nt = t // tm
    n_tiles = b * nt
    qkvo = 4 * width
    assert t % tm == 0 and tm % MLSTM_CHUNK == 0 and conv_w.shape[0] - 1 <= SUBLANES and dh % LANES == 0
    assert ncols % col_chunk == 0 and qkvo % col_chunk == 0 and w_qkvo.shape[2] == qkvo
    cur = lambda s: jnp.minimum(s, n_tiles - 1)
    prev = lambda s: jnp.maximum(s - 1, 0)
    const = lambda shape: pl.BlockSpec(shape, lambda s: (0,) * len(shape))
    return pl.pallas_call(
        functools.partial(_mixin_kernel, heads=heads, tiles_per_seq=nt, col_chunk=col_chunk),
        grid=(n_tiles + 1,),
        in_specs=[
            pl.BlockSpec((tm, d), lambda s: (cur(s), 0)),
            const((1, d)),
            pl.BlockSpec((None, 6, 1, d), lambda s: (cur(s) // nt, 0, 0, 0)),
            pl.BlockSpec((None, d, qkvo), lambda s: (layer, 0, 0), pipeline_mode=pl.Buffered(1)),
            pl.BlockSpec((d, ncols - qkvo), lambda s: (0, 0), pipeline_mode=pl.Buffered(1)),
            const((d, LANES)), const((ng, d)), const(conv_w.shape), const((1, LANES)), const((ng, 1)), const((1, width)),
        ],
        out_specs=[
            pl.BlockSpec((tm, width), lambda s: (prev(s), 0)),
            pl.BlockSpec((tm, ncols - qkvo), lambda s: (cur(s), 0)),
        ],
        out_shape=[
            jax.ShapeDtypeStruct((n, width), BF16),
            jax.ShapeDtypeStruct((n, ncols - qkvo), BF16),
        ],
        scratch_shapes=[
            pltpu.VMEM((tm, d), BF16),
            pltpu.VMEM((2, tm, qkvo), BF16),
            pltpu.VMEM((2, tm, LANES), F32),
            pltpu.VMEM((2, ng, tm), F32),
            pltpu.VMEM((2 * width // LANES, SUBLANES + MLSTM_CHUNK, LANES), F32),
            pltpu.VMEM((heads, dh, dh), F32),
            pltpu.VMEM((heads, 1, dh), F32),
            pltpu.VMEM((heads, 1, LANES), F32),
        ],
        compiler_params=_params(("arbitrary",)),
        name="mixin",
    )(x.reshape(n, d), g, mod, w_qkvo, w_rest, w_if, w_ift, conv_w, bias_col, bias_row, head_gain)


def _mixout_kernel(ha_ref, u_ref, ga_ref, gb_ref, x_ref, mod_ref, poolw_ref, pscale_ref, pa_ref, pb_ref, wo_ref,
                   out_ref, ubuf, *, windows):
    tm = x_ref.shape[0]
    gd = poolw_ref.shape[1]
    halo = max(windows)
    i = pl.program_id(1)

    @pl.when(i == 0)
    def _():
        ubuf[:, 0:halo, :] = jnp.zeros((ubuf.shape[0], halo, gd), F32)

    tpos = i * tm + lax.broadcasted_iota(jnp.int32, (tm, 1), 0)
    parts = []
    for g, win in enumerate(windows):
        cur = u_ref[:, g * gd:(g + 1) * gd].astype(F32)
        ubuf[g, halo:halo + tm, :] = cur
        wsum = cur
        for j in range(1, win):
            wsum = wsum + ubuf[g, halo - j:halo - j + tm, :]
        ubuf[g, 0:halo, :] = ubuf[g, tm:tm + halo, :]
        count = jnp.minimum(tpos + 1, win).astype(F32)
        pooled = wsum / count - cur
        parts.append(_dot(pooled.astype(BF16), poolw_ref[g]))
    hb = (jnp.concatenate(parts, axis=-1) * pscale_ref[...]).astype(BF16)

    pa = _dot(ha_ref[...], pa_ref[...])
    pb = _dot(hb, pb_ref[...])
    merged = _sigmoid(ga_ref[...].astype(F32)) * pa + _sigmoid(gb_ref[...].astype(F32)) * pb
    y = _dot(merged.astype(BF16), wo_ref[...])
    out_ref[...] = x_ref[...] + mod_ref[2] * y


def _mixout(h_a, zr, x, mod, pool_w, pool_scale, proj_a, proj_b, w_out, tm=512):
    b, t, d = x.shape
    p = pool_scale.shape[1]
    width = h_a.shape[2]
    tm = min(tm, t)
    ga_blk = 0
    u_blk = 2 * d // p
    assert t % tm == 0 and (2 * d) % p == 0
    const = lambda shape: pl.BlockSpec(shape, lambda bi, i: (0,) * len(shape))
    return pl.pallas_call(
        functools.partial(_mixout_kernel, windows=POOL_WINDOWS),
        grid=(b, t // tm),
        in_specs=[
            pl.BlockSpec((None, tm, width), lambda bi, i: (bi, i, 0)),
            pl.BlockSpec((None, tm, p), lambda bi, i: (bi, i, u_blk)),
            pl.BlockSpec((None, tm, d), lambda bi, i: (bi, i, ga_blk)),
            pl.BlockSpec((None, tm, d), lambda bi, i: (bi, i, ga_blk + 1)),
            pl.BlockSpec((None, tm, d), lambda bi, i: (bi, i, 0)),
            pl.BlockSpec((None, 6, 1, d), lambda bi, i: (bi, 0, 0, 0)),
            const(pool_w.shape), const(pool_scale.shape), const(proj_a.shape), const(proj_b.shape), const(w_out.shape),
        ],
        out_specs=pl.BlockSpec((None, tm, d), lambda bi, i: (bi, i, 0)),
        out_shape=jax.ShapeDtypeStruct((b, t, d), F32),
        scratch_shapes=[pltpu.VMEM((pool_w.shape[0], max(POOL_WINDOWS) + tm, pool_w.shape[1]), F32)],
        compiler_params=_params(("arbitrary", "arbitrary")),
        name="mixout",
    )(h_a, zr, zr, zr, x, mod, pool_w, pool_scale, proj_a, proj_b, w_out)


def _ffn_kernel(x_ref, g_ref, mod_ref, wg_ref, wu_ref, wd_ref, fin_ref, out_ref, *, final_norm, ff_chunk):
    x = x_ref[...]
    h = _norm_mod(x, g_ref[...], mod_ref[3], mod_ref[4]).astype(BF16)
    acc = None
    for c0 in range(0, wg_ref.shape[1], ff_chunk):
        gate = _dot(h, wg_ref[:, c0:c0 + ff_chunk])
        act = (gate * _sigmoid(gate) * _dot(h, wu_ref[:, c0:c0 + ff_chunk])).astype(BF16)
        part = _dot(act, wd_ref[c0:c0 + ff_chunk, :])
        acc = part if acc is None else acc + part
    y = x + mod_ref[5] * acc
    out_ref[...] = _rms(y, fin_ref[...]) if final_norm else y


def _ffn(x, g, mod, w_gate, w_up, w_down, fin, final_norm, tm=512, ff_chunk=1408):
    b, t, d = x.shape
    ff = w_gate.shape[1]
    tm = min(tm, t)
    ff_chunk = min(ff_chunk, ff)
    assert t % tm == 0 and ff % ff_chunk == 0
    const = lambda shape: pl.BlockSpec(shape, lambda bi, i: (0,) * len(shape))
    return pl.pallas_call(
        functools.partial(_ffn_kernel, final_norm=final_norm, ff_chunk=ff_chunk),
        grid=(b, t // tm),
        in_specs=[
            pl.BlockSpec((None, tm, d), lambda bi, i: (bi, i, 0)),
            const((1, d)),
            pl.BlockSpec((None, 6, 1, d), lambda bi, i: (bi, 0, 0, 0)),
            const((d, ff)), const((d, ff)), const((ff, d)), const((1, d)),
        ],
        out_specs=pl.BlockSpec((None, tm, d), lambda bi, i: (bi, i, 0)),
        out_shape=jax.ShapeDtypeStruct((b, t, d), F32),
        compiler_params=_params(("arbitrary", "arbitrary")),
        name="ffn_dense",
    )(x, g, mod, w_gate, w_up, w_down, fin)


def _to_slabs(ref, x):
    rows, d = x.shape
    per = d // LANES
    for k in range(per):
        ref[pl.ds(k, rows, stride=per), :] = x[:, k * LANES:(k + 1) * LANES]


def _from_slabs(ref, rows):
    per = ref.shape[0] // rows
    return jnp.concatenate([ref[pl.ds(k, rows, stride=per), :] for k in range(per)], axis=-1)


def _router_kernel(x_ref, g_ref, mod_ref, rw_ref, rb_ref, h_ref, info_ref, cnt_ref, carry_ref, *, n_experts, cap):
    first = (pl.program_id(0) == 0) & (pl.program_id(1) == 0)

    @pl.when(first)
    def _():
        carry_ref[...] = jnp.zeros_like(carry_ref)

    h = _norm_mod(x_ref[...], g_ref[...], mod_ref[3], mod_ref[4])
    _to_slabs(h_ref, h)
    tm = h.shape[0]
    lane = lax.broadcasted_iota(jnp.int32, (tm, LANES), 1)
    logits = jnp.where(lane < n_experts, _dot_f32(h, rw_ref[...]) + rb_ref[...], -jnp.inf)
    v1 = jnp.max(logits, axis=-1, keepdims=True)
    i1 = jnp.min(jnp.where(logits == v1, lane, LANES), axis=-1, keepdims=True)
    rest = jnp.where(lane == i1, -jnp.inf, logits)
    v2 = jnp.max(rest, axis=-1, keepdims=True)
    i2 = jnp.min(jnp.where(rest == v2, lane, LANES), axis=-1, keepdims=True)
    e2 = jnp.exp(v2 - v1)
    w1 = 1.0 / (1.0 + e2)
    w2 = e2 / (1.0 + e2)
    sel1 = lane == i1
    sel2 = lane == i2
    picked = jnp.where(sel1 | sel2, 1.0, 0.0)
    row = lax.broadcasted_iota(jnp.int32, (tm, tm), 0)
    col = lax.broadcasted_iota(jnp.int32, (tm, tm), 1)
    before = jnp.where(col < row, 1.0, 0.0).astype(BF16)
    ex = _dot(before, picked.astype(BF16)) + carry_ref[...]
    pos1 = i1.astype(F32) * cap + jnp.sum(jnp.where(sel1, ex, 0.0), axis=-1, keepdims=True)
    pos2 = i2.astype(F32) * cap + jnp.sum(jnp.where(sel2, ex, 0.0), axis=-1, keepdims=True)
    carry = carry_ref[...] + jnp.sum(picked, axis=0, keepdims=True)
    carry_ref[...] = carry
    cnt_ref[...] = carry
    info_ref[...] = (jnp.where(lane == 0, pos1, 0.0) + jnp.where(lane == 1, pos2, 0.0)
                     + jnp.where(lane == 2, w1, 0.0) + jnp.where(lane == 3, w2, 0.0))


def _router(x, g, mod, router_w, router_b, tm=512):
    b, t, d = x.shape
    n_experts = router_w.shape[1]
    tm = min(tm, t)
    nt = t // tm
    per = d // LANES
    assert t % tm == 0 and d % LANES == 0 and n_experts <= LANES and n_experts * b * t < 2 ** 24
    rw = jnp.zeros((d, LANES), F32).at[:, :n_experts].set(router_w)
    rb = jnp.zeros((1, LANES), F32).at[:, :n_experts].set(router_b[None, :])
    return pl.pallas_call(
        functools.partial(_router_kernel, n_experts=n_experts, cap=b * t),
        grid=(b, t // tm),
        in_specs=[
            pl.BlockSpec((None, tm, d), lambda bi, i: (bi, i, 0)),
            pl.BlockSpec((1, d), lambda bi, i: (0, 0)),
            pl.BlockSpec((None, 6, 1, d), lambda bi, i: (bi, 0, 0, 0)),
            pl.BlockSpec((d, LANES), lambda bi, i: (0, 0)),
            pl.BlockSpec((1, LANES), lambda bi, i: (0, 0)),
        ],
        out_specs=[
            pl.BlockSpec((tm * per, LANES), lambda bi, i: (bi * nt + i, 0)),
            pl.BlockSpec((None, tm, LANES), lambda bi, i: (bi, i, 0)),
            pl.BlockSpec((1, LANES), lambda bi, i: (0, 0)),
        ],
        out_shape=[jax.ShapeDtypeStruct((b * t * per, LANES), F32), jax.ShapeDtypeStruct((b, t, LANES), F32),
                   jax.ShapeDtypeStruct((1, LANES), F32)],
        scratch_shapes=[pltpu.VMEM((1, LANES), F32)],
        compiler_params=_params(("arbitrary", "arbitrary")),
        name="router",
    )(x, g, mod, rw, rb)


def _token_copy(src, src_tok, dst, dst_tok, sem, per):
    return pltpu.make_async_copy(src.at[pl.ds(pl.multiple_of(src_tok * per, per), per)],
                                 dst.at[pl.ds(pl.multiple_of(dst_tok * per, per), per)], sem)


def _scatter_kernel(pos_ref, h_ref, hs_hbm, sem, *, per):
    rows = h_ref.shape[0] // per
    base = pl.program_id(0) * rows

    def copies(r):
        t = base + r
        return (_token_copy(h_ref, r, hs_hbm, pos_ref[TOP_K * t], sem, per),
                _token_copy(h_ref, r, hs_hbm, pos_ref[TOP_K * t + 1], sem, per))

    def start(r, carry):
        for slot, cp in enumerate(copies(r)):
            cp.start(priority=slot)
        return carry

    def wait(r, carry):
        for cp in copies(r):
            cp.wait()
        return carry

    lax.fori_loop(0, rows, start, 0, unroll=8)
    lax.fori_loop(0, rows, wait, 0, unroll=8)


def _scatter_rows(pos, h, n, n_slots_out, rows=512):
    per = h.shape[0] // n
    rows = min(rows, n)
    assert n % rows == 0
    return pl.pallas_call(
        functools.partial(_scatter_kernel, per=per),
        grid_spec=pltpu.PrefetchScalarGridSpec(
            num_scalar_prefetch=1,
            grid=(n // rows,),
            in_specs=[pl.BlockSpec((rows * per, LANES), lambda i, pos: (i, 0))],
            out_specs=pl.BlockSpec(memory_space=pl.ANY),
            scratch_shapes=[pltpu.SemaphoreType.DMA],
        ),
        out_shape=jax.ShapeDtypeStruct((n_slots_out * per, LANES), h.dtype),
        compiler_params=_params(("arbitrary",)),
        name="moe_scatter",
    )(pos, h)


def _moe_ffn_kernel(te_ref, tb_ref, tv_ref, hs_ref, wg_ref, wu_ref, wd_ref, y_ref, xb_ref, acc_ref):
    i = pl.program_id(0)
    j = pl.program_id(1)
    valid = tv_ref[i]

    @pl.when(valid > 0)
    def _():
        @pl.when(j == 0)
        def _():
            row = lax.broadcasted_iota(jnp.int32, (xb_ref.shape[0], 1), 0)
            xb_ref[...] = jnp.where(row < valid, _from_slabs(hs_ref, xb_ref.shape[0]), 0.0).astype(BF16)
            acc_ref[...] = jnp.zeros_like(acc_ref)

        tm = xb_ref.shape[0]
        step = tm // MOE_TILE_PARTS
        for part in range(1, MOE_TILE_PARTS + 1):
            rows = part * step

            @pl.when((valid > rows - step) & (valid <= rows))
            def _(rows=rows):
                h = xb_ref[0:rows, :]
                gate = _dot(h, wg_ref[...].astype(BF16))
                act = (gate * _sigmoid(gate) * _dot(h, wu_ref[...].astype(BF16))).astype(BF16)
                acc_ref[0:rows, :] += _dot(act, wd_ref[...].astype(BF16))

        @pl.when(j == pl.num_programs(1) - 1)
        def _():
            _to_slabs(y_ref, acc_ref[...])


def _moe_ffn(tile_e, tile_blk, tile_valid, hs, w_gate, w_up, w_down, tm, tf=512):
    n_experts, d, ff = w_gate.shape
    per = d // LANES
    tf = min(tf, ff)
    nf = ff // tf
    assert ff % tf == 0
    n_tiles = tile_e.shape[0]
    jj = lambda i, j, tv: jnp.where(tv[i] > 0, j, nf - 1)
    return pl.pallas_call(
        _moe_ffn_kernel,
        grid_spec=pltpu.PrefetchScalarGridSpec(
            num_scalar_prefetch=3,
            grid=(n_tiles, nf),
            in_specs=[
                pl.BlockSpec((tm * per, LANES), lambda i, j, te, tb, tv: (tb[i], 0)),
                pl.BlockSpec((None, d, tf), lambda i, j, te, tb, tv: (te[i], 0, jj(i, j, tv))),
                pl.BlockSpec((None, d, tf), lambda i, j, te, tb, tv: (te[i], 0, jj(i, j, tv))),
                pl.BlockSpec((None, tf, d), lambda i, j, te, tb, tv: (te[i], jj(i, j, tv), 0)),
            ],
            out_specs=pl.BlockSpec((tm * per, LANES), lambda i, j, te, tb, tv: (tb[i], 0)),
            scratch_shapes=[pltpu.VMEM((tm, d), BF16), pltpu.VMEM((tm, d), F32)],
        ),
        out_shape=jax.ShapeDtypeStruct(hs.shape, F32),
        compiler_params=_params(("arbitrary", "arbitrary")),
        name="moe_ffn",
    )(tile_e, tile_blk, tile_valid, hs, w_gate, w_up, w_down)


def _combine_kernel(pos_ref, x_ref, info_ref, mod_ref, fin_ref, y_hbm, out_ref, ybuf, sem, *, final_norm):
    rows = x_ref.shape[0]
    base = pl.program_id(0) * rows

    per = ybuf.shape[1] // rows

    def copies(r):
        t = base + r
        return (_token_copy(y_hbm, pos_ref[TOP_K * t], ybuf.at[0], r, sem, per),
                _token_copy(y_hbm, pos_ref[TOP_K * t + 1], ybuf.at[1], r, sem, per))

    def start(r, carry):
        for slot, cp in enumerate(copies(r)):
            cp.start(priority=slot)
        return carry

    def wait(r, carry):
        for cp in copies(r):
            cp.wait()
        return carry

    lax.fori_loop(0, rows, start, 0, unroll=8)
    lax.fori_loop(0, rows, wait, 0, unroll=8)
    info = info_ref[...]
    f = info[:, 2:3] * _from_slabs(ybuf.at[0], rows) + info[:, 3:4] * _from_slabs(ybuf.at[1], rows)
    y = x_ref[...] + mod_ref[5] * f
    out_ref[...] = _rms(y, fin_ref[...]) if final_norm else y


def _combine(pos, x, info, mod, fin, y, final_norm, rows=256):
    b, t, d = x.shape
    n = b * t
    rows = min(rows, t)
    assert t % rows == 0
    per_b = t // rows
    return pl.pallas_call(
        functools.partial(_combine_kernel, final_norm=final_norm),
        grid_spec=pltpu.PrefetchScalarGridSpec(
            num_scalar_prefetch=1,
            grid=(n // rows,),
            in_specs=[
                pl.BlockSpec((rows, d), lambda i, pos: (i, 0)),
                pl.BlockSpec((rows, LANES), lambda i, pos: (i, 0)),
                pl.BlockSpec((None, 6, 1, d), lambda i, pos: (i // per_b, 0, 0, 0)),
                pl.BlockSpec((1, d), lambda i, pos: (0, 0)),
                pl.BlockSpec(memory_space=pl.ANY),
            ],
            out_specs=pl.BlockSpec((rows, d), lambda i, pos: (i, 0)),
            scratch_shapes=[pltpu.VMEM((TOP_K, rows * (d // LANES), LANES), F32), pltpu.SemaphoreType.DMA],
        ),
        out_shape=jax.ShapeDtypeStruct((n, d), F32),
        compiler_params=_params(("arbitrary",)),
        name="moe_combine",
    )(pos, x.reshape(n, d), info.reshape(n, LANES), mod, fin, y).reshape(b, t, d)


def _tile_tables(counts, n_experts, cap, tm, n_tiles):
    counts = counts.astype(jnp.int32)
    tiles_per = (counts + tm - 1) // tm
    ends = jnp.cumsum(tiles_per)
    used = ends[-1]
    i = jnp.minimum(jnp.arange(n_tiles, dtype=jnp.int32), used - 1)
    e = jnp.sum((i[:, None] >= ends[None, :]).astype(jnp.int32), axis=1)
    k = i - (ends - tiles_per)[e]
    valid = jnp.where(jnp.arange(n_tiles) < used, jnp.minimum(counts[e] - k * tm, tm), 0)
    return e, e * (cap // tm) + k, valid.astype(jnp.int32)


def _moe(x, g, mod, router_w, router_b, w_gate, w_up, w_down, fin, final_norm, tm=1024):
    b, t, d = x.shape
    n = b * t
    n_experts = router_w.shape[1]
    tm = min(tm, n)
    assert n % tm == 0 and tm % (MOE_TILE_PARTS * 2 * SUBLANES) == 0
    h, info, counts = _router(x, g, mod, router_w, router_b)
    pos = info[:, :, :TOP_K].astype(jnp.int32).reshape(n * TOP_K)
    hs = _scatter_rows(pos, h, n, n_experts * n)
    n_tiles = TOP_K * n // tm + n_experts
    tile_e, tile_blk, tile_valid = _tile_tables(counts[0, :n_experts], n_experts, n, tm, n_tiles)
    y = _moe_ffn(tile_e, tile_blk, tile_valid, hs, w_gate, w_up, w_down, tm)
    return _combine(pos, x, info, mod, fin, y, final_norm)


def kernel(x, c, norm_mix, norm_ffn, w_ada, b_ada, w_in, conv_w, i_bias, f_bias, head_gain, pool_w, pool_scale,
           proj_a, proj_b, w_out, ffn_w_gate, ffn_w_up, ffn_w_down, router_w, router_b, moe_w_gate, moe_w_up,
           moe_w_down, final_norm):
    depth = w_in.shape[0]
    b, t, d = x.shape
    heads = i_bias.shape[1]
    width = head_gain.shape[1]
    p = pool_scale.shape[1]
    ng = 2 * heads
    qkvo = 4 * width

    mod_all = _adaln(c, w_ada, b_ada).reshape(depth, b, 6, 1, d)
    fin = final_norm.reshape(1, d)

    w_qkvo = w_in[:, :, :qkvo].astype(BF16)

    for l in range(depth):
        mod = mod_all[l]
        w_rest = jnp.concatenate([w_in[l, :, qkvo + ng + p:], w_in[l, :, qkvo + ng:qkvo + ng + p]], axis=1).astype(BF16)
        w_gates = w_in[l, :, qkvo:qkvo + ng]
        w_if = jnp.zeros((d, LANES), BF16).at[:, :ng].set(w_gates.astype(BF16))
        w_ift = w_gates.T.astype(BF16)
        bias = jnp.concatenate([i_bias[l], f_bias[l]])
        bias_col = jnp.zeros((1, LANES), F32).at[0, :ng].set(bias)
        bias_row = bias.reshape(ng, 1)

        h_a, zr = _mixin(x, norm_mix[l].reshape(1, d), mod, w_qkvo, l, w_rest, w_if, w_ift, conv_w[l], bias_col,
                         bias_row, head_gain[l].reshape(1, width), width, heads)
        x = _mixout(h_a.reshape(b, t, width), zr.reshape(b, t, -1), x, mod, pool_w[l].astype(BF16),
                    pool_scale[l].reshape(1, p), proj_a[l].astype(BF16), proj_b[l].astype(BF16), w_out[l].astype(BF16))

        last = l == depth - 1
        j = l // 2
        g_ffn = norm_ffn[l].reshape(1, d)
        if l % 2 == 0:
            x = _ffn(x, g_ffn, mod, ffn_w_gate[j].astype(BF16), ffn_w_up[j].astype(BF16),
                     ffn_w_down[j].astype(BF16), fin, last)
        else:
            x = _moe(x, g_ffn, mod, router_w[j], router_b[j], moe_w_gate[j], moe_w_up[j], moe_w_down[j], fin, last)
    return x
```

```python
import functools

import jax
import jax.numpy as jnp
from jax import lax
from jax.experimental import pallas as pl
from jax.experimental.pallas import tpu as pltpu

F32 = jnp.float32
BF16 = jnp.bfloat16

EPS = 1e-6
MLSTM_CHUNK = 256
POOL_WINDOWS = (2, 4, 8, 16)
TOP_K = 2
MOE_TILE_PARTS = 4
LANES = 128
SUBLANES = 8
VMEM_LIMIT = 56 * 1024 * 1024

_NT = (((1,), (1,)), ((), ()))
_TN = (((0,), (0,)), ((), ()))


def _params(sem):
    return pltpu.CompilerParams(dimension_semantics=sem, vmem_limit_bytes=VMEM_LIMIT)


def _sigmoid(x):
    return 1.0 / (1.0 + jnp.exp(-x))


def _log_sigmoid(x):
    return jnp.minimum(x, 0.0) - jnp.log(1.0 + jnp.exp(-jnp.abs(x)))


def _rms(x, g):
    return x * lax.rsqrt(jnp.mean(x * x, axis=-1, keepdims=True) + EPS) * g


def _norm_mod(x, g, shift, scale):
    return _rms(x, g) * (1.0 + scale) + shift


def _dot(a, b):
    return jnp.dot(a, b, preferred_element_type=F32)


def _dot_f32(a, b, dims=None):
    dims = dims or (((1,), (0,)), ((), ()))
    return lax.dot_general(a, b, dims, precision=lax.Precision.HIGHEST, preferred_element_type=F32)


def _dot_split(a, b):
    a_hi = a.astype(BF16)
    a_lo = (a - a_hi.astype(F32)).astype(BF16)
    b_hi = b.astype(BF16)
    b_lo = (b - b_hi.astype(F32)).astype(BF16)
    return _dot(a_hi, b_hi) + (_dot(a_hi, b_lo) + _dot(a_lo, b_hi))


def _adaln_kernel(c_ref, w_ref, b_ref, o_ref):
    c = c_ref[...]
    o_ref[...] = _dot_f32(c * _sigmoid(c), w_ref[...]) + b_ref[...]


def _adaln(c, w_ada, b_ada, tn=1536):
    depth, d, n = w_ada.shape
    b = c.shape[0]
    assert n % tn == 0
    return pl.pallas_call(
        _adaln_kernel,
        grid=(depth, n // tn),
        in_specs=[
            pl.BlockSpec((b, d), lambda l, j: (0, 0)),
            pl.BlockSpec((None, d, tn), lambda l, j: (l, 0, j)),
            pl.BlockSpec((None, 1, tn), lambda l, j: (l, 0, j)),
        ],
        out_specs=pl.BlockSpec((None, b, tn), lambda l, j: (l, 0, j)),
        out_shape=jax.ShapeDtypeStruct((depth, b, n), F32),
        compiler_params=_params(("arbitrary", "arbitrary")),
        name="adaln",
    )(c, w_ada, b_ada.reshape(depth, 1, n))


def _mlstm_chunk(zq, gc, gr, convw_ref, gain_ref, out_ref, r0, cbuf, c_st, n_st, m_st, *, heads, fresh):
    L = MLSTM_CHUNK
    W = out_ref.shape[1]
    dh = W // heads
    taps = convw_ref.shape[0]
    halo = SUBLANES
    rows = slice(r0, r0 + L)

    def carried(x):
        return x if fresh is None else jnp.where(fresh, 0.0, x)

    def conv_silu(col0, slab0, scale):
        outs = []
        for cb in range(W // LANES):
            slab = slab0 + cb
            cs = slice(slab * LANES, (slab + 1) * LANES)
            if fresh is not None:
                cbuf[slab, 0:halo, :] = carried(cbuf[slab, 0:halo, :])
            cbuf[slab, halo:halo + L, :] = zq[rows, col0 + cb * LANES:col0 + (cb + 1) * LANES].astype(F32)
            acc = convw_ref[taps - 1:taps, cs] * cbuf[slab, halo:halo + L, :]
            for j in range(taps - 1):
                off = halo - (taps - 1) + j
                acc = acc + convw_ref[j:j + 1, cs] * cbuf[slab, off:off + L, :]
            cbuf[slab, 0:halo, :] = cbuf[slab, L:L + halo, :]
            outs.append(acc * _sigmoid(acc) * scale if scale != 1.0 else acc * _sigmoid(acc))
        return outs

    q_slabs = conv_silu(0, 0, dh ** -0.5)
    k_slabs = conv_silu(W, W // LANES, 1.0)
    per_head = dh // LANES

    row = lax.broadcasted_iota(jnp.int32, (L, L), 0)
    col = lax.broadcasted_iota(jnp.int32, (L, L), 1)
    causal = row >= col
    tri_low = causal.astype(F32)
    tri_up = (row <= col).astype(F32)
    b_cols = _dot_f32(tri_low, _log_sigmoid(gc))
    b_rows = _dot_f32(_log_sigmoid(gr), tri_up)

    for h in range(heads):
        hs = slice(h * dh, (h + 1) * dh)
        q = jnp.concatenate(q_slabs[h * per_head:(h + 1) * per_head], axis=-1)
        k = jnp.concatenate(k_slabs[h * per_head:(h + 1) * per_head], axis=-1)
        vb = zq[rows, 2 * W + h * dh:2 * W + (h + 1) * dh]
        v = vb.astype(F32)
        qb = q.astype(BF16)
        kb = k.astype(BF16)
        li_c = gc[:, h:h + 1]
        b_c = b_cols[:, heads + h:heads + h + 1]
        li_r = gr[h:h + 1, :]
        b_r = b_rows[heads + h:heads + h + 1, :]
        b_tot = b_r[:, L - 1:L]
        c_prev = carried(c_st[h])
        n_prev = carried(n_st[h])
        m_prev = carried(m_st[h][:, 0:1])

        d = jnp.where(causal, b_c - b_r + li_r, -jnp.inf)
        inter_log = b_c + m_prev
        m_comb = jnp.maximum(inter_log, jnp.max(d, axis=-1, keepdims=True))
        s = lax.dot_general(qb, kb, _NT, preferred_element_type=F32) * jnp.exp(d - m_comb)
        w_inter = jnp.exp(inter_log - m_comb)
        num = _dot(s.astype(BF16), vb) + w_inter * lax.dot_general(
            qb, c_prev.astype(BF16), _NT, preferred_element_type=F32)
        den = jnp.sum(s, axis=-1, keepdims=True) + w_inter * jnp.sum(q * n_prev, axis=-1, keepdims=True)
        den = jnp.maximum(jnp.abs(den), jnp.exp(-m_comb))
        hh = num / den
        hh = hh * lax.rsqrt(jnp.mean(hh * hh, axis=-1, keepdims=True) + EPS)
        gate = _sigmoid(zq[rows, 3 * W + h * dh:3 * W + (h + 1) * dh].astype(F32))
        out_ref[rows, hs] = (hh * gain_ref[:, hs] * gate).astype(out_ref.dtype)

        a = b_tot - b_c + li_c
        m_loc = jnp.max(a, axis=0, keepdims=True)
        w = jnp.exp(a - m_loc)
        c_loc = lax.dot_general((w * v).astype(BF16), kb, _TN, preferred_element_type=F32)
        n_loc = jnp.sum(w * k, axis=0, keepdims=True)
        m_new = jnp.maximum(b_tot + m_prev, m_loc)
        s_old = jnp.exp(b_tot + m_prev - m_new)
        s_loc = jnp.exp(m_loc - m_new)
        c_st[h] = s_old * c_prev + s_loc * c_loc
        n_st[h] = s_old * n_prev + s_loc * n_loc
        m_st[h] = jnp.broadcast_to(m_new, (1, LANES))


def _mixin_kernel(x_ref, g_ref, mod_ref, wq_ref, wr_ref, wif_ref, wift_ref, convw_ref, bcol_ref, brow_ref, gain_ref,
                  ha_ref, zr_ref, h_ref, zring, gcring, grring, cbuf, c_st, n_st, m_st,
                  *, heads, tiles_per_seq, col_chunk):
    s = pl.program_id(0)
    tm = x_ref.shape[0]
    W = ha_ref.shape[1]
    L = MLSTM_CHUNK
    n_chunks = tm // L
    qkvo = 4 * W
    slot_a = lax.rem(s, 2)
    slot_b = 1 - slot_a

    @pl.when(s == 0)
    def _():
        zring[1] = jnp.zeros(zring.shape[1:], zring.dtype)
        gcring[1] = jnp.zeros(gcring.shape[1:], F32)
        grring[1] = jnp.zeros(grring.shape[1:], F32)
        cbuf[...] = jnp.zeros_like(cbuf)
        c_st[...] = jnp.zeros_like(c_st)
        n_st[...] = jnp.zeros_like(n_st)
        m_st[...] = jnp.zeros_like(m_st)

    h = _norm_mod(x_ref[...], g_ref[...], mod_ref[0], mod_ref[1]).astype(BF16)
    h_ref[...] = h
    gcring[slot_a] = _dot(h, wif_ref[...])
    grring[slot_a] = lax.dot_general(wift_ref[...], h, _NT, preferred_element_type=F32)

    fresh = lax.rem(s + tiles_per_seq - 1, tiles_per_seq) == 0
    zq = zring.at[slot_b]
    col_starts = list(range(0, qkvo + wr_ref.shape[1], col_chunk))
    share = -(-len(col_starts) // n_chunks)
    for c in range(n_chunks):
        for c0 in col_starts[c * share:(c + 1) * share]:
            if c0 < qkvo:
                zring[slot_a, :, c0:c0 + col_chunk] = _dot(h_ref[...], wq_ref[:, c0:c0 + col_chunk]).astype(BF16)
            else:
                cr = slice(c0 - qkvo, c0 - qkvo + col_chunk)
                zr_ref[:, cr] = _dot(h_ref[...], wr_ref[:, cr]).astype(BF16)
        r0 = c * L
        gc = gcring[slot_b, r0:r0 + L, :] + bcol_ref[...]
        gr = grring[slot_b, :, r0:r0 + L] + brow_ref[...]
        _mlstm_chunk(zq, gc, gr, convw_ref, gain_ref, ha_ref, r0, cbuf, c_st, n_st, m_st,
                     heads=heads, fresh=fresh if c == 0 else None)


def _mixin(x, g, mod, w_qkvo, layer, w_rest, w_if, w_ift, conv_w, bias_col, bias_row, head_gain, width, heads, tm=512,
           col_chunk=256):
    b, t, d = x.shape
    n = b * t
    ncols = w_qkvo.shape[2] + w_rest.shape[1]
    ng = w_ift.shape[0]
    dh = width // heads
    tm = min(tm, t)
    nt = t // tm
    n_tiles = b * nt
    qkvo = 4 * width
    assert t % tm == 0 and tm % MLSTM_CHUNK == 0 and conv_w.shape[0] - 1 <= SUBLANES and dh % LANES == 0
    assert ncols % col_chunk == 0 and qkvo % col_chunk == 0 and w_qkvo.shape[2] == qkvo
    cur = lambda s: jnp.minimum(s, n_tiles - 1)
    prev = lambda s: jnp.maximum(s - 1, 0)
    const = lambda shape: pl.BlockSpec(shape, lambda s: (0,) * len(shape))
    return pl.pallas_call(
        functools.partial(_mixin_kernel, heads=heads, tiles_per_seq=nt, col_chunk=col_chunk),
        grid=(n_tiles + 1,),
        in_specs=[
            pl.BlockSpec((tm, d), lambda s: (cur(s), 0)),
            const((1, d)),
            pl.BlockSpec((None, 6, 1, d), lambda s: (cur(s) // nt, 0, 0, 0)),
            pl.BlockSpec((None, d, qkvo), lambda s: (layer, 0, 0), pipeline_mode=pl.Buffered(1)),
            pl.BlockSpec((d, ncols - qkvo), lambda s: (0, 0), pipeline_mode=pl.Buffered(1)),
            const((d, LANES)), const((ng, d)), const(conv_w.shape), const((1, LANES)), const((ng, 1)), const((1, width)),
        ],
        out_specs=[
            pl.BlockSpec((tm, width), lambda s: (prev(s), 0)),
            pl.BlockSpec((tm, ncols - qkvo), lambda s: (cur(s), 0)),
        ],
        out_shape=[
            jax.ShapeDtypeStruct((n, width), BF16),
            jax.ShapeDtypeStruct((n, ncols - qkvo), BF16),
        ],
        scratch_shapes=[
            pltpu.VMEM((tm, d), BF16),
            pltpu.VMEM((2, tm, qkvo), BF16),
            pltpu.VMEM((2, tm, LANES), F32),
            pltpu.VMEM((2, ng, tm), F32),
            pltpu.VMEM((2 * width // LANES, SUBLANES + MLSTM_CHUNK, LANES), F32),
            pltpu.VMEM((heads, dh, dh), F32),
            pltpu.VMEM((heads, 1, dh), F32),
            pltpu.VMEM((heads, 1, LANES), F32),
        ],
        compiler_params=_params(("arbitrary",)),
        name="mixin",
    )(x.reshape(n, d), g, mod, w_qkvo, w_rest, w_if, w_ift, conv_w, bias_col, bias_row, head_gain)


def _mixout_tile(ha_ref, u_ref, ga_ref, gb_ref, x_ref, mod_ref, poolw_ref, pscale_ref, pa_ref, pb_ref, wo_ref, ubuf,
                 windows):
    tm = x_ref.shape[0]
    gd = poolw_ref.shape[1]
    halo = max(windows)
    i = pl.program_id(1)

    @pl.when(i == 0)
    def _():
        ubuf[:, 0:halo, :] = jnp.zeros((ubuf.shape[0], halo, gd), F32)

    tpos = i * tm + lax.broadcasted_iota(jnp.int32, (tm, 1), 0)
    parts = []
    for g, win in enumerate(windows):
        cur = u_ref[:, g * gd:(g + 1) * gd].astype(F32)
        ubuf[g, halo:halo + tm, :] = cur
        wsum = cur
        for j in range(1, win):
            wsum = wsum + ubuf[g, halo - j:halo - j + tm, :]
        ubuf[g, 0:halo, :] = ubuf[g, tm:tm + halo, :]
        count = jnp.minimum(tpos + 1, win).astype(F32)
        pooled = wsum / count - cur
        parts.append(_dot(pooled.astype(BF16), poolw_ref[g]))
    hb = (jnp.concatenate(parts, axis=-1) * pscale_ref[...]).astype(BF16)

    pa = _dot(ha_ref[...], pa_ref[...])
    pb = _dot(hb, pb_ref[...])
    merged = _sigmoid(ga_ref[...].astype(F32)) * pa + _sigmoid(gb_ref[...].astype(F32)) * pb
    y = _dot(merged.astype(BF16), wo_ref[...])
    return x_ref[...] + mod_ref[2] * y


def _mixout_specs(h_a, zr, x, pool_w, pool_scale, proj_a, proj_b, w_out, tm):
    b, t, d = x.shape
    p = pool_scale.shape[1]
    width = h_a.shape[2]
    u_blk = 2 * d // p
    assert t % tm == 0 and (2 * d) % p == 0
    const = lambda shape: pl.BlockSpec(shape, lambda bi, i: (0,) * len(shape))
    specs = [
        pl.BlockSpec((None, tm, width), lambda bi, i: (bi, i, 0)),
        pl.BlockSpec((None, tm, p), lambda bi, i: (bi, i, u_blk)),
        pl.BlockSpec((None, tm, d), lambda bi, i: (bi, i, 0)),
        pl.BlockSpec((None, tm, d), lambda bi, i: (bi, i, 1)),
        pl.BlockSpec((None, tm, d), lambda bi, i: (bi, i, 0)),
        pl.BlockSpec((None, 6, 1, d), lambda bi, i: (bi, 0, 0, 0)),
        const(pool_w.shape), const(pool_scale.shape), const(proj_a.shape), const(proj_b.shape), const(w_out.shape),
    ]
    return specs, const


def _mixout_kernel(ha_ref, u_ref, ga_ref, gb_ref, x_ref, mod_ref, poolw_ref, pscale_ref, pa_ref, pb_ref, wo_ref,
                   out_ref, ubuf, *, windows):
    out_ref[...] = _mixout_tile(ha_ref, u_ref, ga_ref, gb_ref, x_ref, mod_ref, poolw_ref, pscale_ref, pa_ref, pb_ref,
                                wo_ref, ubuf, windows)


def _mixout(h_a, zr, x, mod, mix_w, tm=512):
    b, t, d = x.shape
    tm = min(tm, t)
    specs, _ = _mixout_specs(h_a, zr, x, *mix_w, tm)
    pool_w = mix_w[0]
    return pl.pallas_call(
        functools.partial(_mixout_kernel, windows=POOL_WINDOWS),
        grid=(b, t // tm),
        in_specs=specs,
        out_specs=pl.BlockSpec((None, tm, d), lambda bi, i: (bi, i, 0)),
        out_shape=jax.ShapeDtypeStruct((b, t, d), F32),
        scratch_shapes=[pltpu.VMEM((pool_w.shape[0], max(POOL_WINDOWS) + tm, pool_w.shape[1]), F32)],
        compiler_params=_params(("arbitrary", "arbitrary")),
        name="mixout",
    )(h_a, zr, zr, zr, x, mod, *mix_w)


def _mix_ffn_kernel(ha_ref, u_ref, ga_ref, gb_ref, x_ref, mod_ref, poolw_ref, pscale_ref, pa_ref, pb_ref, wo_ref,
                    g_ref, wg_ref, wu_ref, wd_ref, fin_ref, out_ref, ubuf, *, windows, final_norm, ff_chunk):
    x = _mixout_tile(ha_ref, u_ref, ga_ref, gb_ref, x_ref, mod_ref, poolw_ref, pscale_ref, pa_ref, pb_ref, wo_ref,
                     ubuf, windows)
    h = _norm_mod(x, g_ref[...], mod_ref[3], mod_ref[4]).astype(BF16)
    acc = None
    for c0 in range(0, wg_ref.shape[1], ff_chunk):
        gate = _dot(h, wg_ref[:, c0:c0 + ff_chunk])
        act = (gate * _sigmoid(gate) * _dot(h, wu_ref[:, c0:c0 + ff_chunk])).astype(BF16)
        part = _dot(act, wd_ref[c0:c0 + ff_chunk, :])
        acc = part if acc is None else acc + part
    y = x + mod_ref[5] * acc
    out_ref[...] = _rms(y, fin_ref[...]) if final_norm else y


def _mix_ffn(h_a, zr, x, mod, mix_w, g, w_gate, w_up, w_down, fin, final_norm, tm=512, ff_chunk=1408):
    b, t, d = x.shape
    ff = w_gate.shape[1]
    tm = min(tm, t)
    ff_chunk = min(ff_chunk, ff)
    assert ff % ff_chunk == 0
    specs, const = _mixout_specs(h_a, zr, x, *mix_w, tm)
    pool_w = mix_w[0]
    return pl.pallas_call(
        functools.partial(_mix_ffn_kernel, windows=POOL_WINDOWS, final_norm=final_norm, ff_chunk=ff_chunk),
        grid=(b, t // tm),
        in_specs=specs + [const((1, d)), const((d, ff)), const((d, ff)), const((ff, d)), const((1, d))],
        out_specs=pl.BlockSpec((None, tm, d), lambda bi, i: (bi, i, 0)),
        out_shape=jax.ShapeDtypeStruct((b, t, d), F32),
        scratch_shapes=[pltpu.VMEM((pool_w.shape[0], max(POOL_WINDOWS) + tm, pool_w.shape[1]), F32)],
        compiler_params=_params(("arbitrary", "arbitrary")),
        name="mix_ffn",
    )(h_a, zr, zr, zr, x, mod, *mix_w, g, w_gate, w_up, w_down, fin)


def _to_slabs(ref, x):
    rows, d = x.shape
    per = d // LANES
    for k in range(per):
        ref[pl.ds(k, rows, stride=per), :] = x[:, k * LANES:(k + 1) * LANES]


def _from_slabs(ref, rows):
    per = ref.shape[0] // rows
    return jnp.concatenate([ref[pl.ds(k, rows, stride=per), :] for k in range(per)], axis=-1)


def _router_kernel(x_ref, g_ref, mod_ref, rw_ref, rb_ref, h_ref, info_ref, cnt_ref, carry_ref, *, n_experts, cap):
    first = (pl.program_id(0) == 0) & (pl.program_id(1) == 0)

    @pl.when(first)
    def _():
        carry_ref[...] = jnp.zeros_like(carry_ref)

    h = _norm_mod(x_ref[...], g_ref[...], mod_ref[3], mod_ref[4])
    _to_slabs(h_ref, h)
    tm = h.shape[0]
    lane = lax.broadcasted_iota(jnp.int32, (tm, LANES), 1)
    logits = jnp.where(lane < n_experts, _dot_split(h, rw_ref[...]) + rb_ref[...], -jnp.inf)
    v1 = jnp.max(logits, axis=-1, keepdims=True)
    i1 = jnp.min(jnp.where(logits == v1, lane, LANES), axis=-1, keepdims=True)
    rest = jnp.where(lane == i1, -jnp.inf, logits)
    v2 = jnp.max(rest, axis=-1, keepdims=True)
    i2 = jnp.min(jnp.where(rest == v2, lane, LANES), axis=-1, keepdims=True)
    e2 = jnp.exp(v2 - v1)
    w1 = 1.0 / (1.0 + e2)
    w2 = e2 / (1.0 + e2)
    sel1 = lane == i1
    sel2 = lane == i2
    picked = jnp.where(sel1 | sel2, 1.0, 0.0)
    row = lax.broadcasted_iota(jnp.int32, (tm, tm), 0)
    col = lax.broadcasted_iota(jnp.int32, (tm, tm), 1)
    before = jnp.where(col < row, 1.0, 0.0).astype(BF16)
    ex = _dot(before, picked.astype(BF16)) + carry_ref[...]
    pos1 = i1.astype(F32) * cap + jnp.sum(jnp.where(sel1, ex, 0.0), axis=-1, keepdims=True)
    pos2 = i2.astype(F32) * cap + jnp.sum(jnp.where(sel2, ex, 0.0), axis=-1, keepdims=True)
    carry = carry_ref[...] + jnp.sum(picked, axis=0, keepdims=True)
    carry_ref[...] = carry
    cnt_ref[...] = carry
    info_ref[...] = (jnp.where(lane == 0, pos1, 0.0) + jnp.where(lane == 1, pos2, 0.0)
                     + jnp.where(lane == 2, w1, 0.0) + jnp.where(lane == 3, w2, 0.0))


def _router(x, g, mod, router_w, router_b, tm=512):
    b, t, d = x.shape
    n_experts = router_w.shape[1]
    tm = min(tm, t)
    nt = t // tm
    per = d // LANES
    assert t % tm == 0 and d % LANES == 0 and n_experts <= LANES and n_experts * b * t < 2 ** 24
    rw = jnp.zeros((d, LANES), F32).at[:, :n_experts].set(router_w)
    rb = jnp.zeros((1, LANES), F32).at[:, :n_experts].set(router_b[None, :])
    return pl.pallas_call(
        functools.partial(_router_kernel, n_experts=n_experts, cap=b * t),
        grid=(b, t // tm),
        in_specs=[
            pl.BlockSpec((None, tm, d), lambda bi, i: (bi, i, 0)),
            pl.BlockSpec((1, d), lambda bi, i: (0, 0)),
            pl.BlockSpec((None, 6, 1, d), lambda bi, i: (bi, 0, 0, 0)),
            pl.BlockSpec((d, LANES), lambda bi, i: (0, 0)),
            pl.BlockSpec((1, LANES), lambda bi, i: (0, 0)),
        ],
        out_specs=[
            pl.BlockSpec((tm * per, LANES), lambda bi, i: (bi * nt + i, 0)),
            pl.BlockSpec((None, tm, LANES), lambda bi, i: (bi, i, 0)),
            pl.BlockSpec((1, LANES), lambda bi, i: (0, 0)),
        ],
        out_shape=[jax.ShapeDtypeStruct((b * t * per, LANES), F32), jax.ShapeDtypeStruct((b, t, LANES), F32),
                   jax.ShapeDtypeStruct((1, LANES), F32)],
        scratch_shapes=[pltpu.VMEM((1, LANES), F32)],
        compiler_params=_params(("arbitrary", "arbitrary")),
        name="router",
    )(x, g, mod, rw, rb)


def _token_copy(src, src_tok, dst, dst_tok, sem, per):
    return pltpu.make_async_copy(src.at[pl.ds(pl.multiple_of(src_tok * per, per), per)],
                                 dst.at[pl.ds(pl.multiple_of(dst_tok * per, per), per)], sem)


def _scatter_kernel(pos_ref, h_ref, hs_hbm, sem, *, per):
    rows = h_ref.shape[0] // per
    base = pl.program_id(0) * rows

    def copies(r):
        t = base + r
        return (_token_copy(h_ref, r, hs_hbm, pos_ref[TOP_K * t], sem, per),
                _token_copy(h_ref, r, hs_hbm, pos_ref[TOP_K * t + 1], sem, per))

    def start(r, carry):
        for slot, cp in enumerate(copies(r)):
            cp.start(priority=slot)
        return carry

    def wait(r, carry):
        for cp in copies(r):
            cp.wait()
        return carry

    lax.fori_loop(0, rows, start, 0, unroll=8)
    lax.fori_loop(0, rows, wait, 0, unroll=8)


def _scatter_rows(pos, h, n, n_slots_out, rows=512):
    per = h.shape[0] // n
    rows = min(rows, n)
    assert n % rows == 0
    return pl.pallas_call(
        functools.partial(_scatter_kernel, per=per),
        grid_spec=pltpu.PrefetchScalarGridSpec(
            num_scalar_prefetch=1,
            grid=(n // rows,),
            in_specs=[pl.BlockSpec((rows * per, LANES), lambda i, pos: (i, 0))],
            out_specs=pl.BlockSpec(memory_space=pl.ANY),
            scratch_shapes=[pltpu.SemaphoreType.DMA],
        ),
        out_shape=jax.ShapeDtypeStruct((n_slots_out * per, LANES), h.dtype),
        compiler_params=_params(("arbitrary",)),
        name="moe_scatter",
    )(pos, h)


def _moe_ffn_kernel(te_ref, tb_ref, tv_ref, hs_ref, wg_ref, wu_ref, wd_ref, y_ref, xb_ref, acc_ref):
    i = pl.program_id(0)
    j = pl.program_id(1)
    valid = tv_ref[i]

    @pl.when(valid > 0)
    def _():
        @pl.when(j == 0)
        def _():
            row = lax.broadcasted_iota(jnp.int32, (xb_ref.shape[0], 1), 0)
            xb_ref[...] = jnp.where(row < valid, _from_slabs(hs_ref, xb_ref.shape[0]), 0.0).astype(BF16)
            acc_ref[...] = jnp.zeros_like(acc_ref)

        tm = xb_ref.shape[0]
        step = tm // MOE_TILE_PARTS
        for part in range(1, MOE_TILE_PARTS + 1):
            rows = part * step

            @pl.when((valid > rows - step) & (valid <= rows))
            def _(rows=rows):
                h = xb_ref[0:rows, :]
                gate = _dot(h, wg_ref[...].astype(BF16))
                act = (gate * _sigmoid(gate) * _dot(h, wu_ref[...].astype(BF16))).astype(BF16)
                acc_ref[0:rows, :] += _dot(act, wd_ref[...].astype(BF16))

        @pl.when(j == pl.num_programs(1) - 1)
        def _():
            _to_slabs(y_ref, acc_ref[...])


def _moe_ffn(tile_e, tile_blk, tile_valid, hs, w_gate, w_up, w_down, tm, tf=512):
    n_experts, d, ff = w_gate.shape
    per = d // LANES
    tf = min(tf, ff)
    nf = ff // tf
    assert ff % tf == 0
    n_tiles = tile_e.shape[0]
    jj = lambda i, j, tv: jnp.where(tv[i] > 0, j, nf - 1)
    return pl.pallas_call(
        _moe_ffn_kernel,
        grid_spec=pltpu.PrefetchScalarGridSpec(
            num_scalar_prefetch=3,
            grid=(n_tiles, nf),
            in_specs=[
                pl.BlockSpec((tm * per, LANES), lambda i, j, te, tb, tv: (tb[i], 0)),
                pl.BlockSpec((None, d, tf), lambda i, j, te, tb, tv: (te[i], 0, jj(i, j, tv))),
                pl.BlockSpec((None, d, tf), lambda i, j, te, tb, tv: (te[i], 0, jj(i, j, tv))),
                pl.BlockSpec((None, tf, d), lambda i, j, te, tb, tv: (te[i], jj(i, j, tv), 0)),
            ],
            out_specs=pl.BlockSpec((tm * per, LANES), lambda i, j, te, tb, tv: (tb[i], 0)),
            scratch_shapes=[pltpu.VMEM((tm, d), BF16), pltpu.VMEM((tm, d), F32)],
        ),
        out_shape=jax.ShapeDtypeStruct(hs.shape, F32),
        compiler_params=_params(("arbitrary", "arbitrary")),
        name="moe_ffn",
    )(tile_e, tile_blk, tile_valid, hs, w_gate, w_up, w_down)


def _combine_kernel(pos_ref, x_ref, info_ref, mod_ref, fin_ref, y_hbm, out_ref, ybuf, sem, *, final_norm):
    rows = x_ref.shape[0]
    base = pl.program_id(0) * rows

    per = ybuf.shape[1] // rows

    def copies(r):
        t = base + r
        return (_token_copy(y_hbm, pos_ref[TOP_K * t], ybuf.at[0], r, sem, per),
                _token_copy(y_hbm, pos_ref[TOP_K * t + 1], ybuf.at[1], r, sem, per))

    def start(r, carry):
        for slot, cp in enumerate(copies(r)):
            cp.start(priority=slot)
        return carry

    def wait(r, carry):
        for cp in copies(r):
            cp.wait()
        return carry

    lax.fori_loop(0, rows, start, 0, unroll=8)
    lax.fori_loop(0, rows, wait, 0, unroll=8)
    info = info_ref[...]
    f = info[:, 2:3] * _from_slabs(ybuf.at[0], rows) + info[:, 3:4] * _from_slabs(ybuf.at[1], rows)
    y = x_ref[...] + mod_ref[5] * f
    out_ref[...] = _rms(y, fin_ref[...]) if final_norm else y


def _combine(pos, x, info, mod, fin, y, final_norm, rows=256):
    b, t, d = x.shape
    n = b * t
    rows = min(rows, t)
    assert t % rows == 0
    per_b = t // rows
    return pl.pallas_call(
        functools.partial(_combine_kernel, final_norm=final_norm),
        grid_spec=pltpu.PrefetchScalarGridSpec(
            num_scalar_prefetch=1,
            grid=(n // rows,),
            in_specs=[
                pl.BlockSpec((rows, d), lambda i, pos: (i, 0)),
                pl.BlockSpec((rows, LANES), lambda i, pos: (i, 0)),
                pl.BlockSpec((None, 6, 1, d), lambda i, pos: (i // per_b, 0, 0, 0)),
                pl.BlockSpec((1, d), lambda i, pos: (0, 0)),
                pl.BlockSpec(memory_space=pl.ANY),
            ],
            out_specs=pl.BlockSpec((rows, d), lambda i, pos: (i, 0)),
            scratch_shapes=[pltpu.VMEM((TOP_K, rows * (d // LANES), LANES), F32), pltpu.SemaphoreType.DMA],
        ),
        out_shape=jax.ShapeDtypeStruct((n, d), F32),
        compiler_params=_params(("arbitrary",)),
        name="moe_combine",
    )(pos, x.reshape(n, d), info.reshape(n, LANES), mod, fin, y).reshape(b, t, d)


def _tile_tables(counts, n_experts, cap, tm, n_tiles):
    counts = counts.astype(jnp.int32)
    tiles_per = (counts + tm - 1) // tm
    ends = jnp.cumsum(tiles_per)
    used = ends[-1]
    i = jnp.minimum(jnp.arange(n_tiles, dtype=jnp.int32), used - 1)
    e = jnp.sum((i[:, None] >= ends[None, :]).astype(jnp.int32), axis=1)
    k = i - (ends - tiles_per)[e]
    valid = jnp.where(jnp.arange(n_tiles) < used, jnp.minimum(counts[e] - k * tm, tm), 0)
    return e, e * (cap // tm) + k, valid.astype(jnp.int32)


def _moe(x, h, info, counts, mod, w_gate, w_up, w_down, fin, final_norm, tm=1024):
    b, t, d = x.shape
    n = b * t
    n_experts = w_gate.shape[0]
    tm = min(tm, n)
    assert n % tm == 0 and tm % (MOE_TILE_PARTS * 2 * SUBLANES) == 0
    pos = info[:, :, :TOP_K].astype(jnp.int32).reshape(n * TOP_K)
    hs = _scatter_rows(pos, h, n, n_experts * n)
    n_tiles = TOP_K * n // tm + n_experts
    tile_e, tile_blk, tile_valid = _tile_tables(counts[0, :n_experts], n_experts, n, tm, n_tiles)
    y = _moe_ffn(tile_e, tile_blk, tile_valid, hs, w_gate, w_up, w_down, tm)
    return _combine(pos, x, info, mod, fin, y, final_norm)


def kernel(x, c, norm_mix, norm_ffn, w_ada, b_ada, w_in, conv_w, i_bias, f_bias, head_gain, pool_w, pool_scale,
           proj_a, proj_b, w_out, ffn_w_gate, ffn_w_up, ffn_w_down, router_w, router_b, moe_w_gate, moe_w_up,
           moe_w_down, final_norm):
    depth = w_in.shape[0]
    b, t, d = x.shape
    heads = i_bias.shape[1]
    width = head_gain.shape[1]
    p = pool_scale.shape[1]
    ng = 2 * heads
    qkvo = 4 * width

    mod_all = _adaln(c, w_ada, b_ada).reshape(depth, b, 6, 1, d)
    fin = final_norm.reshape(1, d)

    w_qkvo = w_in[:, :, :qkvo].astype(BF16)

    for l in range(depth):
        mod = mod_all[l]
        w_rest = jnp.concatenate([w_in[l, :, qkvo + ng + p:], w_in[l, :, qkvo + ng:qkvo + ng + p]], axis=1).astype(BF16)
        w_gates = w_in[l, :, qkvo:qkvo + ng]
        w_if = jnp.zeros((d, LANES), BF16).at[:, :ng].set(w_gates.astype(BF16))
        w_ift = w_gates.T.astype(BF16)
        bias = jnp.concatenate([i_bias[l], f_bias[l]])
        bias_col = jnp.zeros((1, LANES), F32).at[0, :ng].set(bias)
        bias_row = bias.reshape(ng, 1)

        h_a, zr = _mixin(x, norm_mix[l].reshape(1, d), mod, w_qkvo, l, w_rest, w_if, w_ift, conv_w[l], bias_col,
                         bias_row, head_gain[l].reshape(1, width), width, heads)
        h_a = h_a.reshape(b, t, width)
        zr = zr.reshape(b, t, -1)
        mix_w = (pool_w[l].astype(BF16), pool_scale[l].reshape(1, p), proj_a[l].astype(BF16), proj_b[l].astype(BF16),
                 w_out[l].astype(BF16))

        last = l == depth - 1
        j = l // 2
        g_ffn = norm_ffn[l].reshape(1, d)
        if l % 2 == 0:
            x = _mix_ffn(h_a, zr, x, mod, mix_w, g_ffn, ffn_w_gate[j].astype(BF16), ffn_w_up[j].astype(BF16),
                         ffn_w_down[j].astype(BF16), fin, last)
        else:
            x = _mixout(h_a, zr, x, mod, mix_w)
            h, info, counts = _router(x, g_ffn, mod, router_w[j], router_b[j])
            x = _moe(x, h, info, counts, mod, moe_w_gate[j], moe_w_up[j], moe_w_down[j], fin, last)
    return x
```

```python
import functools

import jax
import jax.numpy as jnp
from jax import lax
from jax.experimental import pallas as pl
from jax.experimental.pallas import tpu as pltpu

F32 = jnp.float32
BF16 = jnp.bfloat16

EPS = 1e-6
MLSTM_CHUNK = 256
POOL_WINDOWS = (2, 4, 8, 16)
TOP_K = 2
MOE_TILE_PARTS = 4
LANES = 128
SUBLANES = 8
VMEM_LIMIT = 56 * 1024 * 1024

_NT = (((1,), (1,)), ((), ()))
_TN = (((0,), (0,)), ((), ()))


def _params(sem):
    return pltpu.CompilerParams(dimension_semantics=sem, vmem_limit_bytes=VMEM_LIMIT)


def _sigmoid(x):
    return 1.0 / (1.0 + jnp.exp(-x))


def _log_sigmoid(x):
    return jnp.minimum(x, 0.0) - jnp.log(1.0 + jnp.exp(-jnp.abs(x)))


def _rms(x, g):
    return x * lax.rsqrt(jnp.mean(x * x, axis=-1, keepdims=True) + EPS) * g


def _norm_mod(x, g, shift, scale):
    return _rms(x, g) * (1.0 + scale) + shift


def _dot(a, b):
    return jnp.dot(a, b, preferred_element_type=F32)


def _split3(x):
    p0 = x.astype(BF16)
    r0 = x - p0.astype(F32)
    p1 = r0.astype(BF16)
    p2 = (r0 - p1.astype(F32)).astype(BF16)
    return p0, p1, p2


def _dot_split(a, b):
    a_hi = a.astype(BF16)
    a_lo = (a - a_hi.astype(F32)).astype(BF16)
    b_hi = b.astype(BF16)
    b_lo = (b - b_hi.astype(F32)).astype(BF16)
    return _dot(a_hi, b_hi) + (_dot(a_hi, b_lo) + _dot(a_lo, b_hi))


def _adaln_kernel(c_ref, w_ref, b_ref, o_ref):
    c = c_ref[...]
    o_ref[...] = _dot_split(c * _sigmoid(c), w_ref[...]) + b_ref[...]


def _adaln(c, w_ada, b_ada, tn=1536):
    depth, d, n = w_ada.shape
    b = c.shape[0]
    assert n % tn == 0
    return pl.pallas_call(
        _adaln_kernel,
        grid=(depth, n // tn),
        in_specs=[
            pl.BlockSpec((b, d), lambda l, j: (0, 0)),
            pl.BlockSpec((None, d, tn), lambda l, j: (l, 0, j)),
            pl.BlockSpec((None, 1, tn), lambda l, j: (l, 0, j)),
        ],
        out_specs=pl.BlockSpec((None, b, tn), lambda l, j: (l, 0, j)),
        out_shape=jax.ShapeDtypeStruct((depth, b, n), F32),
        compiler_params=_params(("arbitrary", "arbitrary")),
        name="adaln",
    )(c, w_ada, b_ada.reshape(depth, 1, n))


def _mlstm_chunk(zq, gc, gr, convw_ref, gain_ref, out_ref, r0, cbuf, c_st, n_st, m_st, *, heads, fresh):
    L = MLSTM_CHUNK
    W = out_ref.shape[1]
    dh = W // heads
    taps = convw_ref.shape[0]
    halo = SUBLANES
    rows = slice(r0, r0 + L)

    def carried(x):
        return x if fresh is None else jnp.where(fresh, 0.0, x)

    def conv_silu(col0, slab0, scale):
        outs = []
        for cb in range(W // LANES):
            slab = slab0 + cb
            cs = slice(slab * LANES, (slab + 1) * LANES)
            if fresh is not None:
                cbuf[slab, 0:halo, :] = carried(cbuf[slab, 0:halo, :])
            cbuf[slab, halo:halo + L, :] = zq[rows, col0 + cb * LANES:col0 + (cb + 1) * LANES].astype(F32)
            acc = convw_ref[taps - 1:taps, cs] * cbuf[slab, halo:halo + L, :]
            for j in range(taps - 1):
                off = halo - (taps - 1) + j
                acc = acc + convw_ref[j:j + 1, cs] * cbuf[slab, off:off + L, :]
            cbuf[slab, 0:halo, :] = cbuf[slab, L:L + halo, :]
            outs.append(acc * _sigmoid(acc) * scale if scale != 1.0 else acc * _sigmoid(acc))
        return outs

    q_slabs = conv_silu(0, 0, dh ** -0.5)
    k_slabs = conv_silu(W, W // LANES, 1.0)
    per_head = dh // LANES

    row = lax.broadcasted_iota(jnp.int32, (L, L), 0)
    col = lax.broadcasted_iota(jnp.int32, (L, L), 1)
    causal = row >= col
    tri_low = jnp.where(causal, 1.0, 0.0).astype(BF16)
    tri_up = jnp.where(row <= col, 1.0, 0.0).astype(BF16)
    b_cols = sum(_dot(tri_low, part) for part in _split3(_log_sigmoid(gc)))
    b_rows = sum(_dot(part, tri_up) for part in _split3(_log_sigmoid(gr)))

    for h in range(heads):
        hs = slice(h * dh, (h + 1) * dh)
        q = jnp.concatenate(q_slabs[h * per_head:(h + 1) * per_head], axis=-1)
        k = jnp.concatenate(k_slabs[h * per_head:(h + 1) * per_head], axis=-1)
        vb = zq[rows, 2 * W + h * dh:2 * W + (h + 1) * dh]
        v = vb.astype(F32)
        qb = q.astype(BF16)
        kb = k.astype(BF16)
        li_c = gc[:, h:h + 1]
        b_c = b_cols[:, heads + h:heads + h + 1]
        li_r = gr[h:h + 1, :]
        b_r = b_rows[heads + h:heads + h + 1, :]
        b_tot = b_r[:, L - 1:L]
        c_prev = carried(c_st[h])
        n_prev = carried(n_st[h])
        m_prev = carried(m_st[h][:, 0:1])

        d = jnp.where(causal, b_c - b_r + li_r, -jnp.inf)
        inter_log = b_c + m_prev
        m_comb = jnp.maximum(inter_log, jnp.max(d, axis=-1, keepdims=True))
        s = lax.dot_general(qb, kb, _NT, preferred_element_type=F32) * jnp.exp(d - m_comb)
        w_inter = jnp.exp(inter_log - m_comb)
        num = _dot(s.astype(BF16), vb) + w_inter * lax.dot_general(
            qb, c_prev.astype(BF16), _NT, preferred_element_type=F32)
        den = jnp.sum(s, axis=-1, keepdims=True) + w_inter * jnp.sum(q * n_prev, axis=-1, keepdims=True)
        den = jnp.maximum(jnp.abs(den), jnp.exp(-m_comb))
        hh = num / den
        hh = hh * lax.rsqrt(jnp.mean(hh * hh, axis=-1, keepdims=True) + EPS)
        gate = _sigmoid(zq[rows, 3 * W + h * dh:3 * W + (h + 1) * dh].astype(F32))
        out_ref[rows, hs] = (hh * gain_ref[:, hs] * gate).astype(out_ref.dtype)

        a = b_tot - b_c + li_c
        m_loc = jnp.max(a, axis=0, keepdims=True)
        w = jnp.exp(a - m_loc)
        c_loc = lax.dot_general((w * v).astype(BF16), kb, _TN, preferred_element_type=F32)
        n_loc = jnp.sum(w * k, axis=0, keepdims=True)
        m_new = jnp.maximum(b_tot + m_prev, m_loc)
        s_old = jnp.exp(b_tot + m_prev - m_new)
        s_loc = jnp.exp(m_loc - m_new)
        c_st[h] = s_old * c_prev + s_loc * c_loc
        n_st[h] = s_old * n_prev + s_loc * n_loc
        m_st[h] = jnp.broadcast_to(m_new, (1, LANES))


def _mixin_kernel(x_ref, g_ref, mod_ref, wq_ref, wr_ref, wif_ref, wift_ref, convw_ref, bcol_ref, brow_ref, gain_ref,
                  ha_ref, zr_ref, h_ref, zring, gcring, grring, cbuf, c_st, n_st, m_st,
                  *, heads, tiles_per_seq, col_chunk):
    s = pl.program_id(0)
    tm = x_ref.shape[0]
    W = ha_ref.shape[1]
    L = MLSTM_CHUNK
    n_chunks = tm // L
    qkvo = 4 * W
    slot_a = lax.rem(s, 2)
    slot_b = 1 - slot_a

    @pl.when(s == 0)
    def _():
        zring[1] = jnp.zeros(zring.shape[1:], zring.dtype)
        gcring[1] = jnp.zeros(gcring.shape[1:], F32)
        grring[1] = jnp.zeros(grring.shape[1:], F32)
        cbuf[...] = jnp.zeros_like(cbuf)
        c_st[...] = jnp.zeros_like(c_st)
        n_st[...] = jnp.zeros_like(n_st)
        m_st[...] = jnp.zeros_like(m_st)

    h = _norm_mod(x_ref[...], g_ref[...], mod_ref[0], mod_ref[1]).astype(BF16)
    h_ref[...] = h
    gcring[slot_a] = _dot(h, wif_ref[...])
    grring[slot_a] = lax.dot_general(wift_ref[...], h, _NT, preferred_element_type=F32)

    fresh = lax.rem(s + tiles_per_seq - 1, tiles_per_seq) == 0
    zq = zring.at[slot_b]
    col_starts = list(range(0, qkvo + wr_ref.shape[1], col_chunk))
    share = -(-len(col_starts) // n_chunks)
    for c in range(n_chunks):
        for c0 in col_starts[c * share:(c + 1) * share]:
            if c0 < qkvo:
                zring[slot_a, :, c0:c0 + col_chunk] = _dot(h_ref[...], wq_ref[:, c0:c0 + col_chunk]).astype(BF16)
            else:
                cr = slice(c0 - qkvo, c0 - qkvo + col_chunk)
                zr_ref[:, cr] = _dot(h_ref[...], wr_ref[:, cr]).astype(BF16)
        r0 = c * L
        gc = gcring[slot_b, r0:r0 + L, :] + bcol_ref[...]
        gr = grring[slot_b, :, r0:r0 + L] + brow_ref[...]
        _mlstm_chunk(zq, gc, gr, convw_ref, gain_ref, ha_ref, r0, cbuf, c_st, n_st, m_st,
                     heads=heads, fresh=fresh if c == 0 else None)


def _mixin(x, g, mod, w_qkvo, layer, w_rest, w_if, w_ift, conv_w, bias_col, bias_row, head_gain, width, heads, tm=512,
           col_chunk=256):
    b, t, d = x.shape
    n = b * t
    ncols = w_qkvo.shape[2] + w_rest.shape[1]
    ng = w_ift.shape[0]
    dh = width // heads
    tm = min(tm, t)
    nt = t // tm
    n_tiles = b * nt
    qkvo = 4 * width
    assert t % tm == 0 and tm % MLSTM_CHUNK == 0 and conv_w.shape[0] - 1 <= SUBLANES and dh % LANES == 0
    assert ncols % col_chunk == 0 and qkvo % col_chunk == 0 and w_qkvo.shape[2] == qkvo
    cur = lambda s: jnp.minimum(s, n_tiles - 1)
    prev = lambda s: jnp.maximum(s - 1, 0)
    const = lambda shape: pl.BlockSpec(shape, lambda s: (0,) * len(shape))
    return pl.pallas_call(
        functools.partial(_mixin_kernel, heads=heads, tiles_per_seq=nt, col_chunk=col_chunk),
        grid=(n_tiles + 1,),
        in_specs=[
            pl.BlockSpec((tm, d), lambda s: (cur(s), 0)),
            const((1, d)),
            pl.BlockSpec((None, 6, 1, d), lambda s: (cur(s) // nt, 0, 0, 0)),
            pl.BlockSpec((None, d, qkvo), lambda s: (layer, 0, 0), pipeline_mode=pl.Buffered(1)),
            pl.BlockSpec((d, ncols - qkvo), lambda s: (0, 0), pipeline_mode=pl.Buffered(1)),
            const((d, LANES)), const((ng, d)), const(conv_w.shape), const((1, LANES)), const((ng, 1)), const((1, width)),
        ],
        out_specs=[
            pl.BlockSpec((tm, width), lambda s: (prev(s), 0)),
            pl.BlockSpec((tm, ncols - qkvo), lambda s: (cur(s), 0)),
        ],
        out_shape=[
            jax.ShapeDtypeStruct((n, width), BF16),
            jax.ShapeDtypeStruct((n, ncols - qkvo), BF16),
        ],
        scratch_shapes=[
            pltpu.VMEM((tm, d), BF16),
            pltpu.VMEM((2, tm, qkvo), BF16),
            pltpu.VMEM((2, tm, LANES), F32),
            pltpu.VMEM((2, ng, tm), F32),
            pltpu.VMEM((2 * width // LANES, SUBLANES + MLSTM_CHUNK, LANES), F32),
            pltpu.VMEM((heads, dh, dh), F32),
            pltpu.VMEM((heads, 1, dh), F32),
            pltpu.VMEM((heads, 1, LANES), F32),
        ],
        compiler_params=_params(("arbitrary",)),
        name="mixin",
    )(x.reshape(n, d), g, mod, w_qkvo, w_rest, w_if, w_ift, conv_w, bias_col, bias_row, head_gain)


def _mixout_tile(ha_ref, u_ref, ga_ref, gb_ref, x_ref, mod_ref, poolw_ref, pscale_ref, pa_ref, pb_ref, wo_ref, ubuf,
                 windows):
    tm = x_ref.shape[0]
    gd = poolw_ref.shape[1]
    halo = max(windows)
    i = pl.program_id(1)

    @pl.when(i == 0)
    def _():
        ubuf[:, 0:halo, :] = jnp.zeros((ubuf.shape[0], halo, gd), F32)

    tpos = i * tm + lax.broadcasted_iota(jnp.int32, (tm, 1), 0)
    parts = []
    for g, win in enumerate(windows):
        cur = u_ref[:, g * gd:(g + 1) * gd].astype(F32)
        ubuf[g, halo:halo + tm, :] = cur
        wsum = cur
        for j in range(1, win):
            wsum = wsum + ubuf[g, halo - j:halo - j + tm, :]
        ubuf[g, 0:halo, :] = ubuf[g, tm:tm + halo, :]
        count = jnp.minimum(tpos + 1, win).astype(F32)
        pooled = wsum / count - cur
        parts.append(_dot(pooled.astype(BF16), poolw_ref[g]))
    hb = (jnp.concatenate(parts, axis=-1) * pscale_ref[...]).astype(BF16)

    pa = _dot(ha_ref[...], pa_ref[...])
    pb = _dot(hb, pb_ref[...])
    merged = _sigmoid(ga_ref[...].astype(F32)) * pa + _sigmoid(gb_ref[...].astype(F32)) * pb
    y = _dot(merged.astype(BF16), wo_ref[...])
    return x_ref[...] + mod_ref[2] * y


def _mixout_specs(h_a, zr, x, pool_w, pool_scale, proj_a, proj_b, w_out, tm):
    b, t, d = x.shape
    p = pool_scale.shape[1]
    width = h_a.shape[2]
    u_blk = 2 * d // p
    assert t % tm == 0 and (2 * d) % p == 0
    const = lambda shape: pl.BlockSpec(shape, lambda bi, i: (0,) * len(shape))
    specs = [
        pl.BlockSpec((None, tm, width), lambda bi, i: (bi, i, 0)),
        pl.BlockSpec((None, tm, p), lambda bi, i: (bi, i, u_blk)),
        pl.BlockSpec((None, tm, d), lambda bi, i: (bi, i, 0)),
        pl.BlockSpec((None, tm, d), lambda bi, i: (bi, i, 1)),
        pl.BlockSpec((None, tm, d), lambda bi, i: (bi, i, 0)),
        pl.BlockSpec((None, 6, 1, d), lambda bi, i: (bi, 0, 0, 0)),
        const(pool_w.shape), const(pool_scale.shape), const(proj_a.shape), const(proj_b.shape), const(w_out.shape),
    ]
    return specs, const


def _mixout_kernel(ha_ref, u_ref, ga_ref, gb_ref, x_ref, mod_ref, poolw_ref, pscale_ref, pa_ref, pb_ref, wo_ref,
                   out_ref, ubuf, *, windows):
    out_ref[...] = _mixout_tile(ha_ref, u_ref, ga_ref, gb_ref, x_ref, mod_ref, poolw_ref, pscale_ref, pa_ref, pb_ref,
                                wo_ref, ubuf, windows)


def _mixout(h_a, zr, x, mod, mix_w, tm=512):
    b, t, d = x.shape
    tm = min(tm, t)
    specs, _ = _mixout_specs(h_a, zr, x, *mix_w, tm)
    pool_w = mix_w[0]
    return pl.pallas_call(
        functools.partial(_mixout_kernel, windows=POOL_WINDOWS),
        grid=(b, t // tm),
        in_specs=specs,
        out_specs=pl.BlockSpec((None, tm, d), lambda bi, i: (bi, i, 0)),
        out_shape=jax.ShapeDtypeStruct((b, t, d), F32),
        scratch_shapes=[pltpu.VMEM((pool_w.shape[0], max(POOL_WINDOWS) + tm, pool_w.shape[1]), F32)],
        compiler_params=_params(("arbitrary", "arbitrary")),
        name="mixout",
    )(h_a, zr, zr, zr, x, mod, *mix_w)


def _mix_ffn_kernel(ha_ref, u_ref, ga_ref, gb_ref, x_ref, mod_ref, poolw_ref, pscale_ref, pa_ref, pb_ref, wo_ref,
                    g_ref, wg_ref, wu_ref, wd_ref, fin_ref, out_ref, ubuf, *, windows, final_norm, ff_chunk):
    x = _mixout_tile(ha_ref, u_ref, ga_ref, gb_ref, x_ref, mod_ref, poolw_ref, pscale_ref, pa_ref, pb_ref, wo_ref,
                     ubuf, windows)
    h = _norm_mod(x, g_ref[...], mod_ref[3], mod_ref[4]).astype(BF16)
    acc = None
    for c0 in range(0, wg_ref.shape[1], ff_chunk):
        gate = _dot(h, wg_ref[:, c0:c0 + ff_chunk])
        act = (gate * _sigmoid(gate) * _dot(h, wu_ref[:, c0:c0 + ff_chunk])).astype(BF16)
        part = _dot(act, wd_ref[c0:c0 + ff_chunk, :])
        acc = part if acc is None else acc + part
    y = x + mod_ref[5] * acc
    out_ref[...] = _rms(y, fin_ref[...]) if final_norm else y


def _mix_ffn(h_a, zr, x, mod, mix_w, g, w_gate, w_up, w_down, fin, final_norm, tm=512, ff_chunk=1408):
    b, t, d = x.shape
    ff = w_gate.shape[1]
    tm = min(tm, t)
    ff_chunk = min(ff_chunk, ff)
    assert ff % ff_chunk == 0
    specs, const = _mixout_specs(h_a, zr, x, *mix_w, tm)
    pool_w = mix_w[0]
    return pl.pallas_call(
        functools.partial(_mix_ffn_kernel, windows=POOL_WINDOWS, final_norm=final_norm, ff_chunk=ff_chunk),
        grid=(b, t // tm),
        in_specs=specs + [const((1, d)), const((d, ff)), const((d, ff)), const((ff, d)), const((1, d))],
        out_specs=pl.BlockSpec((None, tm, d), lambda bi, i: (bi, i, 0)),
        out_shape=jax.ShapeDtypeStruct((b, t, d), F32),
        scratch_shapes=[pltpu.VMEM((pool_w.shape[0], max(POOL_WINDOWS) + tm, pool_w.shape[1]), F32)],
        compiler_params=_params(("arbitrary", "arbitrary")),
        name="mix_ffn",
    )(h_a, zr, zr, zr, x, mod, *mix_w, g, w_gate, w_up, w_down, fin)


def _to_slabs(ref, x):
    rows, d = x.shape
    per = d // LANES
    for k in range(per):
        ref[pl.ds(k, rows, stride=per), :] = x[:, k * LANES:(k + 1) * LANES]


def _from_slabs(ref, rows):
    per = ref.shape[0] // rows
    return jnp.concatenate([ref[pl.ds(k, rows, stride=per), :] for k in range(per)], axis=-1)


def _router_kernel(x_ref, g_ref, mod_ref, rw_ref, rb_ref, h_ref, info_ref, cnt_ref, carry_ref, *, n_experts, cap):
    first = (pl.program_id(0) == 0) & (pl.program_id(1) == 0)

    @pl.when(first)
    def _():
        carry_ref[...] = jnp.zeros_like(carry_ref)

    h = _norm_mod(x_ref[...], g_ref[...], mod_ref[3], mod_ref[4])
    _to_slabs(h_ref, h)
    tm = h.shape[0]
    lane = lax.broadcasted_iota(jnp.int32, (tm, LANES), 1)
    logits = jnp.where(lane < n_experts, _dot_split(h, rw_ref[...]) + rb_ref[...], -jnp.inf)
    v1 = jnp.max(logits, axis=-1, keepdims=True)
    i1 = jnp.min(jnp.where(logits == v1, lane, LANES), axis=-1, keepdims=True)
    rest = jnp.where(lane == i1, -jnp.inf, logits)
    v2 = jnp.max(rest, axis=-1, keepdims=True)
    i2 = jnp.min(jnp.where(rest == v2, lane, LANES), axis=-1, keepdims=True)
    e2 = jnp.exp(v2 - v1)
    w1 = 1.0 / (1.0 + e2)
    w2 = e2 / (1.0 + e2)
    sel1 = lane == i1
    sel2 = lane == i2
    picked = jnp.where(sel1 | sel2, 1.0, 0.0)
    row = lax.broadcasted_iota(jnp.int32, (tm, tm), 0)
    col = lax.broadcasted_iota(jnp.int32, (tm, tm), 1)
    before = jnp.where(col < row, 1.0, 0.0).astype(BF16)
    ex = _dot(before, picked.astype(BF16)) + carry_ref[...]
    pos1 = i1.astype(F32) * cap + jnp.sum(jnp.where(sel1, ex, 0.0), axis=-1, keepdims=True)
    pos2 = i2.astype(F32) * cap + jnp.sum(jnp.where(sel2, ex, 0.0), axis=-1, keepdims=True)
    carry = carry_ref[...] + jnp.sum(picked, axis=0, keepdims=True)
    carry_ref[...] = carry
    cnt_ref[...] = carry
    info_ref[...] = (jnp.where(lane == 0, pos1, 0.0) + jnp.where(lane == 1, pos2, 0.0)
                     + jnp.where(lane == 2, w1, 0.0) + jnp.where(lane == 3, w2, 0.0))


def _router(x, g, mod, router_w, router_b, tm=512):
    b, t, d = x.shape
    n_experts = router_w.shape[1]
    tm = min(tm, t)
    nt = t // tm
    per = d // LANES
    assert t % tm == 0 and d % LANES == 0 and n_experts <= LANES and n_experts * b * t < 2 ** 24
    rw = jnp.zeros((d, LANES), F32).at[:, :n_experts].set(router_w)
    rb = jnp.zeros((1, LANES), F32).at[:, :n_experts].set(router_b[None, :])
    return pl.pallas_call(
        functools.partial(_router_kernel, n_experts=n_experts, cap=b * t),
        grid=(b, t // tm),
        in_specs=[
            pl.BlockSpec((None, tm, d), lambda bi, i: (bi, i, 0)),
            pl.BlockSpec((1, d), lambda bi, i: (0, 0)),
            pl.BlockSpec((None, 6, 1, d), lambda bi, i: (bi, 0, 0, 0)),
            pl.BlockSpec((d, LANES), lambda bi, i: (0, 0)),
            pl.BlockSpec((1, LANES), lambda bi, i: (0, 0)),
        ],
        out_specs=[
            pl.BlockSpec((tm * per, LANES), lambda bi, i: (bi * nt + i, 0)),
            pl.BlockSpec((None, tm, LANES), lambda bi, i: (bi, i, 0)),
            pl.BlockSpec((1, LANES), lambda bi, i: (0, 0)),
        ],
        out_shape=[jax.ShapeDtypeStruct((b * t * per, LANES), F32), jax.ShapeDtypeStruct((b, t, LANES), F32),
                   jax.ShapeDtypeStruct((1, LANES), F32)],
        scratch_shapes=[pltpu.VMEM((1, LANES), F32)],
        compiler_params=_params(("arbitrary", "arbitrary")),
        name="router",
    )(x, g, mod, rw, rb)


def _token_copy(src, src_tok, dst, dst_tok, sem, per):
    return pltpu.make_async_copy(src.at[pl.ds(pl.multiple_of(src_tok * per, per), per)],
                                 dst.at[pl.ds(pl.multiple_of(dst_tok * per, per), per)], sem)


def _scatter_kernel(pos_ref, h_ref, hs_hbm, sem, *, per):
    rows = h_ref.shape[0] // per
    base = pl.program_id(0) * rows

    def copies(r):
        t = base + r
        return (_token_copy(h_ref, r, hs_hbm, pos_ref[TOP_K * t], sem, per),
                _token_copy(h_ref, r, hs_hbm, pos_ref[TOP_K * t + 1], sem, per))

    def start(r, carry):
        for slot, cp in enumerate(copies(r)):
            cp.start(priority=slot)
        return carry

    def wait(r, carry):
        for cp in copies(r):
            cp.wait()
        return carry

    lax.fori_loop(0, rows, start, 0, unroll=8)
    lax.fori_loop(0, rows, wait, 0, unroll=8)


def _scatter_rows(pos, h, n, n_slots_out, rows=512):
    per = h.shape[0] // n
    rows = min(rows, n)
    assert n % rows == 0
    return pl.pallas_call(
        functools.partial(_scatter_kernel, per=per),
        grid_spec=pltpu.PrefetchScalarGridSpec(
            num_scalar_prefetch=1,
            grid=(n // rows,),
            in_specs=[pl.BlockSpec((rows * per, LANES), lambda i, pos: (i, 0))],
            out_specs=pl.BlockSpec(memory_space=pl.ANY),
            scratch_shapes=[pltpu.SemaphoreType.DMA],
        ),
        out_shape=jax.ShapeDtypeStruct((n_slots_out * per, LANES), h.dtype),
        compiler_params=_params(("arbitrary",)),
        name="moe_scatter",
    )(pos, h)


def _moe_ffn_kernel(te_ref, tb_ref, tv_ref, hs_ref, wg_ref, wu_ref, wd_ref, y_ref, xb_ref, acc_ref):
    i = pl.program_id(0)
    j = pl.program_id(1)
    valid = tv_ref[i]

    @pl.when(valid > 0)
    def _():
        @pl.when(j == 0)
        def _():
            row = lax.broadcasted_iota(jnp.int32, (xb_ref.shape[0], 1), 0)
            xb_ref[...] = jnp.where(row < valid, _from_slabs(hs_ref, xb_ref.shape[0]), 0.0).astype(BF16)
            acc_ref[...] = jnp.zeros_like(acc_ref)

        tm = xb_ref.shape[0]
        step = tm // MOE_TILE_PARTS
        for part in range(1, MOE_TILE_PARTS + 1):
            rows = part * step

            @pl.when((valid > rows - step) & (valid <= rows))
            def _(rows=rows):
                h = xb_ref[0:rows, :]
                gate = _dot(h, wg_ref[...].astype(BF16))
                act = (gate * _sigmoid(gate) * _dot(h, wu_ref[...].astype(BF16))).astype(BF16)
                acc_ref[0:rows, :] += _dot(act, wd_ref[...].astype(BF16))

        @pl.when(j == pl.num_programs(1) - 1)
        def _():
            _to_slabs(y_ref, acc_ref[...])


def _moe_ffn(tile_e, tile_blk, tile_valid, hs, w_gate, w_up, w_down, tm, tf=512):
    n_experts, d, ff = w_gate.shape
    per = d // LANES
    tf = min(tf, ff)
    nf = ff // tf
    assert ff % tf == 0
    n_tiles = tile_e.shape[0]
    jj = lambda i, j, tv: jnp.where(tv[i] > 0, j, nf - 1)
    return pl.pallas_call(
        _moe_ffn_kernel,
        grid_spec=pltpu.PrefetchScalarGridSpec(
            num_scalar_prefetch=3,
            grid=(n_tiles, nf),
            in_specs=[
                pl.BlockSpec((tm * per, LANES), lambda i, j, te, tb, tv: (tb[i], 0)),
                pl.BlockSpec((None, d, tf), lambda i, j, te, tb, tv: (te[i], 0, jj(i, j, tv))),
                pl.BlockSpec((None, d, tf), lambda i, j, te, tb, tv: (te[i], 0, jj(i, j, tv))),
                pl.BlockSpec((None, tf, d), lambda i, j, te, tb, tv: (te[i], jj(i, j, tv), 0)),
            ],
            out_specs=pl.BlockSpec((tm * per, LANES), lambda i, j, te, tb, tv: (tb[i], 0)),
            scratch_shapes=[pltpu.VMEM((tm, d), BF16), pltpu.VMEM((tm, d), F32)],
        ),
        out_shape=jax.ShapeDtypeStruct(hs.shape, F32),
        compiler_params=_params(("arbitrary", "arbitrary")),
        name="moe_ffn",
    )(tile_e, tile_blk, tile_valid, hs, w_gate, w_up, w_down)


def _combine_kernel(pos_ref, x_ref, info_ref, mod_ref, fin_ref, y_hbm, out_ref, ybuf, sem, *, final_norm):
    rows = x_ref.shape[0]
    base = pl.program_id(0) * rows

    per = ybuf.shape[1] // rows

    def copies(r):
        t = base + r
        return (_token_copy(y_hbm, pos_ref[TOP_K * t], ybuf.at[0], r, sem, per),
                _token_copy(y_hbm, pos_ref[TOP_K * t + 1], ybuf.at[1], r, sem, per))

    def start(r, carry):
        for slot, cp in enumerate(copies(r)):
            cp.start(priority=slot)
        return carry

    def wait(r, carry):
        for cp in copies(r):
            cp.wait()
        return carry

    lax.fori_loop(0, rows, start, 0, unroll=8)
    lax.fori_loop(0, rows, wait, 0, unroll=8)
    info = info_ref[...]
    f = info[:, 2:3] * _from_slabs(ybuf.at[0], rows) + info[:, 3:4] * _from_slabs(ybuf.at[1], rows)
    y = x_ref[...] + mod_ref[5] * f
    out_ref[...] = _rms(y, fin_ref[...]) if final_norm else y


def _combine(pos, x, info, mod, fin, y, final_norm, rows=256):
    b, t, d = x.shape
    n = b * t
    rows = min(rows, t)
    assert t % rows == 0
    per_b = t // rows
    return pl.pallas_call(
        functools.partial(_combine_kernel, final_norm=final_norm),
        grid_spec=pltpu.PrefetchScalarGridSpec(
            num_scalar_prefetch=1,
            grid=(n // rows,),
            in_specs=[
                pl.BlockSpec((rows, d), lambda i, pos: (i, 0)),
                pl.BlockSpec((rows, LANES), lambda i, pos: (i, 0)),
                pl.BlockSpec((None, 6, 1, d), lambda i, pos: (i // per_b, 0, 0, 0)),
                pl.BlockSpec((1, d), lambda i, pos: (0, 0)),
                pl.BlockSpec(memory_space=pl.ANY),
            ],
            out_specs=pl.BlockSpec((rows, d), lambda i, pos: (i, 0)),
            scratch_shapes=[pltpu.VMEM((TOP_K, rows * (d // LANES), LANES), F32), pltpu.SemaphoreType.DMA],
        ),
        out_shape=jax.ShapeDtypeStruct((n, d), F32),
        compiler_params=_params(("arbitrary",)),
        name="moe_combine",
    )(pos, x.reshape(n, d), info.reshape(n, LANES), mod, fin, y).reshape(b, t, d)


def _tile_tables(counts, n_experts, cap, tm, n_tiles):
    counts = counts.astype(jnp.int32)
    tiles_per = (counts + tm - 1) // tm
    ends = jnp.cumsum(tiles_per)
    used = ends[-1]
    i = jnp.minimum(jnp.arange(n_tiles, dtype=jnp.int32), used - 1)
    e = jnp.sum((i[:, None] >= ends[None, :]).astype(jnp.int32), axis=1)
    k = i - (ends - tiles_per)[e]
    valid = jnp.where(jnp.arange(n_tiles) < used, jnp.minimum(counts[e] - k * tm, tm), 0)
    return e, e * (cap // tm) + k, valid.astype(jnp.int32)


def _moe(x, h, info, counts, mod, w_gate, w_up, w_down, fin, final_norm, tm=1024):
    b, t, d = x.shape
    n = b * t
    n_experts = w_gate.shape[0]
    tm = min(tm, n)
    assert n % tm == 0 and tm % (MOE_TILE_PARTS * 2 * SUBLANES) == 0
    pos = info[:, :, :TOP_K].astype(jnp.int32).reshape(n * TOP_K)
    hs = _scatter_rows(pos, h, n, n_experts * n)
    n_tiles = TOP_K * n // tm + n_experts
    tile_e, tile_blk, tile_valid = _tile_tables(counts[0, :n_experts], n_experts, n, tm, n_tiles)
    y = _moe_ffn(tile_e, tile_blk, tile_valid, hs, w_gate, w_up, w_down, tm)
    return _combine(pos, x, info, mod, fin, y, final_norm)


def kernel(x, c, norm_mix, norm_ffn, w_ada, b_ada, w_in, conv_w, i_bias, f_bias, head_gain, pool_w, pool_scale,
           proj_a, proj_b, w_out, ffn_w_gate, ffn_w_up, ffn_w_down, router_w, router_b, moe_w_gate, moe_w_up,
           moe_w_down, final_norm):
    depth = w_in.shape[0]
    b, t, d = x.shape
    heads = i_bias.shape[1]
    width = head_gain.shape[1]
    p = pool_scale.shape[1]
    ng = 2 * heads
    qkvo = 4 * width

    mod_all = _adaln(c, w_ada, b_ada).reshape(depth, b, 6, 1, d)
    fin = final_norm.reshape(1, d)


    for l in range(depth):
        mod = mod_all[l]
        w_rest = jnp.concatenate([w_in[l, :, qkvo + ng + p:], w_in[l, :, qkvo + ng:qkvo + ng + p]], axis=1).astype(BF16)
        w_gates = w_in[l, :, qkvo:qkvo + ng]
        w_if = jnp.zeros((d, LANES), BF16).at[:, :ng].set(w_gates.astype(BF16))
        w_ift = w_gates.T.astype(BF16)
        bias = jnp.concatenate([i_bias[l], f_bias[l]])
        bias_col = jnp.zeros((1, LANES), F32).at[0, :ng].set(bias)
        bias_row = bias.reshape(ng, 1)

        w_qkvo = w_in[l, :, :qkvo].astype(BF16)[None]
        h_a, zr = _mixin(x, norm_mix[l].reshape(1, d), mod, w_qkvo, 0, w_rest, w_if, w_ift, conv_w[l], bias_col,
                         bias_row, head_gain[l].reshape(1, width), width, heads)
        h_a = h_a.reshape(b, t, width)
        zr = zr.reshape(b, t, -1)
        mix_w = (pool_w[l].astype(BF16), pool_scale[l].reshape(1, p), proj_a[l].astype(BF16), proj_b[l].astype(BF16),
                 w_out[l].astype(BF16))

        last = l == depth - 1
        j = l // 2
        g_ffn = norm_ffn[l].reshape(1, d)
        if l % 2 == 0:
            x = _mix_ffn(h_a, zr, x, mod, mix_w, g_ffn, ffn_w_gate[j].astype(BF16), ffn_w_up[j].astype(BF16),
                         ffn_w_down[j].astype(BF16), fin, last)
        else:
            x = _mixout(h_a, zr, x, mod, mix_w)
            h, info, counts = _router(x, g_ffn, mod, router_w[j], router_b[j])
            x = _moe(x, h, info, counts, mod, moe_w_gate[j], moe_w_up[j], moe_w_down[j], fin, last)
    return x
```

```python
import functools

import jax
import jax.numpy as jnp
from jax import lax
from jax.experimental import pallas as pl
from jax.experimental.pallas import tpu as pltpu

F32 = jnp.float32
BF16 = jnp.bfloat16

EPS = 1e-6
MLSTM_CHUNK = 256
POOL_WINDOWS = (2, 4, 8, 16)
TOP_K = 2
MOE_TILE_PARTS = 4
LANES = 128
SUBLANES = 8
VMEM_LIMIT = 56 * 1024 * 1024

_NT = (((1,), (1,)), ((), ()))
_TN = (((0,), (0,)), ((), ()))


def _params(sem):
    return pltpu.CompilerParams(dimension_semantics=sem, vmem_limit_bytes=VMEM_LIMIT)


def _sigmoid(x):
    return 1.0 / (1.0 + jnp.exp(-x))


def _log_sigmoid(x):
    return jnp.minimum(x, 0.0) - jnp.log(1.0 + jnp.exp(-jnp.abs(x)))


def _rms(x, g):
    return x * lax.rsqrt(jnp.mean(x * x, axis=-1, keepdims=True) + EPS) * g


def _norm_mod(x, g, shift, scale):
    return _rms(x, g) * (1.0 + scale) + shift


def _dot(a, b):
    return jnp.dot(a, b, preferred_element_type=F32)


def _split3(x):
    p0 = x.astype(BF16)
    r0 = x - p0.astype(F32)
    p1 = r0.astype(BF16)
    p2 = (r0 - p1.astype(F32)).astype(BF16)
    return p0, p1, p2


def _dot_split(a, b):
    a_hi = a.astype(BF16)
    a_lo = (a - a_hi.astype(F32)).astype(BF16)
    b_hi = b.astype(BF16)
    b_lo = (b - b_hi.astype(F32)).astype(BF16)
    return _dot(a_hi, b_hi) + (_dot(a_hi, b_lo) + _dot(a_lo, b_hi))


def _adaln_kernel(c_ref, w_ref, b_ref, o_ref):
    c = c_ref[...]
    o_ref[...] = _dot_split(c * _sigmoid(c), w_ref[...]) + b_ref[...]


def _adaln(c, w_ada, b_ada, tn=1536):
    depth, d, n = w_ada.shape
    b = c.shape[0]
    assert n % tn == 0
    return pl.pallas_call(
        _adaln_kernel,
        grid=(depth, n // tn),
        in_specs=[
            pl.BlockSpec((b, d), lambda l, j: (0, 0)),
            pl.BlockSpec((None, d, tn), lambda l, j: (l, 0, j)),
            pl.BlockSpec((None, 1, tn), lambda l, j: (l, 0, j)),
        ],
        out_specs=pl.BlockSpec((None, b, tn), lambda l, j: (l, 0, j)),
        out_shape=jax.ShapeDtypeStruct((depth, b, n), F32),
        compiler_params=_params(("arbitrary", "arbitrary")),
        name="adaln",
    )(c, w_ada, b_ada.reshape(depth, 1, n))


def _mlstm_chunk(zq, gc, gr, convw_ref, gain_ref, out_ref, r0, cbuf, c_st, n_st, m_st, *, heads, fresh):
    L = MLSTM_CHUNK
    W = out_ref.shape[1]
    dh = W // heads
    taps = convw_ref.shape[0]
    halo = SUBLANES
    rows = slice(r0, r0 + L)

    def carried(x):
        return x if fresh is None else jnp.where(fresh, 0.0, x)

    def conv_silu(col0, slab0, scale):
        outs = []
        for cb in range(W // LANES):
            slab = slab0 + cb
            cs = slice(slab * LANES, (slab + 1) * LANES)
            if fresh is not None:
                cbuf[slab, 0:halo, :] = carried(cbuf[slab, 0:halo, :])
            cbuf[slab, halo:halo + L, :] = zq[rows, col0 + cb * LANES:col0 + (cb + 1) * LANES].astype(F32)
            acc = convw_ref[taps - 1:taps, cs] * cbuf[slab, halo:halo + L, :]
            for j in range(taps - 1):
                off = halo - (taps - 1) + j
                acc = acc + convw_ref[j:j + 1, cs] * cbuf[slab, off:off + L, :]
            cbuf[slab, 0:halo, :] = cbuf[slab, L:L + halo, :]
            outs.append(acc * _sigmoid(acc) * scale if scale != 1.0 else acc * _sigmoid(acc))
        return outs

    q_slabs = conv_silu(0, 0, dh ** -0.5)
    k_slabs = conv_silu(W, W // LANES, 1.0)
    per_head = dh // LANES

    row = lax.broadcasted_iota(jnp.int32, (L, L), 0)
    col = lax.broadcasted_iota(jnp.int32, (L, L), 1)
    causal = row >= col
    tri_low = jnp.where(causal, 1.0, 0.0).astype(BF16)
    tri_up = jnp.where(row <= col, 1.0, 0.0).astype(BF16)
    b_cols = sum(_dot(tri_low, part) for part in _split3(_log_sigmoid(gc)))
    b_rows = sum(_dot(part, tri_up) for part in _split3(_log_sigmoid(gr)))

    for h in range(heads):
        hs = slice(h * dh, (h + 1) * dh)
        q = jnp.concatenate(q_slabs[h * per_head:(h + 1) * per_head], axis=-1)
        k = jnp.concatenate(k_slabs[h * per_head:(h + 1) * per_head], axis=-1)
        vb = zq[rows, 2 * W + h * dh:2 * W + (h + 1) * dh]
        v = vb.astype(F32)
        qb = q.astype(BF16)
        kb = k.astype(BF16)
        li_c = gc[:, h:h + 1]
        b_c = b_cols[:, heads + h:heads + h + 1]
        li_r = gr[h:h + 1, :]
        b_r = b_rows[heads + h:heads + h + 1, :]
        b_tot = b_r[:, L - 1:L]
        c_prev = carried(c_st[h])
        n_prev = carried(n_st[h])
        m_prev = carried(m_st[h][:, 0:1])

        d = jnp.where(causal, b_c - b_r + li_r, -jnp.inf)
        inter_log = b_c + m_prev
        m_comb = jnp.maximum(inter_log, jnp.max(d, axis=-1, keepdims=True))
        s = lax.dot_general(qb, kb, _NT, preferred_element_type=F32) * jnp.exp(d - m_comb)
        w_inter = jnp.exp(inter_log - m_comb)
        num = _dot(s.astype(BF16), vb) + w_inter * lax.dot_general(
            qb, c_prev.astype(BF16), _NT, preferred_element_type=F32)
        den = jnp.sum(s, axis=-1, keepdims=True) + w_inter * jnp.sum(q * n_prev, axis=-1, keepdims=True)
        den = jnp.maximum(jnp.abs(den), jnp.exp(-m_comb))
        hh = num / den
        hh = hh * lax.rsqrt(jnp.mean(hh * hh, axis=-1, keepdims=True) + EPS)
        gate = _sigmoid(zq[rows, 3 * W + h * dh:3 * W + (h + 1) * dh].astype(F32))
        out_ref[rows, hs] = (hh * gain_ref[:, hs] * gate).astype(out_ref.dtype)

        a = b_tot - b_c + li_c
        m_loc = jnp.max(a, axis=0, keepdims=True)
        w = jnp.exp(a - m_loc)
        c_loc = lax.dot_general((w * v).astype(BF16), kb, _TN, preferred_element_type=F32)
        n_loc = jnp.sum(w * k, axis=0, keepdims=True)
        m_new = jnp.maximum(b_tot + m_prev, m_loc)
        s_old = jnp.exp(b_tot + m_prev - m_new)
        s_loc = jnp.exp(m_loc - m_new)
        c_st[h] = s_old * c_prev + s_loc * c_loc
        n_st[h] = s_old * n_prev + s_loc * n_loc
        m_st[h] = jnp.broadcast_to(m_new, (1, LANES))


def _mixin_kernel(x_ref, g_ref, mod_ref, wq_ref, wr_ref, wif_ref, wift_ref, convw_ref, bcol_ref, brow_ref, gain_ref,
                  ha_ref, zr_ref, h_ref, zring, gcring, grring, cbuf, c_st, n_st, m_st,
                  *, heads, tiles_per_seq, col_chunk):
    s = pl.program_id(0)
    tm = x_ref.shape[0]
    W = ha_ref.shape[1]
    L = MLSTM_CHUNK
    n_chunks = tm // L
    qkvo = 4 * W
    slot_a = lax.rem(s, 2)
    slot_b = 1 - slot_a

    @pl.when(s == 0)
    def _():
        zring[1] = jnp.zeros(zring.shape[1:], zring.dtype)
        gcring[1] = jnp.zeros(gcring.shape[1:], F32)
        grring[1] = jnp.zeros(grring.shape[1:], F32)
        cbuf[...] = jnp.zeros_like(cbuf)
        c_st[...] = jnp.zeros_like(c_st)
        n_st[...] = jnp.zeros_like(n_st)
        m_st[...] = jnp.zeros_like(m_st)

    h = _norm_mod(x_ref[...], g_ref[...], mod_ref[0], mod_ref[1]).astype(BF16)
    h_ref[...] = h
    gcring[slot_a] = _dot(h, wif_ref[...])
    grring[slot_a] = lax.dot_general(wift_ref[...], h, _NT, preferred_element_type=F32)

    fresh = lax.rem(s + tiles_per_seq - 1, tiles_per_seq) == 0
    zq = zring.at[slot_b]
    col_starts = list(range(0, qkvo + wr_ref.shape[1], col_chunk))
    share = -(-len(col_starts) // n_chunks)
    for c in range(n_chunks):
        for c0 in col_starts[c * share:(c + 1) * share]:
            if c0 < qkvo:
                zring[slot_a, :, c0:c0 + col_chunk] = _dot(h_ref[...], wq_ref[:, c0:c0 + col_chunk]).astype(BF16)
            else:
                cr = slice(c0 - qkvo, c0 - qkvo + col_chunk)
                zr_ref[:, cr] = _dot(h_ref[...], wr_ref[:, cr]).astype(BF16)
        r0 = c * L
        gc = gcring[slot_b, r0:r0 + L, :] + bcol_ref[...]
        gr = grring[slot_b, :, r0:r0 + L] + brow_ref[...]
        _mlstm_chunk(zq, gc, gr, convw_ref, gain_ref, ha_ref, r0, cbuf, c_st, n_st, m_st,
                     heads=heads, fresh=fresh if c == 0 else None)


def _mixin(x, g, mod, w_qkvo, layer, w_rest, w_if, w_ift, conv_w, bias_col, bias_row, head_gain, width, heads, tm=512,
           col_chunk=256):
    b, t, d = x.shape
    n = b * t
    ncols = w_qkvo.shape[2] + w_rest.shape[1]
    ng = w_ift.shape[0]
    dh = width // heads
    tm = min(tm, t)
    nt = t // tm
    n_tiles = b * nt
    qkvo = 4 * width
    assert t % tm == 0 and tm % MLSTM_CHUNK == 0 and conv_w.shape[0] - 1 <= SUBLANES and dh % LANES == 0
    assert ncols % col_chunk == 0 and qkvo % col_chunk == 0 and w_qkvo.shape[2] == qkvo
    cur = lambda s: jnp.minimum(s, n_tiles - 1)
    prev = lambda s: jnp.maximum(s - 1, 0)
    const = lambda shape: pl.BlockSpec(shape, lambda s: (0,) * len(shape))
    return pl.pallas_call(
        functools.partial(_mixin_kernel, heads=heads, tiles_per_seq=nt, col_chunk=col_chunk),
        grid=(n_tiles + 1,),
        in_specs=[
            pl.BlockSpec((tm, d), lambda s: (cur(s), 0)),
            const((1, d)),
            pl.BlockSpec((None, 6, 1, d), lambda s: (cur(s) // nt, 0, 0, 0)),
            pl.BlockSpec((None, d, qkvo), lambda s: (layer, 0, 0), pipeline_mode=pl.Buffered(1)),
            pl.BlockSpec((d, ncols - qkvo), lambda s: (0, 0), pipeline_mode=pl.Buffered(1)),
            const((d, LANES)), const((ng, d)), const(conv_w.shape), const((1, LANES)), const((ng, 1)), const((1, width)),
        ],
        out_specs=[
            pl.BlockSpec((tm, width), lambda s: (prev(s), 0)),
            pl.BlockSpec((tm, ncols - qkvo), lambda s: (cur(s), 0)),
        ],
        out_shape=[
            jax.ShapeDtypeStruct((n, width), BF16),
            jax.ShapeDtypeStruct((n, ncols - qkvo), BF16),
        ],
        scratch_shapes=[
            pltpu.VMEM((tm, d), BF16),
            pltpu.VMEM((2, tm, qkvo), BF16),
            pltpu.VMEM((2, tm, LANES), F32),
            pltpu.VMEM((2, ng, tm), F32),
            pltpu.VMEM((2 * width // LANES, SUBLANES + MLSTM_CHUNK, LANES), F32),
            pltpu.VMEM((heads, dh, dh), F32),
            pltpu.VMEM((heads, 1, dh), F32),
            pltpu.VMEM((heads, 1, LANES), F32),
        ],
        compiler_params=_params(("arbitrary",)),
        name="mixin",
    )(x.reshape(n, d), g, mod, w_qkvo, w_rest, w_if, w_ift, conv_w, bias_col, bias_row, head_gain)


def _mixout_tile(ha_ref, u_ref, ga_ref, gb_ref, x_ref, mod_ref, poolw_ref, pscale_ref, pa_ref, pb_ref, wo_ref, ubuf,
                 windows):
    tm = x_ref.shape[0]
    gd = poolw_ref.shape[1]
    halo = max(windows)
    i = pl.program_id(1)

    @pl.when(i == 0)
    def _():
        ubuf[:, 0:halo, :] = jnp.zeros((ubuf.shape[0], halo, gd), F32)

    tpos = i * tm + lax.broadcasted_iota(jnp.int32, (tm, 1), 0)
    parts = []
    for g, win in enumerate(windows):
        cur = u_ref[:, g * gd:(g + 1) * gd].astype(F32)
        ubuf[g, halo:halo + tm, :] = cur
        wsum = cur
        for j in range(1, win):
            wsum = wsum + ubuf[g, halo - j:halo - j + tm, :]
        ubuf[g, 0:halo, :] = ubuf[g, tm:tm + halo, :]
        count = jnp.minimum(tpos + 1, win).astype(F32)
        pooled = wsum / count - cur
        parts.append(_dot(pooled.astype(BF16), poolw_ref[g]))
    hb = (jnp.concatenate(parts, axis=-1) * pscale_ref[...]).astype(BF16)

    pa = _dot(ha_ref[...], pa_ref[...])
    pb = _dot(hb, pb_ref[...])
    merged = _sigmoid(ga_ref[...].astype(F32)) * pa + _sigmoid(gb_ref[...].astype(F32)) * pb
    y = _dot(merged.astype(BF16), wo_ref[...])
    return x_ref[...] + mod_ref[2] * y


def _mixout_specs(h_a, zr, x, pool_w, pool_scale, proj_a, proj_b, w_out, tm):
    b, t, d = x.shape
    p = pool_scale.shape[1]
    width = h_a.shape[2]
    u_blk = 2 * d // p
    assert t % tm == 0 and (2 * d) % p == 0
    const = lambda shape: pl.BlockSpec(shape, lambda bi, i: (0,) * len(shape))
    specs = [
        pl.BlockSpec((None, tm, width), lambda bi, i: (bi, i, 0)),
        pl.BlockSpec((None, tm, p), lambda bi, i: (bi, i, u_blk)),
        pl.BlockSpec((None, tm, d), lambda bi, i: (bi, i, 0)),
        pl.BlockSpec((None, tm, d), lambda bi, i: (bi, i, 1)),
        pl.BlockSpec((None, tm, d), lambda bi, i: (bi, i, 0)),
        pl.BlockSpec((None, 6, 1, d), lambda bi, i: (bi, 0, 0, 0)),
        const(pool_w.shape), const(pool_scale.shape), const(proj_a.shape), const(proj_b.shape), const(w_out.shape),
    ]
    return specs, const


def _mixout_kernel(ha_ref, u_ref, ga_ref, gb_ref, x_ref, mod_ref, poolw_ref, pscale_ref, pa_ref, pb_ref, wo_ref,
                   out_ref, ubuf, *, windows):
    out_ref[...] = _mixout_tile(ha_ref, u_ref, ga_ref, gb_ref, x_ref, mod_ref, poolw_ref, pscale_ref, pa_ref, pb_ref,
                                wo_ref, ubuf, windows)


def _mixout(h_a, zr, x, mod, mix_w, tm=512):
    b, t, d = x.shape
    tm = min(tm, t)
    specs, _ = _mixout_specs(h_a, zr, x, *mix_w, tm)
    pool_w = mix_w[0]
    return pl.pallas_call(
        functools.partial(_mixout_kernel, windows=POOL_WINDOWS),
        grid=(b, t // tm),
        in_specs=specs,
        out_specs=pl.BlockSpec((None, tm, d), lambda bi, i: (bi, i, 0)),
        out_shape=jax.ShapeDtypeStruct((b, t, d), F32),
        scratch_shapes=[pltpu.VMEM((pool_w.shape[0], max(POOL_WINDOWS) + tm, pool_w.shape[1]), F32)],
        compiler_params=_params(("arbitrary", "arbitrary")),
        name="mixout",
    )(h_a, zr, zr, zr, x, mod, *mix_w)


def _mix_ffn_kernel(ha_ref, u_ref, ga_ref, gb_ref, x_ref, mod_ref, poolw_ref, pscale_ref, pa_ref, pb_ref, wo_ref,
                    g_ref, wg_ref, wu_ref, wd_ref, fin_ref, out_ref, ubuf, *, windows, final_norm, ff_chunk):
    x = _mixout_tile(ha_ref, u_ref, ga_ref, gb_ref, x_ref, mod_ref, poolw_ref, pscale_ref, pa_ref, pb_ref, wo_ref,
                     ubuf, windows)
    h = _norm_mod(x, g_ref[...], mod_ref[3], mod_ref[4]).astype(BF16)
    acc = None
    for c0 in range(0, wg_ref.shape[1], ff_chunk):
        gate = _dot(h, wg_ref[:, c0:c0 + ff_chunk])
        act = (gate * _sigmoid(gate) * _dot(h, wu_ref[:, c0:c0 + ff_chunk])).astype(BF16)
        part = _dot(act, wd_ref[c0:c0 + ff_chunk, :])
        acc = part if acc is None else acc + part
    y = x + mod_ref[5] * acc
    out_ref[...] = _rms(y, fin_ref[...]) if final_norm else y


def _mix_ffn(h_a, zr, x, mod, mix_w, g, w_gate, w_up, w_down, fin, final_norm, tm=512, ff_chunk=1408):
    b, t, d = x.shape
    ff = w_gate.shape[1]
    tm = min(tm, t)
    ff_chunk = min(ff_chunk, ff)
    assert ff % ff_chunk == 0
    specs, const = _mixout_specs(h_a, zr, x, *mix_w, tm)
    pool_w = mix_w[0]
    return pl.pallas_call(
        functools.partial(_mix_ffn_kernel, windows=POOL_WINDOWS, final_norm=final_norm, ff_chunk=ff_chunk),
        grid=(b, t // tm),
        in_specs=specs + [const((1, d)), const((d, ff)), const((d, ff)), const((ff, d)), const((1, d))],
        out_specs=pl.BlockSpec((None, tm, d), lambda bi, i: (bi, i, 0)),
        out_shape=jax.ShapeDtypeStruct((b, t, d), F32),
        scratch_shapes=[pltpu.VMEM((pool_w.shape[0], max(POOL_WINDOWS) + tm, pool_w.shape[1]), F32)],
        compiler_params=_params(("arbitrary", "arbitrary")),
        name="mix_ffn",
    )(h_a, zr, zr, zr, x, mod, *mix_w, g, w_gate, w_up, w_down, fin)


def _to_slabs(ref, x):
    rows, d = x.shape
    per = d // LANES
    for k in range(per):
        ref[pl.ds(k, rows, stride=per), :] = x[:, k * LANES:(k + 1) * LANES]


def _from_slabs(ref, rows):
    per = ref.shape[0] // rows
    return jnp.concatenate([ref[pl.ds(k, rows, stride=per), :] for k in range(per)], axis=-1)


def _router_kernel(x_ref, g_ref, mod_ref, rw_ref, rb_ref, h_ref, info_ref, cnt_ref, carry_ref, *, n_experts, cap):
    first = (pl.program_id(0) == 0) & (pl.program_id(1) == 0)

    @pl.when(first)
    def _():
        carry_ref[...] = jnp.zeros_like(carry_ref)

    h = _norm_mod(x_ref[...], g_ref[...], mod_ref[3], mod_ref[4])
    _to_slabs(h_ref, h)
    tm = h.shape[0]
    lane = lax.broadcasted_iota(jnp.int32, (tm, LANES), 1)
    logits = jnp.where(lane < n_experts, _dot_split(h, rw_ref[...]) + rb_ref[...], -jnp.inf)
    v1 = jnp.max(logits, axis=-1, keepdims=True)
    i1 = jnp.min(jnp.where(logits == v1, lane, LANES), axis=-1, keepdims=True)
    rest = jnp.where(lane == i1, -jnp.inf, logits)
    v2 = jnp.max(rest, axis=-1, keepdims=True)
    i2 = jnp.min(jnp.where(rest == v2, lane, LANES), axis=-1, keepdims=True)
    e2 = jnp.exp(v2 - v1)
    w1 = 1.0 / (1.0 + e2)
    w2 = e2 / (1.0 + e2)
    sel1 = lane == i1
    sel2 = lane == i2
    picked = jnp.where(sel1 | sel2, 1.0, 0.0)
    row = lax.broadcasted_iota(jnp.int32, (tm, tm), 0)
    col = lax.broadcasted_iota(jnp.int32, (tm, tm), 1)
    before = jnp.where(col < row, 1.0, 0.0).astype(BF16)
    ex = _dot(before, picked.astype(BF16)) + carry_ref[...]
    pos1 = i1.astype(F32) * cap + jnp.sum(jnp.where(sel1, ex, 0.0), axis=-1, keepdims=True)
    pos2 = i2.astype(F32) * cap + jnp.sum(jnp.where(sel2, ex, 0.0), axis=-1, keepdims=True)
    carry = carry_ref[...] + jnp.sum(picked, axis=0, keepdims=True)
    carry_ref[...] = carry
    cnt_ref[...] = carry
    info_ref[...] = (jnp.where(lane == 0, pos1, 0.0) + jnp.where(lane == 1, pos2, 0.0)
                     + jnp.where(lane == 2, w1, 0.0) + jnp.where(lane == 3, w2, 0.0))


def _router(x, g, mod, router_w, router_b, tm=512):
    b, t, d = x.shape
    n_experts = router_w.shape[1]
    tm = min(tm, t)
    nt = t // tm
    per = d // LANES
    assert t % tm == 0 and d % LANES == 0 and n_experts <= LANES and n_experts * b * t < 2 ** 24
    rw = jnp.zeros((d, LANES), F32).at[:, :n_experts].set(router_w)
    rb = jnp.zeros((1, LANES), F32).at[:, :n_experts].set(router_b[None, :])
    return pl.pallas_call(
        functools.partial(_router_kernel, n_experts=n_experts, cap=b * t),
        grid=(b, t // tm),
        in_specs=[
            pl.BlockSpec((None, tm, d), lambda bi, i: (bi, i, 0)),
            pl.BlockSpec((1, d), lambda bi, i: (0, 0)),
            pl.BlockSpec((None, 6, 1, d), lambda bi, i: (bi, 0, 0, 0)),
            pl.BlockSpec((d, LANES), lambda bi, i: (0, 0)),
            pl.BlockSpec((1, LANES), lambda bi, i: (0, 0)),
        ],
        out_specs=[
            pl.BlockSpec((tm * per, LANES), lambda bi, i: (bi * nt + i, 0)),
            pl.BlockSpec((None, tm, LANES), lambda bi, i: (bi, i, 0)),
            pl.BlockSpec((1, LANES), lambda bi, i: (0, 0)),
        ],
        out_shape=[jax.ShapeDtypeStruct((b * t * per, LANES), F32), jax.ShapeDtypeStruct((b, t, LANES), F32),
                   jax.ShapeDtypeStruct((1, LANES), F32)],
        scratch_shapes=[pltpu.VMEM((1, LANES), F32)],
        compiler_params=_params(("arbitrary", "arbitrary")),
        name="router",
    )(x, g, mod, rw, rb)


def _token_copy(src, src_tok, dst, dst_tok, sem, per):
    return pltpu.make_async_copy(src.at[pl.ds(pl.multiple_of(src_tok * per, per), per)],
                                 dst.at[pl.ds(pl.multiple_of(dst_tok * per, per), per)], sem)


def _scatter_kernel(pos_ref, h_ref, hs_hbm, sem, *, per):
    rows = h_ref.shape[0] // per
    base = pl.program_id(0) * rows

    def copies(r):
        t = base + r
        return (_token_copy(h_ref, r, hs_hbm, pos_ref[TOP_K * t], sem, per),
                _token_copy(h_ref, r, hs_hbm, pos_ref[TOP_K * t + 1], sem, per))

    def start(r, carry):
        for slot, cp in enumerate(copies(r)):
            cp.start(priority=slot)
        return carry

    def wait(r, carry):
        for cp in copies(r):
            cp.wait()
        return carry

    lax.fori_loop(0, rows, start, 0, unroll=8)
    lax.fori_loop(0, rows, wait, 0, unroll=8)


def _scatter_rows(pos, h, n, n_slots_out, rows=512):
    per = h.shape[0] // n
    rows = min(rows, n)
    assert n % rows == 0
    return pl.pallas_call(
        functools.partial(_scatter_kernel, per=per),
        grid_spec=pltpu.PrefetchScalarGridSpec(
            num_scalar_prefetch=1,
            grid=(n // rows,),
            in_specs=[pl.BlockSpec((rows * per, LANES), lambda i, pos: (i, 0))],
            out_specs=pl.BlockSpec(memory_space=pl.ANY),
            scratch_shapes=[pltpu.SemaphoreType.DMA],
        ),
        out_shape=jax.ShapeDtypeStruct((n_slots_out * per, LANES), h.dtype),
        compiler_params=_params(("arbitrary",)),
        name="moe_scatter",
    )(pos, h)


def _moe_ffn_kernel(te_ref, tb_ref, tv_ref, hs_ref, wg_ref, wu_ref, wd_ref, y_ref, xb_ref, acc_ref):
    i = pl.program_id(0)
    j = pl.program_id(1)
    valid = tv_ref[i]

    @pl.when(valid > 0)
    def _():
        @pl.when(j == 0)
        def _():
            row = lax.broadcasted_iota(jnp.int32, (xb_ref.shape[0], 1), 0)
            xb_ref[...] = jnp.where(row < valid, _from_slabs(hs_ref, xb_ref.shape[0]), 0.0).astype(BF16)
            acc_ref[...] = jnp.zeros_like(acc_ref)

        tm = xb_ref.shape[0]
        step = tm // MOE_TILE_PARTS
        for part in range(1, MOE_TILE_PARTS + 1):
            rows = part * step

            @pl.when((valid > rows - step) & (valid <= rows))
            def _(rows=rows):
                h = xb_ref[0:rows, :]
                gate = _dot(h, wg_ref[...].astype(BF16))
                act = (gate * _sigmoid(gate) * _dot(h, wu_ref[...].astype(BF16))).astype(BF16)
                acc_ref[0:rows, :] += _dot(act, wd_ref[...].astype(BF16))

        @pl.when(j == pl.num_programs(1) - 1)
        def _():
            _to_slabs(y_ref, acc_ref[...])


def _moe_ffn(tile_e, tile_blk, tile_valid, hs, w_gate, w_up, w_down, tm, tf=512):
    n_experts, d, ff = w_gate.shape
    per = d // LANES
    tf = min(tf, ff)
    nf = ff // tf
    assert ff % tf == 0
    n_tiles = tile_e.shape[0]
    jj = lambda i, j, tv: jnp.where(tv[i] > 0, j, nf - 1)
    return pl.pallas_call(
        _moe_ffn_kernel,
        grid_spec=pltpu.PrefetchScalarGridSpec(
            num_scalar_prefetch=3,
            grid=(n_tiles, nf),
            in_specs=[
                pl.BlockSpec((tm * per, LANES), lambda i, j, te, tb, tv: (tb[i], 0)),
                pl.BlockSpec((None, d, tf), lambda i, j, te, tb, tv: (te[i], 0, jj(i, j, tv))),
                pl.BlockSpec((None, d, tf), lambda i, j, te, tb, tv: (te[i], 0, jj(i, j, tv))),
                pl.BlockSpec((None, tf, d), lambda i, j, te, tb, tv: (te[i], jj(i, j, tv), 0)),
            ],
            out_specs=pl.BlockSpec((tm * per, LANES), lambda i, j, te, tb, tv: (tb[i], 0)),
            scratch_shapes=[pltpu.VMEM((tm, d), BF16), pltpu.VMEM((tm, d), F32)],
        ),
        out_shape=jax.ShapeDtypeStruct(hs.shape, F32),
        compiler_params=_params(("arbitrary", "arbitrary")),
        name="moe_ffn",
    )(tile_e, tile_blk, tile_valid, hs, w_gate, w_up, w_down)


def _combine_kernel(pos_ref, x_ref, info_ref, mod_ref, fin_ref, y_hbm, out_ref, ybuf, sems, *, final_norm):
    rows = x_ref.shape[0]
    i = pl.program_id(0)
    per = ybuf.shape[2] // rows
    cur = lax.rem(i, 2)

    def copies(tile, ring, r):
        t = tile * rows + r
        return (_token_copy(y_hbm, pos_ref[TOP_K * t], ybuf.at[ring, 0], r, sems.at[ring], per),
                _token_copy(y_hbm, pos_ref[TOP_K * t + 1], ybuf.at[ring, 1], r, sems.at[ring], per))

    def issue(tile, ring):
        def start(r, carry):
            for slot, cp in enumerate(copies(tile, ring, r)):
                cp.start(priority=slot)
            return carry
        lax.fori_loop(0, rows, start, 0, unroll=8)

    def drain(tile, ring):
        def wait(r, carry):
            for cp in copies(tile, ring, r):
                cp.wait()
            return carry
        lax.fori_loop(0, rows, wait, 0, unroll=8)

    @pl.when(i == 0)
    def _():
        issue(0, 0)

    @pl.when(i + 1 < pl.num_programs(0))
    def _():
        issue(i + 1, 1 - cur)

    drain(i, cur)
    info = info_ref[...]
    f = info[:, 2:3] * _from_slabs(ybuf.at[cur, 0], rows) + info[:, 3:4] * _from_slabs(ybuf.at[cur, 1], rows)
    y = x_ref[...] + mod_ref[5] * f
    out_ref[...] = _rms(y, fin_ref[...]) if final_norm else y


def _combine(pos, x, info, mod, fin, y, final_norm, rows=256):
    b, t, d = x.shape
    n = b * t
    rows = min(rows, t)
    assert t % rows == 0
    per_b = t // rows
    return pl.pallas_call(
        functools.partial(_combine_kernel, final_norm=final_norm),
        grid_spec=pltpu.PrefetchScalarGridSpec(
            num_scalar_prefetch=1,
            grid=(n // rows,),
            in_specs=[
                pl.BlockSpec((rows, d), lambda i, pos: (i, 0)),
                pl.BlockSpec((rows, LANES), lambda i, pos: (i, 0)),
                pl.BlockSpec((None, 6, 1, d), lambda i, pos: (i // per_b, 0, 0, 0)),
                pl.BlockSpec((1, d), lambda i, pos: (0, 0)),
                pl.BlockSpec(memory_space=pl.ANY),
            ],
            out_specs=pl.BlockSpec((rows, d), lambda i, pos: (i, 0)),
            scratch_shapes=[pltpu.VMEM((2, TOP_K, rows * (d // LANES), LANES), F32), pltpu.SemaphoreType.DMA((2,))],
        ),
        out_shape=jax.ShapeDtypeStruct((n, d), F32),
        compiler_params=_params(("arbitrary",)),
        name="moe_combine",
    )(pos, x.reshape(n, d), info.reshape(n, LANES), mod, fin, y).reshape(b, t, d)


def _tile_tables(counts, n_experts, cap, tm, n_tiles):
    counts = counts.astype(jnp.int32)
    tiles_per = (counts + tm - 1) // tm
    ends = jnp.cumsum(tiles_per)
    used = ends[-1]
    i = jnp.minimum(jnp.arange(n_tiles, dtype=jnp.int32), used - 1)
    e = jnp.sum((i[:, None] >= ends[None, :]).astype(jnp.int32), axis=1)
    k = i - (ends - tiles_per)[e]
    valid = jnp.where(jnp.arange(n_tiles) < used, jnp.minimum(counts[e] - k * tm, tm), 0)
    return e, e * (cap // tm) + k, valid.astype(jnp.int32)


def _moe(x, h, info, counts, mod, w_gate, w_up, w_down, fin, final_norm, tm=1024):
    b, t, d = x.shape
    n = b * t
    n_experts = w_gate.shape[0]
    tm = min(tm, n)
    assert n % tm == 0 and tm % (MOE_TILE_PARTS * 2 * SUBLANES) == 0
    pos = info[:, :, :TOP_K].astype(jnp.int32).reshape(n * TOP_K)
    hs = _scatter_rows(pos, h, n, n_experts * n)
    n_tiles = TOP_K * n // tm + n_experts
    tile_e, tile_blk, tile_valid = _tile_tables(counts[0, :n_experts], n_experts, n, tm, n_tiles)
    y = _moe_ffn(tile_e, tile_blk, tile_valid, hs, w_gate, w_up, w_down, tm)
    return _combine(pos, x, info, mod, fin, y, final_norm)


def kernel(x, c, norm_mix, norm_ffn, w_ada, b_ada, w_in, conv_w, i_bias, f_bias, head_gain, pool_w, pool_scale,
           proj_a, proj_b, w_out, ffn_w_gate, ffn_w_up, ffn_w_down, router_w, router_b, moe_w_gate, moe_w_up,
           moe_w_down, final_norm):
    depth = w_in.shape[0]
    b, t, d = x.shape
    heads = i_bias.shape[1]
    width = head_gain.shape[1]
    p = pool_scale.shape[1]
    ng = 2 * heads
    qkvo = 4 * width

    mod_all = _adaln(c, w_ada, b_ada).reshape(depth, b, 6, 1, d)
    fin = final_norm.reshape(1, d)


    for l in range(depth):
        mod = mod_all[l]
        w_rest = jnp.concatenate([w_in[l, :, qkvo + ng + p:], w_in[l, :, qkvo + ng:qkvo + ng + p]], axis=1).astype(BF16)
        w_gates = w_in[l, :, qkvo:qkvo + ng]
        w_if = jnp.zeros((d, LANES), BF16).at[:, :ng].set(w_gates.astype(BF16))
        w_ift = w_gates.T.astype(BF16)
        bias = jnp.concatenate([i_bias[l], f_bias[l]])
        bias_col = jnp.zeros((1, LANES), F32).at[0, :ng].set(bias)
        bias_row = bias.reshape(ng, 1)

        w_qkvo = w_in[l, :, :qkvo].astype(BF16)[None]
        h_a, zr = _mixin(x, norm_mix[l].reshape(1, d), mod, w_qkvo, 0, w_rest, w_if, w_ift, conv_w[l], bias_col,
                         bias_row, head_gain[l].reshape(1, width), width, heads)
        h_a = h_a.reshape(b, t, width)
        zr = zr.reshape(b, t, -1)
        mix_w = (pool_w[l].astype(BF16), pool_scale[l].reshape(1, p), proj_a[l].astype(BF16), proj_b[l].astype(BF16),
                 w_out[l].astype(BF16))

        last = l == depth - 1
        j = l // 2
        g_ffn = norm_ffn[l].reshape(1, d)
        if l % 2 == 0:
            x = _mix_ffn(h_a, zr, x, mod, mix_w, g_ffn, ffn_w_gate[j].astype(BF16), ffn_w_up[j].astype(BF16),
                         ffn_w_down[j].astype(BF16), fin, last)
        else:
            x = _mixout(h_a, zr, x, mod, mix_w)
            h, info, counts = _router(x, g_ffn, mod, router_w[j], router_b[j])
            x = _moe(x, h, info, counts, mod, moe_w_gate[j], moe_w_up[j], moe_w_down[j], fin, last)
    return x
```

```python
import functools

import jax
import jax.numpy as jnp
from jax import lax
from jax.experimental import pallas as pl
from jax.experimental.pallas import tpu as pltpu

F32 = jnp.float32
BF16 = jnp.bfloat16

EPS = 1e-6
MLSTM_CHUNK = 256
POOL_WINDOWS = (2, 4, 8, 16)
TOP_K = 2
MOE_TILE_PARTS = 4
LANES = 128
SUBLANES = 8
VMEM_LIMIT = 56 * 1024 * 1024

_NT = (((1,), (1,)), ((), ()))
_TN = (((0,), (0,)), ((), ()))


def _params(sem):
    return pltpu.CompilerParams(dimension_semantics=sem, vmem_limit_bytes=VMEM_LIMIT)


def _sigmoid(x):
    return 1.0 / (1.0 + jnp.exp(-x))


def _log_sigmoid(x):
    return jnp.minimum(x, 0.0) - jnp.log(1.0 + jnp.exp(-jnp.abs(x)))


def _rms(x, g):
    return x * lax.rsqrt(jnp.mean(x * x, axis=-1, keepdims=True) + EPS) * g


def _norm_mod(x, g, shift, scale):
    return _rms(x, g) * (1.0 + scale) + shift


def _dot(a, b):
    return jnp.dot(a, b, preferred_element_type=F32)


def _split3(x):
    p0 = x.astype(BF16)
    r0 = x - p0.astype(F32)
    p1 = r0.astype(BF16)
    p2 = (r0 - p1.astype(F32)).astype(BF16)
    return p0, p1, p2


def _dot_split(a, b):
    a_hi = a.astype(BF16)
    a_lo = (a - a_hi.astype(F32)).astype(BF16)
    b_hi = b.astype(BF16)
    b_lo = (b - b_hi.astype(F32)).astype(BF16)
    return _dot(a_hi, b_hi) + (_dot(a_hi, b_lo) + _dot(a_lo, b_hi))


def _adaln_kernel(c_ref, w_ref, b_ref, o_ref):
    c = c_ref[...]
    o_ref[...] = _dot_split(c * _sigmoid(c), w_ref[...]) + b_ref[...]


def _adaln(c, w_ada, b_ada, tn=1536):
    depth, d, n = w_ada.shape
    b = c.shape[0]
    assert n % tn == 0
    return pl.pallas_call(
        _adaln_kernel,
        grid=(depth, n // tn),
        in_specs=[
            pl.BlockSpec((b, d), lambda l, j: (0, 0)),
            pl.BlockSpec((None, d, tn), lambda l, j: (l, 0, j)),
            pl.BlockSpec((None, 1, tn), lambda l, j: (l, 0, j)),
        ],
        out_specs=pl.BlockSpec((None, b, tn), lambda l, j: (l, 0, j)),
        out_shape=jax.ShapeDtypeStruct((depth, b, n), F32),
        compiler_params=_params(("arbitrary", "arbitrary")),
        name="adaln",
    )(c, w_ada, b_ada.reshape(depth, 1, n))


def _mlstm_chunk(zq, gc, gr, convw_ref, gain_ref, out_ref, r0, cbuf, c_st, n_st, m_st, *, heads, fresh):
    L = MLSTM_CHUNK
    W = out_ref.shape[1]
    dh = W // heads
    taps = convw_ref.shape[0]
    halo = SUBLANES
    rows = slice(r0, r0 + L)

    def carried(x):
        return x if fresh is None else jnp.where(fresh, 0.0, x)

    def conv_silu(col0, slab0, scale):
        outs = []
        for cb in range(W // LANES):
            slab = slab0 + cb
            cs = slice(slab * LANES, (slab + 1) * LANES)
            if fresh is not None:
                cbuf[slab, 0:halo, :] = carried(cbuf[slab, 0:halo, :])
            cbuf[slab, halo:halo + L, :] = zq[rows, col0 + cb * LANES:col0 + (cb + 1) * LANES].astype(F32)
            acc = convw_ref[taps - 1:taps, cs] * cbuf[slab, halo:halo + L, :]
            for j in range(taps - 1):
                off = halo - (taps - 1) + j
                acc = acc + convw_ref[j:j + 1, cs] * cbuf[slab, off:off + L, :]
            cbuf[slab, 0:halo, :] = cbuf[slab, L:L + halo, :]
            outs.append(acc * _sigmoid(acc) * scale if scale != 1.0 else acc * _sigmoid(acc))
        return outs

    q_slabs = conv_silu(0, 0, dh ** -0.5)
    k_slabs = conv_silu(W, W // LANES, 1.0)
    per_head = dh // LANES

    row = lax.broadcasted_iota(jnp.int32, (L, L), 0)
    col = lax.broadcasted_iota(jnp.int32, (L, L), 1)
    causal = row >= col
    tri_low = jnp.where(causal, 1.0, 0.0).astype(BF16)
    tri_up = jnp.where(row <= col, 1.0, 0.0).astype(BF16)
    b_cols = sum(_dot(tri_low, part) for part in _split3(_log_sigmoid(gc)))
    b_rows = sum(_dot(part, tri_up) for part in _split3(_log_sigmoid(gr)))

    for h in range(heads):
        hs = slice(h * dh, (h + 1) * dh)
        q = jnp.concatenate(q_slabs[h * per_head:(h + 1) * per_head], axis=-1)
        k = jnp.concatenate(k_slabs[h * per_head:(h + 1) * per_head], axis=-1)
        vb = zq[rows, 2 * W + h * dh:2 * W + (h + 1) * dh]
        v = vb.astype(F32)
        qb = q.astype(BF16)
        kb = k.astype(BF16)
        li_c = gc[:, h:h + 1]
        b_c = b_cols[:, heads + h:heads + h + 1]
        li_r = gr[h:h + 1, :]
        b_r = b_rows[heads + h:heads + h + 1, :]
        b_tot = b_r[:, L - 1:L]
        c_prev = carried(c_st[h])
        n_prev = carried(n_st[h])
        m_prev = carried(m_st[h][:, 0:1])

        d = jnp.where(causal, b_c - b_r + li_r, -jnp.inf)
        inter_log = b_c + m_prev
        m_comb = jnp.maximum(inter_log, jnp.max(d, axis=-1, keepdims=True))
        s = lax.dot_general(qb, kb, _NT, preferred_element_type=F32) * jnp.exp(d - m_comb)
        w_inter = jnp.exp(inter_log - m_comb)
        num = _dot(s.astype(BF16), vb) + w_inter * lax.dot_general(
            qb, c_prev.astype(BF16), _NT, preferred_element_type=F32)
        den = jnp.sum(s, axis=-1, keepdims=True) + w_inter * jnp.sum(q * n_prev, axis=-1, keepdims=True)
        den = jnp.maximum(jnp.abs(den), jnp.exp(-m_comb))
        hh = num / den
        hh = hh * lax.rsqrt(jnp.mean(hh * hh, axis=-1, keepdims=True) + EPS)
        gate = _sigmoid(zq[rows, 3 * W + h * dh:3 * W + (h + 1) * dh].astype(F32))
        out_ref[rows, hs] = (hh * gain_ref[:, hs] * gate).astype(out_ref.dtype)

        a = b_tot - b_c + li_c
        m_loc = jnp.max(a, axis=0, keepdims=True)
        w = jnp.exp(a - m_loc)
        c_loc = lax.dot_general((w * v).astype(BF16), kb, _TN, preferred_element_type=F32)
        n_loc = jnp.sum(w * k, axis=0, keepdims=True)
        m_new = jnp.maximum(b_tot + m_prev, m_loc)
        s_old = jnp.exp(b_tot + m_prev - m_new)
        s_loc = jnp.exp(m_loc - m_new)
        c_st[h] = s_old * c_prev + s_loc * c_loc
        n_st[h] = s_old * n_prev + s_loc * n_loc
        m_st[h] = jnp.broadcast_to(m_new, (1, LANES))


def _mixin_kernel(x_ref, g_ref, mod_ref, wq_ref, wr_ref, wif_ref, wift_ref, convw_ref, bcol_ref, brow_ref, gain_ref,
                  ha_ref, zr_ref, h_ref, zring, gcring, grring, cbuf, c_st, n_st, m_st,
                  *, heads, tiles_per_seq, col_chunk):
    s = pl.program_id(0)
    tm = x_ref.shape[0]
    W = ha_ref.shape[1]
    L = MLSTM_CHUNK
    n_chunks = tm // L
    qkvo = 4 * W
    slot_a = lax.rem(s, 2)
    slot_b = 1 - slot_a

    @pl.when(s == 0)
    def _():
        zring[1] = jnp.zeros(zring.shape[1:], zring.dtype)
        gcring[1] = jnp.zeros(gcring.shape[1:], F32)
        grring[1] = jnp.zeros(grring.shape[1:], F32)
        cbuf[...] = jnp.zeros_like(cbuf)
        c_st[...] = jnp.zeros_like(c_st)
        n_st[...] = jnp.zeros_like(n_st)
        m_st[...] = jnp.zeros_like(m_st)

    h = _norm_mod(x_ref[...], g_ref[...], mod_ref[0], mod_ref[1]).astype(BF16)
    h_ref[...] = h
    gcring[slot_a] = _dot(h, wif_ref[...])
    grring[slot_a] = lax.dot_general(wift_ref[...], h, _NT, preferred_element_type=F32)

    fresh = lax.rem(s + tiles_per_seq - 1, tiles_per_seq) == 0
    zq = zring.at[slot_b]
    col_starts = list(range(0, qkvo + wr_ref.shape[1], col_chunk))
    share = -(-len(col_starts) // n_chunks)
    for c in range(n_chunks):
        for c0 in col_starts[c * share:(c + 1) * share]:
            if c0 < qkvo:
                zring[slot_a, :, c0:c0 + col_chunk] = _dot(h_ref[...], wq_ref[:, c0:c0 + col_chunk]).astype(BF16)
            else:
                cr = slice(c0 - qkvo, c0 - qkvo + col_chunk)
                zr_ref[:, cr] = _dot(h_ref[...], wr_ref[:, cr]).astype(BF16)
        r0 = c * L
        gc = gcring[slot_b, r0:r0 + L, :] + bcol_ref[...]
        gr = grring[slot_b, :, r0:r0 + L] + brow_ref[...]
        _mlstm_chunk(zq, gc, gr, convw_ref, gain_ref, ha_ref, r0, cbuf, c_st, n_st, m_st,
                     heads=heads, fresh=fresh if c == 0 else None)


def _mixin(x, g, mod, w_qkvo, layer, w_rest, w_if, w_ift, conv_w, bias_col, bias_row, head_gain, width, heads, tm=512,
           col_chunk=256):
    b, t, d = x.shape
    n = b * t
    ncols = w_qkvo.shape[2] + w_rest.shape[1]
    ng = w_ift.shape[0]
    dh = width // heads
    tm = min(tm, t)
    nt = t // tm
    n_tiles = b * nt
    qkvo = 4 * width
    assert t % tm == 0 and tm % MLSTM_CHUNK == 0 and conv_w.shape[0] - 1 <= SUBLANES and dh % LANES == 0
    assert ncols % col_chunk == 0 and qkvo % col_chunk == 0 and w_qkvo.shape[2] == qkvo
    cur = lambda s: jnp.minimum(s, n_tiles - 1)
    prev = lambda s: jnp.maximum(s - 1, 0)
    const = lambda shape: pl.BlockSpec(shape, lambda s: (0,) * len(shape))
    return pl.pallas_call(
        functools.partial(_mixin_kernel, heads=heads, tiles_per_seq=nt, col_chunk=col_chunk),
        grid=(n_tiles + 1,),
        in_specs=[
            pl.BlockSpec((tm, d), lambda s: (cur(s), 0)),
            const((1, d)),
            pl.BlockSpec((None, 6, 1, d), lambda s: (cur(s) // nt, 0, 0, 0)),
            pl.BlockSpec((None, d, qkvo), lambda s: (layer, 0, 0), pipeline_mode=pl.Buffered(1)),
            pl.BlockSpec((d, ncols - qkvo), lambda s: (0, 0), pipeline_mode=pl.Buffered(1)),
            const((d, LANES)), const((ng, d)), const(conv_w.shape), const((1, LANES)), const((ng, 1)), const((1, width)),
        ],
        out_specs=[
            pl.BlockSpec((tm, width), lambda s: (prev(s), 0)),
            pl.BlockSpec((tm, ncols - qkvo), lambda s: (cur(s), 0)),
        ],
        out_shape=[
            jax.ShapeDtypeStruct((n, width), BF16),
            jax.ShapeDtypeStruct((n, ncols - qkvo), BF16),
        ],
        scratch_shapes=[
            pltpu.VMEM((tm, d), BF16),
            pltpu.VMEM((2, tm, qkvo), BF16),
            pltpu.VMEM((2, tm, LANES), F32),
            pltpu.VMEM((2, ng, tm), F32),
            pltpu.VMEM((2 * width // LANES, SUBLANES + MLSTM_CHUNK, LANES), F32),
            pltpu.VMEM((heads, dh, dh), F32),
            pltpu.VMEM((heads, 1, dh), F32),
            pltpu.VMEM((heads, 1, LANES), F32),
        ],
        compiler_params=_params(("arbitrary",)),
        name="mixin",
    )(x.reshape(n, d), g, mod, w_qkvo, w_rest, w_if, w_ift, conv_w, bias_col, bias_row, head_gain)


def _mixout_tile(ha_ref, u_ref, ga_ref, gb_ref, x_ref, mod_ref, poolw_ref, pscale_ref, pa_ref, pb_ref, wo_ref, ubuf,
                 windows):
    tm = x_ref.shape[0]
    gd = poolw_ref.shape[1]
    halo = max(windows)
    i = pl.program_id(1)

    @pl.when(i == 0)
    def _():
        ubuf[:, 0:halo, :] = jnp.zeros((ubuf.shape[0], halo, gd), F32)

    tpos = i * tm + lax.broadcasted_iota(jnp.int32, (tm, 1), 0)
    parts = []
    for g, win in enumerate(windows):
        cur = u_ref[:, g * gd:(g + 1) * gd].astype(F32)
        ubuf[g, halo:halo + tm, :] = cur
        wsum = cur
        for j in range(1, win):
            wsum = wsum + ubuf[g, halo - j:halo - j + tm, :]
        ubuf[g, 0:halo, :] = ubuf[g, tm:tm + halo, :]
        count = jnp.minimum(tpos + 1, win).astype(F32)
        pooled = wsum / count - cur
        parts.append(_dot(pooled.astype(BF16), poolw_ref[g]))
    hb = (jnp.concatenate(parts, axis=-1) * pscale_ref[...]).astype(BF16)

    pa = _dot(ha_ref[...], pa_ref[...])
    pb = _dot(hb, pb_ref[...])
    merged = _sigmoid(ga_ref[...].astype(F32)) * pa + _sigmoid(gb_ref[...].astype(F32)) * pb
    y = _dot(merged.astype(BF16), wo_ref[...])
    return x_ref[...] + mod_ref[2] * y


def _mixout_specs(h_a, zr, x, pool_w, pool_scale, proj_a, proj_b, w_out, tm):
    b, t, d = x.shape
    p = pool_scale.shape[1]
    width = h_a.shape[2]
    u_blk = 2 * d // p
    assert t % tm == 0 and (2 * d) % p == 0
    const = lambda shape: pl.BlockSpec(shape, lambda bi, i: (0,) * len(shape))
    specs = [
        pl.BlockSpec((None, tm, width), lambda bi, i: (bi, i, 0)),
        pl.BlockSpec((None, tm, p), lambda bi, i: (bi, i, u_blk)),
        pl.BlockSpec((None, tm, d), lambda bi, i: (bi, i, 0)),
        pl.BlockSpec((None, tm, d), lambda bi, i: (bi, i, 1)),
        pl.BlockSpec((None, tm, d), lambda bi, i: (bi, i, 0)),
        pl.BlockSpec((None, 6, 1, d), lambda bi, i: (bi, 0, 0, 0)),
        const(pool_w.shape), const(pool_scale.shape), const(proj_a.shape), const(proj_b.shape), const(w_out.shape),
    ]
    return specs, const


def _mixout_kernel(ha_ref, u_ref, ga_ref, gb_ref, x_ref, mod_ref, poolw_ref, pscale_ref, pa_ref, pb_ref, wo_ref,
                   out_ref, ubuf, *, windows):
    out_ref[...] = _mixout_tile(ha_ref, u_ref, ga_ref, gb_ref, x_ref, mod_ref, poolw_ref, pscale_ref, pa_ref, pb_ref,
                                wo_ref, ubuf, windows)


def _mixout(h_a, zr, x, mod, mix_w, tm=512):
    b, t, d = x.shape
    tm = min(tm, t)
    specs, _ = _mixout_specs(h_a, zr, x, *mix_w, tm)
    pool_w = mix_w[0]
    return pl.pallas_call(
        functools.partial(_mixout_kernel, windows=POOL_WINDOWS),
        grid=(b, t // tm),
        in_specs=specs,
        out_specs=pl.BlockSpec((None, tm, d), lambda bi, i: (bi, i, 0)),
        out_shape=jax.ShapeDtypeStruct((b, t, d), F32),
        scratch_shapes=[pltpu.VMEM((pool_w.shape[0], max(POOL_WINDOWS) + tm, pool_w.shape[1]), F32)],
        compiler_params=_params(("arbitrary", "arbitrary")),
        name="mixout",
    )(h_a, zr, zr, zr, x, mod, *mix_w)


def _mix_ffn_kernel(ha_ref, u_ref, ga_ref, gb_ref, x_ref, mod_ref, poolw_ref, pscale_ref, pa_ref, pb_ref, wo_ref,
                    g_ref, wg_ref, wu_ref, wd_ref, fin_ref, out_ref, ubuf, *, windows, final_norm, ff_chunk):
    x = _mixout_tile(ha_ref, u_ref, ga_ref, gb_ref, x_ref, mod_ref, poolw_ref, pscale_ref, pa_ref, pb_ref, wo_ref,
                     ubuf, windows)
    h = _norm_mod(x, g_ref[...], mod_ref[3], mod_ref[4]).astype(BF16)
    acc = None
    for c0 in range(0, wg_ref.shape[1], ff_chunk):
        gate = _dot(h, wg_ref[:, c0:c0 + ff_chunk])
        act = (gate * _sigmoid(gate) * _dot(h, wu_ref[:, c0:c0 + ff_chunk])).astype(BF16)
        part = _dot(act, wd_ref[c0:c0 + ff_chunk, :])
        acc = part if acc is None else acc + part
    y = x + mod_ref[5] * acc
    out_ref[...] = _rms(y, fin_ref[...]) if final_norm else y


def _mix_ffn(h_a, zr, x, mod, mix_w, g, w_gate, w_up, w_down, fin, final_norm, tm=512, ff_chunk=1408):
    b, t, d = x.shape
    ff = w_gate.shape[1]
    tm = min(tm, t)
    ff_chunk = min(ff_chunk, ff)
    assert ff % ff_chunk == 0
    specs, const = _mixout_specs(h_a, zr, x, *mix_w, tm)
    pool_w = mix_w[0]
    return pl.pallas_call(
        functools.partial(_mix_ffn_kernel, windows=POOL_WINDOWS, final_norm=final_norm, ff_chunk=ff_chunk),
        grid=(b, t // tm),
        in_specs=specs + [const((1, d)), const((d, ff)), const((d, ff)), const((ff, d)), const((1, d))],
        out_specs=pl.BlockSpec((None, tm, d), lambda bi, i: (bi, i, 0)),
        out_shape=jax.ShapeDtypeStruct((b, t, d), F32),
        scratch_shapes=[pltpu.VMEM((pool_w.shape[0], max(POOL_WINDOWS) + tm, pool_w.shape[1]), F32)],
        compiler_params=_params(("arbitrary", "arbitrary")),
        name="mix_ffn",
    )(h_a, zr, zr, zr, x, mod, *mix_w, g, w_gate, w_up, w_down, fin)


def _to_slabs(ref, x):
    rows, d = x.shape
    per = d // LANES
    for k in range(per):
        ref[pl.ds(k, rows, stride=per), :] = x[:, k * LANES:(k + 1) * LANES]


def _from_slabs(ref, rows):
    per = ref.shape[0] // rows
    return jnp.concatenate([ref[pl.ds(k, rows, stride=per), :] for k in range(per)], axis=-1)


def _router_kernel(x_ref, g_ref, mod_ref, rw_ref, rb_ref, h_ref, info_ref, cnt_ref, carry_ref, *, n_experts, cap):
    first = (pl.program_id(0) == 0) & (pl.program_id(1) == 0)

    @pl.when(first)
    def _():
        carry_ref[...] = jnp.zeros_like(carry_ref)

    h = _norm_mod(x_ref[...], g_ref[...], mod_ref[3], mod_ref[4])
    _to_slabs(h_ref, h)
    tm = h.shape[0]
    lane = lax.broadcasted_iota(jnp.int32, (tm, LANES), 1)
    logits = jnp.where(lane < n_experts, _dot_split(h, rw_ref[...]) + rb_ref[...], -jnp.inf)
    v1 = jnp.max(logits, axis=-1, keepdims=True)
    i1 = jnp.min(jnp.where(logits == v1, lane, LANES), axis=-1, keepdims=True)
    rest = jnp.where(lane == i1, -jnp.inf, logits)
    v2 = jnp.max(rest, axis=-1, keepdims=True)
    i2 = jnp.min(jnp.where(rest == v2, lane, LANES), axis=-1, keepdims=True)
    e2 = jnp.exp(v2 - v1)
    w1 = 1.0 / (1.0 + e2)
    w2 = e2 / (1.0 + e2)
    sel1 = lane == i1
    sel2 = lane == i2
    picked = jnp.where(sel1 | sel2, 1.0, 0.0)
    row = lax.broadcasted_iota(jnp.int32, (tm, tm), 0)
    col = lax.broadcasted_iota(jnp.int32, (tm, tm), 1)
    before = jnp.where(col < row, 1.0, 0.0).astype(BF16)
    ex = _dot(before, picked.astype(BF16)) + carry_ref[...]
    pos1 = i1.astype(F32) * cap + jnp.sum(jnp.where(sel1, ex, 0.0), axis=-1, keepdims=True)
    pos2 = i2.astype(F32) * cap + jnp.sum(jnp.where(sel2, ex, 0.0), axis=-1, keepdims=True)
    carry = carry_ref[...] + jnp.sum(picked, axis=0, keepdims=True)
    carry_ref[...] = carry
    cnt_ref[...] = carry
    info_ref[...] = (jnp.where(lane == 0, pos1, 0.0) + jnp.where(lane == 1, pos2, 0.0)
                     + jnp.where(lane == 2, w1, 0.0) + jnp.where(lane == 3, w2, 0.0))


def _router(x, g, mod, router_w, router_b, tm=512):
    b, t, d = x.shape
    n_experts = router_w.shape[1]
    tm = min(tm, t)
    nt = t // tm
    per = d // LANES
    assert t % tm == 0 and d % LANES == 0 and n_experts <= LANES and n_experts * b * t < 2 ** 24
    rw = jnp.zeros((d, LANES), F32).at[:, :n_experts].set(router_w)
    rb = jnp.zeros((1, LANES), F32).at[:, :n_experts].set(router_b[None, :])
    return pl.pallas_call(
        functools.partial(_router_kernel, n_experts=n_experts, cap=b * t),
        grid=(b, t // tm),
        in_specs=[
            pl.BlockSpec((None, tm, d), lambda bi, i: (bi, i, 0)),
            pl.BlockSpec((1, d), lambda bi, i: (0, 0)),
            pl.BlockSpec((None, 6, 1, d), lambda bi, i: (bi, 0, 0, 0)),
            pl.BlockSpec((d, LANES), lambda bi, i: (0, 0)),
            pl.BlockSpec((1, LANES), lambda bi, i: (0, 0)),
        ],
        out_specs=[
            pl.BlockSpec((tm * per, LANES), lambda bi, i: (bi * nt + i, 0)),
            pl.BlockSpec((None, tm, LANES), lambda bi, i: (bi, i, 0)),
            pl.BlockSpec((1, LANES), lambda bi, i: (0, 0)),
        ],
        out_shape=[jax.ShapeDtypeStruct((b * t * per, LANES), F32), jax.ShapeDtypeStruct((b, t, LANES), F32),
                   jax.ShapeDtypeStruct((1, LANES), F32)],
        scratch_shapes=[pltpu.VMEM((1, LANES), F32)],
        compiler_params=_params(("arbitrary", "arbitrary")),
        name="router",
    )(x, g, mod, rw, rb)


def _token_copy(src, src_tok, dst, dst_tok, sem, per):
    return pltpu.make_async_copy(src.at[pl.ds(pl.multiple_of(src_tok * per, per), per)],
                                 dst.at[pl.ds(pl.multiple_of(dst_tok * per, per), per)], sem)


def _scatter_kernel(pos_ref, h_ref, hs_hbm, sem, *, per):
    rows = h_ref.shape[0] // per
    base = pl.program_id(0) * rows

    def copies(r):
        t = base + r
        return (_token_copy(h_ref, r, hs_hbm, pos_ref[TOP_K * t], sem, per),
                _token_copy(h_ref, r, hs_hbm, pos_ref[TOP_K * t + 1], sem, per))

    def start(r, carry):
        for slot, cp in enumerate(copies(r)):
            cp.start(priority=slot)
        return carry

    def wait(r, carry):
        for cp in copies(r):
            cp.wait()
        return carry

    lax.fori_loop(0, rows, start, 0, unroll=8)
    lax.fori_loop(0, rows, wait, 0, unroll=8)


def _scatter_rows(pos, h, n, n_slots_out, rows=512):
    per = h.shape[0] // n
    rows = min(rows, n)
    assert n % rows == 0
    return pl.pallas_call(
        functools.partial(_scatter_kernel, per=per),
        grid_spec=pltpu.PrefetchScalarGridSpec(
            num_scalar_prefetch=1,
            grid=(n // rows,),
            in_specs=[pl.BlockSpec((rows * per, LANES), lambda i, pos: (i, 0))],
            out_specs=pl.BlockSpec(memory_space=pl.ANY),
            scratch_shapes=[pltpu.SemaphoreType.DMA],
        ),
        out_shape=jax.ShapeDtypeStruct((n_slots_out * per, LANES), h.dtype),
        compiler_params=_params(("arbitrary",)),
        name="moe_scatter",
    )(pos, h)


def _moe_ffn_kernel(te_ref, tb_ref, tv_ref, hs_ref, wg_ref, wu_ref, wd_ref, y_ref, xb_ref, acc_ref, *, n_steps):
    i = pl.program_id(0)
    j = pl.program_id(1)
    valid = tv_ref[i]

    nj = pl.num_programs(1)
    tm = xb_ref.shape[0]
    per = hs_ref.shape[0] // tm
    step = tm // MOE_TILE_PARTS

    def body(rows, first, last):
        if first:
            row = lax.broadcasted_iota(jnp.int32, (rows, 1), 0)
            x = _from_slabs(hs_ref.at[pl.ds(0, rows * per)], rows)
            h = jnp.where(row < valid, x, 0.0).astype(BF16)
            xb_ref[0:rows, :] = h
        else:
            h = xb_ref[0:rows, :]
        gate = _dot(h, wg_ref[...].astype(BF16))
        act = (gate * _sigmoid(gate) * _dot(h, wu_ref[...].astype(BF16))).astype(BF16)
        acc = _dot(act, wd_ref[...].astype(BF16))
        if not first:
            acc = acc_ref[0:rows, :] + acc
        if last:
            _to_slabs(y_ref.at[pl.ds(0, rows * per)], acc)
        else:
            acc_ref[0:rows, :] = acc

    for part in range(1, MOE_TILE_PARTS + 1):
        rows = part * step
        in_part = (valid > rows - step) & (valid <= rows)
        if n_steps == 1:
            pl.when(in_part)(functools.partial(body, rows, True, True))
        else:
            pl.when(in_part & (j == 0))(functools.partial(body, rows, True, False))
            pl.when(in_part & (j > 0) & (j < nj - 1))(functools.partial(body, rows, False, False))
            pl.when(in_part & (j == nj - 1))(functools.partial(body, rows, False, True))


def _moe_ffn(tile_e, tile_blk, tile_valid, hs, w_gate, w_up, w_down, tm, tf=512):
    n_experts, d, ff = w_gate.shape
    per = d // LANES
    tf = min(tf, ff)
    nf = ff // tf
    assert ff % tf == 0
    n_tiles = tile_e.shape[0]
    jj = lambda i, j, tv: jnp.where(tv[i] > 0, j, nf - 1)
    return pl.pallas_call(
        functools.partial(_moe_ffn_kernel, n_steps=nf),
        grid_spec=pltpu.PrefetchScalarGridSpec(
            num_scalar_prefetch=3,
            grid=(n_tiles, nf),
            in_specs=[
                pl.BlockSpec((tm * per, LANES), lambda i, j, te, tb, tv: (tb[i], 0)),
                pl.BlockSpec((None, d, tf), lambda i, j, te, tb, tv: (te[i], 0, jj(i, j, tv))),
                pl.BlockSpec((None, d, tf), lambda i, j, te, tb, tv: (te[i], 0, jj(i, j, tv))),
                pl.BlockSpec((None, tf, d), lambda i, j, te, tb, tv: (te[i], jj(i, j, tv), 0)),
            ],
            out_specs=pl.BlockSpec((tm * per, LANES), lambda i, j, te, tb, tv: (tb[i], 0)),
            scratch_shapes=[pltpu.VMEM((tm, d), BF16), pltpu.VMEM((tm, d), F32)],
        ),
        out_shape=jax.ShapeDtypeStruct(hs.shape, F32),
        compiler_params=_params(("arbitrary", "arbitrary")),
        name="moe_ffn",
    )(tile_e, tile_blk, tile_valid, hs, w_gate, w_up, w_down)


def _combine_kernel(pos_ref, x_ref, info_ref, mod_ref, fin_ref, y_hbm, out_ref, ybuf, sems, *, final_norm):
    rows = x_ref.shape[0]
    i = pl.program_id(0)
    per = ybuf.shape[2] // rows
    cur = lax.rem(i, 2)

    def copies(tile, ring, r):
        t = tile * rows + r
        return (_token_copy(y_hbm, pos_ref[TOP_K * t], ybuf.at[ring, 0], r, sems.at[ring], per),
                _token_copy(y_hbm, pos_ref[TOP_K * t + 1], ybuf.at[ring, 1], r, sems.at[ring], per))

    def issue(tile, ring):
        def start(r, carry):
            for slot, cp in enumerate(copies(tile, ring, r)):
                cp.start(priority=slot)
            return carry
        lax.fori_loop(0, rows, start, 0, unroll=8)

    def drain(tile, ring):
        def wait(r, carry):
            for cp in copies(tile, ring, r):
                cp.wait()
            return carry
        lax.fori_loop(0, rows, wait, 0, unroll=8)

    @pl.when(i == 0)
    def _():
        issue(0, 0)

    @pl.when(i + 1 < pl.num_programs(0))
    def _():
        issue(i + 1, 1 - cur)

    drain(i, cur)
    info = info_ref[...]
    f = info[:, 2:3] * _from_slabs(ybuf.at[cur, 0], rows) + info[:, 3:4] * _from_slabs(ybuf.at[cur, 1], rows)
    y = x_ref[...] + mod_ref[5] * f
    out_ref[...] = _rms(y, fin_ref[...]) if final_norm else y


def _combine(pos, x, info, mod, fin, y, final_norm, rows=256):
    b, t, d = x.shape
    n = b * t
    rows = min(rows, t)
    assert t % rows == 0
    per_b = t // rows
    return pl.pallas_call(
        functools.partial(_combine_kernel, final_norm=final_norm),
        grid_spec=pltpu.PrefetchScalarGridSpec(
            num_scalar_prefetch=1,
            grid=(n // rows,),
            in_specs=[
                pl.BlockSpec((rows, d), lambda i, pos: (i, 0)),
                pl.BlockSpec((rows, LANES), lambda i, pos: (i, 0)),
                pl.BlockSpec((None, 6, 1, d), lambda i, pos: (i // per_b, 0, 0, 0)),
                pl.BlockSpec((1, d), lambda i, pos: (0, 0)),
                pl.BlockSpec(memory_space=pl.ANY),
            ],
            out_specs=pl.BlockSpec((rows, d), lambda i, pos: (i, 0)),
            scratch_shapes=[pltpu.VMEM((2, TOP_K, rows * (d // LANES), LANES), F32), pltpu.SemaphoreType.DMA((2,))],
        ),
        out_shape=jax.ShapeDtypeStruct((n, d), F32),
        compiler_params=_params(("arbitrary",)),
        name="moe_combine",
    )(pos, x.reshape(n, d), info.reshape(n, LANES), mod, fin, y).reshape(b, t, d)


def _tile_tables(counts, n_experts, cap, tm, n_tiles):
    counts = counts.astype(jnp.int32)
    tiles_per = (counts + tm - 1) // tm
    ends = jnp.cumsum(tiles_per)
    used = ends[-1]
    i = jnp.minimum(jnp.arange(n_tiles, dtype=jnp.int32), used - 1)
    e = jnp.sum((i[:, None] >= ends[None, :]).astype(jnp.int32), axis=1)
    k = i - (ends - tiles_per)[e]
    valid = jnp.where(jnp.arange(n_tiles) < used, jnp.minimum(counts[e] - k * tm, tm), 0)
    return e, e * (cap // tm) + k, valid.astype(jnp.int32)


def _moe(x, h, info, counts, mod, w_gate, w_up, w_down, fin, final_norm, tm=1024):
    b, t, d = x.shape
    n = b * t
    n_experts = w_gate.shape[0]
    tm = min(tm, n)
    assert n % tm == 0 and tm % (MOE_TILE_PARTS * 2 * SUBLANES) == 0
    pos = info[:, :, :TOP_K].astype(jnp.int32).reshape(n * TOP_K)
    hs = _scatter_rows(pos, h, n, n_experts * n)
    n_tiles = TOP_K * n // tm + n_experts
    tile_e, tile_blk, tile_valid = _tile_tables(counts[0, :n_experts], n_experts, n, tm, n_tiles)
    y = _moe_ffn(tile_e, tile_blk, tile_valid, hs, w_gate, w_up, w_down, tm)
    return _combine(pos, x, info, mod, fin, y, final_norm)


def kernel(x, c, norm_mix, norm_ffn, w_ada, b_ada, w_in, conv_w, i_bias, f_bias, head_gain, pool_w, pool_scale,
           proj_a, proj_b, w_out, ffn_w_gate, ffn_w_up, ffn_w_down, router_w, router_b, moe_w_gate, moe_w_up,
           moe_w_down, final_norm):
    depth = w_in.shape[0]
    b, t, d = x.shape
    heads = i_bias.shape[1]
    width = head_gain.shape[1]
    p = pool_scale.shape[1]
    ng = 2 * heads
    qkvo = 4 * width

    mod_all = _adaln(c, w_ada, b_ada).reshape(depth, b, 6, 1, d)
    fin = final_norm.reshape(1, d)


    for l in range(depth):
        mod = mod_all[l]
        w_rest = jnp.concatenate([w_in[l, :, qkvo + ng + p:], w_in[l, :, qkvo + ng:qkvo + ng + p]], axis=1).astype(BF16)
        w_gates = w_in[l, :, qkvo:qkvo + ng]
        w_if = jnp.zeros((d, LANES), BF16).at[:, :ng].set(w_gates.astype(BF16))
        w_ift = w_gates.T.astype(BF16)
        bias = jnp.concatenate([i_bias[l], f_bias[l]])
        bias_col = jnp.zeros((1, LANES), F32).at[0, :ng].set(bias)
        bias_row = bias.reshape(ng, 1)

        w_qkvo = w_in[l, :, :qkvo].astype(BF16)[None]
        h_a, zr = _mixin(x, norm_mix[l].reshape(1, d), mod, w_qkvo, 0, w_rest, w_if, w_ift, conv_w[l], bias_col,
                         bias_row, head_gain[l].reshape(1, width), width, heads)
        h_a = h_a.reshape(b, t, width)
        zr = zr.reshape(b, t, -1)
        mix_w = (pool_w[l].astype(BF16), pool_scale[l].reshape(1, p), proj_a[l].astype(BF16), proj_b[l].astype(BF16),
                 w_out[l].astype(BF16))

        last = l == depth - 1
        j = l // 2
        g_ffn = norm_ffn[l].reshape(1, d)
        if l % 2 == 0:
            x = _mix_ffn(h_a, zr, x, mod, mix_w, g_ffn, ffn_w_gate[j].astype(BF16), ffn_w_up[j].astype(BF16),
                         ffn_w_down[j].astype(BF16), fin, last)
        else:
            x = _mixout(h_a, zr, x, mod, mix_w)
            h, info, counts = _router(x, g_ffn, mod, router_w[j], router_b[j])
            x = _moe(x, h, info, counts, mod, moe_w_gate[j], moe_w_up[j], moe_w_down[j], fin, last)
    return x
```

```python
import functools

import jax
import jax.numpy as jnp
from jax import lax
from jax.experimental import pallas as pl
from jax.experimental.pallas import tpu as pltpu

F32 = jnp.float32
BF16 = jnp.bfloat16

EPS = 1e-6
MLSTM_CHUNK = 256
POOL_WINDOWS = (2, 4, 8, 16)
TOP_K = 2
MOE_TILE_PARTS = 2
LANES = 128
SUBLANES = 8
VMEM_LIMIT = 56 * 1024 * 1024

_NT = (((1,), (1,)), ((), ()))
_TN = (((0,), (0,)), ((), ()))


def _params(sem):
    return pltpu.CompilerParams(dimension_semantics=sem, vmem_limit_bytes=VMEM_LIMIT)


def _sigmoid(x):
    return 1.0 / (1.0 + jnp.exp(-x))


def _log_sigmoid(x):
    return jnp.minimum(x, 0.0) - jnp.log(1.0 + jnp.exp(-jnp.abs(x)))


def _rms(x, g):
    return x * lax.rsqrt(jnp.mean(x * x, axis=-1, keepdims=True) + EPS) * g


def _norm_mod(x, g, shift, scale):
    return _rms(x, g) * (1.0 + scale) + shift


def _dot(a, b):
    return jnp.dot(a, b, preferred_element_type=F32)


def _split3(x):
    p0 = x.astype(BF16)
    r0 = x - p0.astype(F32)
    p1 = r0.astype(BF16)
    p2 = (r0 - p1.astype(F32)).astype(BF16)
    return p0, p1, p2


def _dot_split(a, b):
    a_hi = a.astype(BF16)
    a_lo = (a - a_hi.astype(F32)).astype(BF16)
    b_hi = b.astype(BF16)
    b_lo = (b - b_hi.astype(F32)).astype(BF16)
    return _dot(a_hi, b_hi) + (_dot(a_hi, b_lo) + _dot(a_lo, b_hi))


def _adaln_kernel(c_ref, w_ref, b_ref, o_ref):
    c = c_ref[...]
    o_ref[...] = _dot_split(c * _sigmoid(c), w_ref[...]) + b_ref[...]


def _adaln(c, w_ada, b_ada, tn=1536):
    depth, d, n = w_ada.shape
    b = c.shape[0]
    assert n % tn == 0
    return pl.pallas_call(
        _adaln_kernel,
        grid=(depth, n // tn),
        in_specs=[
            pl.BlockSpec((b, d), lambda l, j: (0, 0)),
            pl.BlockSpec((None, d, tn), lambda l, j: (l, 0, j)),
            pl.BlockSpec((None, 1, tn), lambda l, j: (l, 0, j)),
        ],
        out_specs=pl.BlockSpec((None, b, tn), lambda l, j: (l, 0, j)),
        out_shape=jax.ShapeDtypeStruct((depth, b, n), F32),
        compiler_params=_params(("arbitrary", "arbitrary")),
        name="adaln",
    )(c, w_ada, b_ada.reshape(depth, 1, n))


def _mlstm_chunk(zq, gc, gr, convw_ref, gain_ref, out_ref, r0, cbuf, c_st, n_st, m_st, *, heads, fresh):
    L = MLSTM_CHUNK
    W = out_ref.shape[1]
    dh = W // heads
    taps = convw_ref.shape[0]
    halo = SUBLANES
    rows = slice(r0, r0 + L)

    def carried(x):
        return x if fresh is None else jnp.where(fresh, 0.0, x)

    def conv_silu(col0, slab0, scale):
        outs = []
        for cb in range(W // LANES):
            slab = slab0 + cb
            cs = slice(slab * LANES, (slab + 1) * LANES)
            if fresh is not None:
                cbuf[slab, 0:halo, :] = carried(cbuf[slab, 0:halo, :])
            cbuf[slab, halo:halo + L, :] = zq[rows, col0 + cb * LANES:col0 + (cb + 1) * LANES].astype(F32)
            acc = convw_ref[taps - 1:taps, cs] * cbuf[slab, halo:halo + L, :]
            for j in range(taps - 1):
                off = halo - (taps - 1) + j
                acc = acc + convw_ref[j:j + 1, cs] * cbuf[slab, off:off + L, :]
            cbuf[slab, 0:halo, :] = cbuf[slab, L:L + halo, :]
            outs.append(acc * _sigmoid(acc) * scale if scale != 1.0 else acc * _sigmoid(acc))
        return outs

    q_slabs = conv_silu(0, 0, dh ** -0.5)
    k_slabs = conv_silu(W, W // LANES, 1.0)
    per_head = dh // LANES

    row = lax.broadcasted_iota(jnp.int32, (L, L), 0)
    col = lax.broadcasted_iota(jnp.int32, (L, L), 1)
    causal = row >= col
    tri_low = jnp.where(causal, 1.0, 0.0).astype(BF16)
    tri_up = jnp.where(row <= col, 1.0, 0.0).astype(BF16)
    b_cols = sum(_dot(tri_low, part) for part in _split3(_log_sigmoid(gc)))
    b_rows = sum(_dot(part, tri_up) for part in _split3(_log_sigmoid(gr)))

    for h in range(heads):
        hs = slice(h * dh, (h + 1) * dh)
        q = jnp.concatenate(q_slabs[h * per_head:(h + 1) * per_head], axis=-1)
        k = jnp.concatenate(k_slabs[h * per_head:(h + 1) * per_head], axis=-1)
        vb = zq[rows, 2 * W + h * dh:2 * W + (h + 1) * dh]
        v = vb.astype(F32)
        qb = q.astype(BF16)
        kb = k.astype(BF16)
        li_c = gc[:, h:h + 1]
        b_c = b_cols[:, heads + h:heads + h + 1]
        li_r = gr[h:h + 1, :]
        b_r = b_rows[heads + h:heads + h + 1, :]
        b_tot = b_r[:, L - 1:L]
        c_prev = carried(c_st[h])
        n_prev = carried(n_st[h])
        m_prev = carried(m_st[h][:, 0:1])

        d = jnp.where(causal, b_c - b_r + li_r, -jnp.inf)
        inter_log = b_c + m_prev
        m_comb = jnp.maximum(inter_log, jnp.max(d, axis=-1, keepdims=True))
        s = lax.dot_general(qb, kb, _NT, preferred_element_type=F32) * jnp.exp(d - m_comb)
        w_inter = jnp.exp(inter_log - m_comb)
        num = _dot(s.astype(BF16), vb) + w_inter * lax.dot_general(
            qb, c_prev.astype(BF16), _NT, preferred_element_type=F32)
        den = jnp.sum(s, axis=-1, keepdims=True) + w_inter * jnp.sum(q * n_prev, axis=-1, keepdims=True)
        den = jnp.maximum(jnp.abs(den), jnp.exp(-m_comb))
        hh = num / den
        hh = hh * lax.rsqrt(jnp.mean(hh * hh, axis=-1, keepdims=True) + EPS)
        gate = _sigmoid(zq[rows, 3 * W + h * dh:3 * W + (h + 1) * dh].astype(F32))
        out_ref[rows, hs] = (hh * gain_ref[:, hs] * gate).astype(out_ref.dtype)

        a = b_tot - b_c + li_c
        m_loc = jnp.max(a, axis=0, keepdims=True)
        w = jnp.exp(a - m_loc)
        c_loc = lax.dot_general((w * v).astype(BF16), kb, _TN, preferred_element_type=F32)
        n_loc = jnp.sum(w * k, axis=0, keepdims=True)
        m_new = jnp.maximum(b_tot + m_prev, m_loc)
        s_old = jnp.exp(b_tot + m_prev - m_new)
        s_loc = jnp.exp(m_loc - m_new)
        c_st[h] = s_old * c_prev + s_loc * c_loc
        n_st[h] = s_old * n_prev + s_loc * n_loc
        m_st[h] = jnp.broadcast_to(m_new, (1, LANES))


def _mixin_kernel(x_ref, g_ref, mod_ref, wq_ref, wr_ref, wif_ref, wift_ref, convw_ref, bcol_ref, brow_ref, gain_ref,
                  ha_ref, zr_ref, h_ref, zring, gcring, grring, cbuf, c_st, n_st, m_st,
                  *, heads, tiles_per_seq, col_chunk):
    s = pl.program_id(0)
    tm = x_ref.shape[0]
    W = ha_ref.shape[1]
    L = MLSTM_CHUNK
    n_chunks = tm // L
    qkvo = 4 * W
    slot_a = lax.rem(s, 2)
    slot_b = 1 - slot_a

    @pl.when(s == 0)
    def _():
        zring[1] = jnp.zeros(zring.shape[1:], zring.dtype)
        gcring[1] = jnp.zeros(gcring.shape[1:], F32)
        grring[1] = jnp.zeros(grring.shape[1:], F32)
        cbuf[...] = jnp.zeros_like(cbuf)
        c_st[...] = jnp.zeros_like(c_st)
        n_st[...] = jnp.zeros_like(n_st)
        m_st[...] = jnp.zeros_like(m_st)

    h = _norm_mod(x_ref[...], g_ref[...], mod_ref[0], mod_ref[1]).astype(BF16)
    h_ref[...] = h
    gcring[slot_a] = _dot(h, wif_ref[...])
    grring[slot_a] = lax.dot_general(wift_ref[...], h, _NT, preferred_element_type=F32)

    fresh = lax.rem(s + tiles_per_seq - 1, tiles_per_seq) == 0
    zq = zring.at[slot_b]
    col_starts = list(range(0, qkvo + wr_ref.shape[1], col_chunk))
    share = -(-len(col_starts) // n_chunks)
    for c in range(n_chunks):
        for c0 in col_starts[c * share:(c + 1) * share]:
            if c0 < qkvo:
                zring[slot_a, :, c0:c0 + col_chunk] = _dot(h_ref[...], wq_ref[:, c0:c0 + col_chunk]).astype(BF16)
            else:
                cr = slice(c0 - qkvo, c0 - qkvo + col_chunk)
                zr_ref[:, cr] = _dot(h_ref[...], wr_ref[:, cr]).astype(BF16)
        r0 = c * L
        gc = gcring[slot_b, r0:r0 + L, :] + bcol_ref[...]
        gr = grring[slot_b, :, r0:r0 + L] + brow_ref[...]
        _mlstm_chunk(zq, gc, gr, convw_ref, gain_ref, ha_ref, r0, cbuf, c_st, n_st, m_st,
                     heads=heads, fresh=fresh if c == 0 else None)


def _mixin(x, g, mod, w_qkvo, layer, w_rest, w_if, w_ift, conv_w, bias_col, bias_row, head_gain, width, heads, tm=512,
           col_chunk=256):
    b, t, d = x.shape
    n = b * t
    ncols = w_qkvo.shape[2] + w_rest.shape[1]
    ng = w_ift.shape[0]
    dh = width // heads
    tm = min(tm, t)
    nt = t // tm
    n_tiles = b * nt
    qkvo = 4 * width
    assert t % tm == 0 and tm % MLSTM_CHUNK == 0 and conv_w.shape[0] - 1 <= SUBLANES and dh % LANES == 0
    assert ncols % col_chunk == 0 and qkvo % col_chunk == 0 and w_qkvo.shape[2] == qkvo
    cur = lambda s: jnp.minimum(s, n_tiles - 1)
    prev = lambda s: jnp.maximum(s - 1, 0)
    const = lambda shape: pl.BlockSpec(shape, lambda s: (0,) * len(shape))
    return pl.pallas_call(
        functools.partial(_mixin_kernel, heads=heads, tiles_per_seq=nt, col_chunk=col_chunk),
        grid=(n_tiles + 1,),
        in_specs=[
            pl.BlockSpec((tm, d), lambda s: (cur(s), 0)),
            const((1, d)),
            pl.BlockSpec((None, 6, 1, d), lambda s: (cur(s) // nt, 0, 0, 0)),
            pl.BlockSpec((None, d, qkvo), lambda s: (layer, 0, 0), pipeline_mode=pl.Buffered(1)),
            pl.BlockSpec((d, ncols - qkvo), lambda s: (0, 0), pipeline_mode=pl.Buffered(1)),
            const((d, LANES)), const((ng, d)), const(conv_w.shape), const((1, LANES)), const((ng, 1)), const((1, width)),
        ],
        out_specs=[
            pl.BlockSpec((tm, width), lambda s: (prev(s), 0)),
            pl.BlockSpec((tm, ncols - qkvo), lambda s: (cur(s), 0)),
        ],
        out_shape=[
            jax.ShapeDtypeStruct((n, width), BF16),
            jax.ShapeDtypeStruct((n, ncols - qkvo), BF16),
        ],
        scratch_shapes=[
            pltpu.VMEM((tm, d), BF16),
            pltpu.VMEM((2, tm, qkvo), BF16),
            pltpu.VMEM((2, tm, LANES), F32),
            pltpu.VMEM((2, ng, tm), F32),
            pltpu.VMEM((2 * width // LANES, SUBLANES + MLSTM_CHUNK, LANES), F32),
            pltpu.VMEM((heads, dh, dh), F32),
            pltpu.VMEM((heads, 1, dh), F32),
            pltpu.VMEM((heads, 1, LANES), F32),
        ],
        compiler_params=_params(("arbitrary",)),
        name="mixin",
    )(x.reshape(n, d), g, mod, w_qkvo, w_rest, w_if, w_ift, conv_w, bias_col, bias_row, head_gain)


def _mixout_tile(ha_ref, u_ref, ga_ref, gb_ref, x_ref, mod_ref, poolw_ref, pscale_ref, pa_ref, pb_ref, wo_ref, ubuf,
                 windows):
    tm = x_ref.shape[0]
    gd = poolw_ref.shape[1]
    halo = max(windows)
    i = pl.program_id(1)

    @pl.when(i == 0)
    def _():
        ubuf[:, 0:halo, :] = jnp.zeros((ubuf.shape[0], halo, gd), F32)

    tpos = i * tm + lax.broadcasted_iota(jnp.int32, (tm, 1), 0)
    parts = []
    for g, win in enumerate(windows):
        cur = u_ref[:, g * gd:(g + 1) * gd].astype(F32)
        ubuf[g, halo:halo + tm, :] = cur
        wsum = cur
        for j in range(1, win):
            wsum = wsum + ubuf[g, halo - j:halo - j + tm, :]
        ubuf[g, 0:halo, :] = ubuf[g, tm:tm + halo, :]
        count = jnp.minimum(tpos + 1, win).astype(F32)
        pooled = wsum / count - cur
        parts.append(_dot(pooled.astype(BF16), poolw_ref[g]))
    hb = (jnp.concatenate(parts, axis=-1) * pscale_ref[...]).astype(BF16)

    pa = _dot(ha_ref[...], pa_ref[...])
    pb = _dot(hb, pb_ref[...])
    merged = _sigmoid(ga_ref[...].astype(F32)) * pa + _sigmoid(gb_ref[...].astype(F32)) * pb
    y = _dot(merged.astype(BF16), wo_ref[...])
    return x_ref[...] + mod_ref[2] * y


def _mixout_specs(h_a, zr, x, pool_w, pool_scale, proj_a, proj_b, w_out, tm):
    b, t, d = x.shape
    p = pool_scale.shape[1]
    width = h_a.shape[2]
    u_blk = 2 * d // p
    assert t % tm == 0 and (2 * d) % p == 0
    const = lambda shape: pl.BlockSpec(shape, lambda bi, i: (0,) * len(shape))
    specs = [
        pl.BlockSpec((None, tm, width), lambda bi, i: (bi, i, 0)),
        pl.BlockSpec((None, tm, p), lambda bi, i: (bi, i, u_blk)),
        pl.BlockSpec((None, tm, d), lambda bi, i: (bi, i, 0)),
        pl.BlockSpec((None, tm, d), lambda bi, i: (bi, i, 1)),
        pl.BlockSpec((None, tm, d), lambda bi, i: (bi, i, 0)),
        pl.BlockSpec((None, 6, 1, d), lambda bi, i: (bi, 0, 0, 0)),
        const(pool_w.shape), const(pool_scale.shape), const(proj_a.shape), const(proj_b.shape), const(w_out.shape),
    ]
    return specs, const


def _mixout_kernel(ha_ref, u_ref, ga_ref, gb_ref, x_ref, mod_ref, poolw_ref, pscale_ref, pa_ref, pb_ref, wo_ref,
                   out_ref, ubuf, *, windows):
    out_ref[...] = _mixout_tile(ha_ref, u_ref, ga_ref, gb_ref, x_ref, mod_ref, poolw_ref, pscale_ref, pa_ref, pb_ref,
                                wo_ref, ubuf, windows)


def _mixout(h_a, zr, x, mod, mix_w, tm=512):
    b, t, d = x.shape
    tm = min(tm, t)
    specs, _ = _mixout_specs(h_a, zr, x, *mix_w, tm)
    pool_w = mix_w[0]
    return pl.pallas_call(
        functools.partial(_mixout_kernel, windows=POOL_WINDOWS),
        grid=(b, t // tm),
        in_specs=specs,
        out_specs=pl.BlockSpec((None, tm, d), lambda bi, i: (bi, i, 0)),
        out_shape=jax.ShapeDtypeStruct((b, t, d), F32),
        scratch_shapes=[pltpu.VMEM((pool_w.shape[0], max(POOL_WINDOWS) + tm, pool_w.shape[1]), F32)],
        compiler_params=_params(("arbitrary", "arbitrary")),
        name="mixout",
    )(h_a, zr, zr, zr, x, mod, *mix_w)


def _mix_ffn_kernel(ha_ref, u_ref, ga_ref, gb_ref, x_ref, mod_ref, poolw_ref, pscale_ref, pa_ref, pb_ref, wo_ref,
                    g_ref, wg_ref, wu_ref, wd_ref, fin_ref, out_ref, ubuf, *, windows, final_norm, ff_chunk):
    x = _mixout_tile(ha_ref, u_ref, ga_ref, gb_ref, x_ref, mod_ref, poolw_ref, pscale_ref, pa_ref, pb_ref, wo_ref,
                     ubuf, windows)
    h = _norm_mod(x, g_ref[...], mod_ref[3], mod_ref[4]).astype(BF16)
    acc = None
    for c0 in range(0, wg_ref.shape[1], ff_chunk):
        gate = _dot(h, wg_ref[:, c0:c0 + ff_chunk])
        act = (gate * _sigmoid(gate) * _dot(h, wu_ref[:, c0:c0 + ff_chunk])).astype(BF16)
        part = _dot(act, wd_ref[c0:c0 + ff_chunk, :])
        acc = part if acc is None else acc + part
    y = x + mod_ref[5] * acc
    out_ref[...] = _rms(y, fin_ref[...]) if final_norm else y


def _mix_ffn(h_a, zr, x, mod, mix_w, g, w_gate, w_up, w_down, fin, final_norm, tm=512, ff_chunk=1408):
    b, t, d = x.shape
    ff = w_gate.shape[1]
    tm = min(tm, t)
    ff_chunk = min(ff_chunk, ff)
    assert ff % ff_chunk == 0
    specs, const = _mixout_specs(h_a, zr, x, *mix_w, tm)
    pool_w = mix_w[0]
    return pl.pallas_call(
        functools.partial(_mix_ffn_kernel, windows=POOL_WINDOWS, final_norm=final_norm, ff_chunk=ff_chunk),
        grid=(b, t // tm),
        in_specs=specs + [const((1, d)), const((d, ff)), const((d, ff)), const((ff, d)), const((1, d))],
        out_specs=pl.BlockSpec((None, tm, d), lambda bi, i: (bi, i, 0)),
        out_shape=jax.ShapeDtypeStruct((b, t, d), F32),
        scratch_shapes=[pltpu.VMEM((pool_w.shape[0], max(POOL_WINDOWS) + tm, pool_w.shape[1]), F32)],
        compiler_params=_params(("arbitrary", "arbitrary")),
        name="mix_ffn",
    )(h_a, zr, zr, zr, x, mod, *mix_w, g, w_gate, w_up, w_down, fin)


def _to_slabs(ref, x):
    rows, d = x.shape
    per = d // LANES
    for k in range(per):
        ref[pl.ds(k, rows, stride=per), :] = x[:, k * LANES:(k + 1) * LANES]


def _from_slabs(ref, rows):
    per = ref.shape[0] // rows
    return jnp.concatenate([ref[pl.ds(k, rows, stride=per), :] for k in range(per)], axis=-1)


def _router_kernel(x_ref, g_ref, mod_ref, rw_ref, rb_ref, h_ref, info_ref, cnt_ref, carry_ref, *, n_experts, cap):
    first = (pl.program_id(0) == 0) & (pl.program_id(1) == 0)

    @pl.when(first)
    def _():
        carry_ref[...] = jnp.zeros_like(carry_ref)

    h = _norm_mod(x_ref[...], g_ref[...], mod_ref[3], mod_ref[4])
    _to_slabs(h_ref, h)
    tm = h.shape[0]
    lane = lax.broadcasted_iota(jnp.int32, (tm, LANES), 1)
    logits = jnp.where(lane < n_experts, _dot_split(h, rw_ref[...]) + rb_ref[...], -jnp.inf)
    v1 = jnp.max(logits, axis=-1, keepdims=True)
    i1 = jnp.min(jnp.where(logits == v1, lane, LANES), axis=-1, keepdims=True)
    rest = jnp.where(lane == i1, -jnp.inf, logits)
    v2 = jnp.max(rest, axis=-1, keepdims=True)
    i2 = jnp.min(jnp.where(rest == v2, lane, LANES), axis=-1, keepdims=True)
    e2 = jnp.exp(v2 - v1)
    w1 = 1.0 / (1.0 + e2)
    w2 = e2 / (1.0 + e2)
    sel1 = lane == i1
    sel2 = lane == i2
    picked = jnp.where(sel1 | sel2, 1.0, 0.0)
    row = lax.broadcasted_iota(jnp.int32, (tm, tm), 0)
    col = lax.broadcasted_iota(jnp.int32, (tm, tm), 1)
    before = jnp.where(col < row, 1.0, 0.0).astype(BF16)
    ex = _dot(before, picked.astype(BF16)) + carry_ref[...]
    pos1 = i1.astype(F32) * cap + jnp.sum(jnp.where(sel1, ex, 0.0), axis=-1, keepdims=True)
    pos2 = i2.astype(F32) * cap + jnp.sum(jnp.where(sel2, ex, 0.0), axis=-1, keepdims=True)
    carry = carry_ref[...] + jnp.sum(picked, axis=0, keepdims=True)
    carry_ref[...] = carry
    cnt_ref[...] = carry
    info_ref[...] = (jnp.where(lane == 0, pos1, 0.0) + jnp.where(lane == 1, pos2, 0.0)
                     + jnp.where(lane == 2, w1, 0.0) + jnp.where(lane == 3, w2, 0.0))


def _router(x, g, mod, router_w, router_b, tm=512):
    b, t, d = x.shape
    n_experts = router_w.shape[1]
    tm = min(tm, t)
    nt = t // tm
    per = d // LANES
    assert t % tm == 0 and d % LANES == 0 and n_experts <= LANES and n_experts * b * t < 2 ** 24
    rw = jnp.zeros((d, LANES), F32).at[:, :n_experts].set(router_w)
    rb = jnp.zeros((1, LANES), F32).at[:, :n_experts].set(router_b[None, :])
    return pl.pallas_call(
        functools.partial(_router_kernel, n_experts=n_experts, cap=b * t),
        grid=(b, t // tm),
        in_specs=[
            pl.BlockSpec((None, tm, d), lambda bi, i: (bi, i, 0)),
            pl.BlockSpec((1, d), lambda bi, i: (0, 0)),
            pl.BlockSpec((None, 6, 1, d), lambda bi, i: (bi, 0, 0, 0)),
            pl.BlockSpec((d, LANES), lambda bi, i: (0, 0)),
            pl.BlockSpec((1, LANES), lambda bi, i: (0, 0)),
        ],
        out_specs=[
            pl.BlockSpec((tm * per, LANES), lambda bi, i: (bi * nt + i, 0)),
            pl.BlockSpec((None, tm, LANES), lambda bi, i: (bi, i, 0)),
            pl.BlockSpec((1, LANES), lambda bi, i: (0, 0)),
        ],
        out_shape=[jax.ShapeDtypeStruct((b * t * per, LANES), F32), jax.ShapeDtypeStruct((b, t, LANES), F32),
                   jax.ShapeDtypeStruct((1, LANES), F32)],
        scratch_shapes=[pltpu.VMEM((1, LANES), F32)],
        compiler_params=_params(("arbitrary", "arbitrary")),
        name="router",
    )(x, g, mod, rw, rb)


def _token_copy(src, src_tok, dst, dst_tok, sem, per):
    return pltpu.make_async_copy(src.at[pl.ds(pl.multiple_of(src_tok * per, per), per)],
                                 dst.at[pl.ds(pl.multiple_of(dst_tok * per, per), per)], sem)


def _scatter_kernel(pos_ref, h_ref, hs_hbm, sem, *, per):
    rows = h_ref.shape[0] // per
    base = pl.program_id(0) * rows

    def copies(r):
        t = base + r
        return (_token_copy(h_ref, r, hs_hbm, pos_ref[TOP_K * t], sem, per),
                _token_copy(h_ref, r, hs_hbm, pos_ref[TOP_K * t + 1], sem, per))

    def start(r, carry):
        for slot, cp in enumerate(copies(r)):
            cp.start(priority=slot)
        return carry

    def wait(r, carry):
        for cp in copies(r):
            cp.wait()
        return carry

    lax.fori_loop(0, rows, start, 0, unroll=8)
    lax.fori_loop(0, rows, wait, 0, unroll=8)


def _scatter_rows(pos, h, n, n_slots_out, rows=512):
    per = h.shape[0] // n
    rows = min(rows, n)
    assert n % rows == 0
    return pl.pallas_call(
        functools.partial(_scatter_kernel, per=per),
        grid_spec=pltpu.PrefetchScalarGridSpec(
            num_scalar_prefetch=1,
            grid=(n // rows,),
            in_specs=[pl.BlockSpec((rows * per, LANES), lambda i, pos: (i, 0))],
            out_specs=pl.BlockSpec(memory_space=pl.ANY),
            scratch_shapes=[pltpu.SemaphoreType.DMA],
        ),
        out_shape=jax.ShapeDtypeStruct((n_slots_out * per, LANES), h.dtype),
        compiler_params=_params(("arbitrary",)),
        name="moe_scatter",
    )(pos, h)


def _moe_ffn_kernel(te_ref, tb_ref, tv_ref, hs_ref, wg_ref, wu_ref, wd_ref, y_ref, xb_ref, acc_ref, *, n_steps):
    i = pl.program_id(0)
    j = pl.program_id(1)
    valid = tv_ref[i]

    nj = pl.num_programs(1)
    tm = xb_ref.shape[0]
    per = hs_ref.shape[0] // tm
    step = tm // MOE_TILE_PARTS

    def body(rows, first, last):
        if first:
            row = lax.broadcasted_iota(jnp.int32, (rows, 1), 0)
            x = _from_slabs(hs_ref.at[pl.ds(0, rows * per)], rows)
            h = jnp.where(row < valid, x, 0.0).astype(BF16)
            xb_ref[0:rows, :] = h
        else:
            h = xb_ref[0:rows, :]
        gate = _dot(h, wg_ref[...].astype(BF16))
        act = (gate * _sigmoid(gate) * _dot(h, wu_ref[...].astype(BF16))).astype(BF16)
        acc = _dot(act, wd_ref[...].astype(BF16))
        if not first:
            acc = acc_ref[0:rows, :] + acc
        if last:
            _to_slabs(y_ref.at[pl.ds(0, rows * per)], acc)
        else:
            acc_ref[0:rows, :] = acc

    for part in range(1, MOE_TILE_PARTS + 1):
        rows = part * step
        in_part = (valid > rows - step) & (valid <= rows)
        if n_steps == 1:
            pl.when(in_part)(functools.partial(body, rows, True, True))
        else:
            pl.when(in_part & (j == 0))(functools.partial(body, rows, True, False))
            pl.when(in_part & (j > 0) & (j < nj - 1))(functools.partial(body, rows, False, False))
            pl.when(in_part & (j == nj - 1))(functools.partial(body, rows, False, True))


def _moe_ffn(tile_e, tile_blk, tile_valid, hs, w_gate, w_up, w_down, tm, tf=512):
    n_experts, d, ff = w_gate.shape
    per = d // LANES
    tf = min(tf, ff)
    nf = ff // tf
    assert ff % tf == 0
    n_tiles = tile_e.shape[0]
    jj = lambda i, j, tv: jnp.where(tv[i] > 0, j, nf - 1)
    return pl.pallas_call(
        functools.partial(_moe_ffn_kernel, n_steps=nf),
        grid_spec=pltpu.PrefetchScalarGridSpec(
            num_scalar_prefetch=3,
            grid=(n_tiles, nf),
            in_specs=[
                pl.BlockSpec((tm * per, LANES), lambda i, j, te, tb, tv: (tb[i], 0)),
                pl.BlockSpec((None, d, tf), lambda i, j, te, tb, tv: (te[i], 0, jj(i, j, tv))),
                pl.BlockSpec((None, d, tf), lambda i, j, te, tb, tv: (te[i], 0, jj(i, j, tv))),
                pl.BlockSpec((None, tf, d), lambda i, j, te, tb, tv: (te[i], jj(i, j, tv), 0)),
            ],
            out_specs=pl.BlockSpec((tm * per, LANES), lambda i, j, te, tb, tv: (tb[i], 0)),
            scratch_shapes=[pltpu.VMEM((tm, d), BF16), pltpu.VMEM((tm, d), F32)],
        ),
        out_shape=jax.ShapeDtypeStruct(hs.shape, F32),
        compiler_params=_params(("arbitrary", "arbitrary")),
        name="moe_ffn",
    )(tile_e, tile_blk, tile_valid, hs, w_gate, w_up, w_down)


def _combine_kernel(pos_ref, x_ref, info_ref, mod_ref, fin_ref, y_hbm, out_ref, ybuf, sems, *, final_norm):
    rows = x_ref.shape[0]
    i = pl.program_id(0)
    per = ybuf.shape[2] // rows
    cur = lax.rem(i, 2)

    def copies(tile, ring, r):
        t = tile * rows + r
        return (_token_copy(y_hbm, pos_ref[TOP_K * t], ybuf.at[ring, 0], r, sems.at[ring], per),
                _token_copy(y_hbm, pos_ref[TOP_K * t + 1], ybuf.at[ring, 1], r, sems.at[ring], per))

    def issue(tile, ring):
        def start(r, carry):
            for slot, cp in enumerate(copies(tile, ring, r)):
                cp.start(priority=slot)
            return carry
        lax.fori_loop(0, rows, start, 0, unroll=8)

    def drain(tile, ring):
        def wait(r, carry):
            for cp in copies(tile, ring, r):
                cp.wait()
            return carry
        lax.fori_loop(0, rows, wait, 0, unroll=8)

    @pl.when(i == 0)
    def _():
        issue(0, 0)

    @pl.when(i + 1 < pl.num_programs(0))
    def _():
        issue(i + 1, 1 - cur)

    drain(i, cur)
    info = info_ref[...]
    f = info[:, 2:3] * _from_slabs(ybuf.at[cur, 0], rows) + info[:, 3:4] * _from_slabs(ybuf.at[cur, 1], rows)
    y = x_ref[...] + mod_ref[5] * f
    out_ref[...] = _rms(y, fin_ref[...]) if final_norm else y


def _combine(pos, x, info, mod, fin, y, final_norm, rows=256):
    b, t, d = x.shape
    n = b * t
    rows = min(rows, t)
    assert t % rows == 0
    per_b = t // rows
    return pl.pallas_call(
        functools.partial(_combine_kernel, final_norm=final_norm),
        grid_spec=pltpu.PrefetchScalarGridSpec(
            num_scalar_prefetch=1,
            grid=(n // rows,),
            in_specs=[
                pl.BlockSpec((rows, d), lambda i, pos: (i, 0)),
                pl.BlockSpec((rows, LANES), lambda i, pos: (i, 0)),
                pl.BlockSpec((None, 6, 1, d), lambda i, pos: (i // per_b, 0, 0, 0)),
                pl.BlockSpec((1, d), lambda i, pos: (0, 0)),
                pl.BlockSpec(memory_space=pl.ANY),
            ],
            out_specs=pl.BlockSpec((rows, d), lambda i, pos: (i, 0)),
            scratch_shapes=[pltpu.VMEM((2, TOP_K, rows * (d // LANES), LANES), F32), pltpu.SemaphoreType.DMA((2,))],
        ),
        out_shape=jax.ShapeDtypeStruct((n, d), F32),
        compiler_params=_params(("arbitrary",)),
        name="moe_combine",
    )(pos, x.reshape(n, d), info.reshape(n, LANES), mod, fin, y).reshape(b, t, d)


def _tile_tables(counts, n_experts, cap, tm, n_tiles):
    counts = counts.astype(jnp.int32)
    tiles_per = (counts + tm - 1) // tm
    ends = jnp.cumsum(tiles_per)
    used = ends[-1]
    i = jnp.minimum(jnp.arange(n_tiles, dtype=jnp.int32), used - 1)
    e = jnp.sum((i[:, None] >= ends[None, :]).astype(jnp.int32), axis=1)
    k = i - (ends - tiles_per)[e]
    valid = jnp.where(jnp.arange(n_tiles) < used, jnp.minimum(counts[e] - k * tm, tm), 0)
    return e, e * (cap // tm) + k, valid.astype(jnp.int32)


def _moe(x, h, info, counts, mod, w_gate, w_up, w_down, fin, final_norm, tm=1024):
    b, t, d = x.shape
    n = b * t
    n_experts = w_gate.shape[0]
    tm = min(tm, n)
    assert n % tm == 0 and tm % (MOE_TILE_PARTS * 2 * SUBLANES) == 0
    pos = info[:, :, :TOP_K].astype(jnp.int32).reshape(n * TOP_K)
    hs = _scatter_rows(pos, h, n, n_experts * n)
    n_tiles = TOP_K * n // tm + n_experts
    tile_e, tile_blk, tile_valid = _tile_tables(counts[0, :n_experts], n_experts, n, tm, n_tiles)
    y = _moe_ffn(tile_e, tile_blk, tile_valid, hs, w_gate, w_up, w_down, tm)
    return _combine(pos, x, info, mod, fin, y, final_norm)


def kernel(x, c, norm_mix, norm_ffn, w_ada, b_ada, w_in, conv_w, i_bias, f_bias, head_gain, pool_w, pool_scale,
           proj_a, proj_b, w_out, ffn_w_gate, ffn_w_up, ffn_w_down, router_w, router_b, moe_w_gate, moe_w_up,
           moe_w_down, final_norm):
    depth = w_in.shape[0]
    b, t, d = x.shape
    heads = i_bias.shape[1]
    width = head_gain.shape[1]
    p = pool_scale.shape[1]
    ng = 2 * heads
    qkvo = 4 * width

    mod_all = _adaln(c, w_ada, b_ada).reshape(depth, b, 6, 1, d)
    fin = final_norm.reshape(1, d)


    for l in range(depth):
        mod = mod_all[l]
        w_rest = jnp.concatenate([w_in[l, :, qkvo + ng + p:], w_in[l, :, qkvo + ng:qkvo + ng + p]], axis=1).astype(BF16)
        w_gates = w_in[l, :, qkvo:qkvo + ng]
        w_if = jnp.zeros((d, LANES), BF16).at[:, :ng].set(w_gates.astype(BF16))
        w_ift = w_gates.T.astype(BF16)
        bias = jnp.concatenate([i_bias[l], f_bias[l]])
        bias_col = jnp.zeros((1, LANES), F32).at[0, :ng].set(bias)
        bias_row = bias.reshape(ng, 1)

        w_qkvo = w_in[l, :, :qkvo].astype(BF16)[None]
        h_a, zr = _mixin(x, norm_mix[l].reshape(1, d), mod, w_qkvo, 0, w_rest, w_if, w_ift, conv_w[l], bias_col,
                         bias_row, head_gain[l].reshape(1, width), width, heads)
        h_a = h_a.reshape(b, t, width)
        zr = zr.reshape(b, t, -1)
        mix_w = (pool_w[l].astype(BF16), pool_scale[l].reshape(1, p), proj_a[l].astype(BF16), proj_b[l].astype(BF16),
                 w_out[l].astype(BF16))

        last = l == depth - 1
        j = l // 2
        g_ffn = norm_ffn[l].reshape(1, d)
        if l % 2 == 0:
            x = _mix_ffn(h_a, zr, x, mod, mix_w, g_ffn, ffn_w_gate[j].astype(BF16), ffn_w_up[j].astype(BF16),
                         ffn_w_down[j].astype(BF16), fin, last)
        else:
            x = _mixout(h_a, zr, x, mod, mix_w)
            h, info, counts = _router(x, g_ffn, mod, router_w[j], router_b[j])
            x = _moe(x, h, info, counts, mod, moe_w_gate[j], moe_w_up[j], moe_w_down[j], fin, last)
    return x
```
